```python
import jax, jax.numpy as jnp
from jax import lax
import numpy as np

D_MODEL = 1024
BATCH = 4
SEQ = 4096
DEPTH = 1

CHUNK = 64
N_PREV_CHUNKS = 8
BAND_CHUNKS = N_PREV_CHUNKS + 1
N_HEADS = 16
HEAD_DIM = 64
D_ATTN = N_HEADS * HEAD_DIM
D_CONV = D_MODEL
CONV_WIDTH = 3
MAX_REL = 256
D_FF = 4 * D_MODEL
N_BRANCHES = 2
EPS = 1e-6
NEG_INF = -1e30

kernel_name = "chunk_causal_hybrid_attn_shortconv_block"


def rms_norm(x, g):
    xf = x.astype(jnp.float32)
    y = xf * lax.rsqrt(jnp.mean(xf * xf, axis=-1, keepdims=True) + EPS)
    return (y * g.astype(jnp.float32)).astype(x.dtype)


def chunk_band(t):
    b, nc, c, h, dh = t.shape
    tp = jnp.pad(t, ((0, 0), (N_PREV_CHUNKS, 0), (0, 0), (0, 0), (0, 0)))
    band = jnp.stack([tp[:, o:o + nc] for o in range(BAND_CHUNKS)], axis=2)
    return band.reshape(b, nc, BAND_CHUNKS * c, h, dh)


def chunked_relpos_attention(q, k, v, q_norm_g, k_norm_g, rel_bias):
    b, s, _ = q.shape
    nc = s // CHUNK
    kw = BAND_CHUNKS * CHUNK
    q = rms_norm(q.reshape(b, nc, CHUNK, N_HEADS, HEAD_DIM), q_norm_g)
    k = rms_norm(k.reshape(b, nc, CHUNK, N_HEADS, HEAD_DIM), k_norm_g)
    v = v.reshape(b, nc, CHUNK, N_HEADS, HEAD_DIM)
    kb = chunk_band(k)
    vb = chunk_band(v)

    q_idx = jnp.arange(CHUNK)[:, None]
    k_idx = jnp.arange(kw)[None, :]
    dist = q_idx - k_idx + N_PREV_CHUNKS * CHUNK
    rel_idx = jnp.clip(dist, -MAX_REL, MAX_REL) + MAX_REL
    bias = rel_bias[:, rel_idx].astype(jnp.float32)

    key_chunk = jnp.arange(nc)[:, None] + (jnp.arange(kw) // CHUNK)[None, :] - N_PREV_CHUNKS
    valid = key_chunk >= 0

    scale = HEAD_DIM ** -0.5
    scores = jnp.einsum('bnqhd,bnkhd->bnhqk', q, kb).astype(jnp.float32) * scale
    scores = scores + bias[None, None]
    scores = jnp.where(valid[None, :, None, None, :], scores, NEG_INF)
    probs = jax.nn.softmax(scores, axis=-1).astype(vb.dtype)
    out = jnp.einsum('bnhqk,bnkhd->bnqhd', probs, vb)
    return out.reshape(b, s, D_ATTN)


def gated_short_conv(bg, cg, xc, conv_w, conv_b):
    s = xc.shape[1]
    u = cg * xc
    up = jnp.pad(u, ((0, 0), (CONV_WIDTH - 1, 0), (0, 0)))
    conv = conv_b + sum(conv_w[j] * up[:, j:j + s] for j in range(CONV_WIDTH))
    return bg * conv


def setup_inputs(seed: int = 0) -> dict:
    key = jax.random.key(seed)
    ks = jax.random.split(key, 20)
    f32 = jnp.float32
    d_in = 3 * D_ATTN + 3 * D_CONV
    return {
        "x": jax.random.normal(ks[0], (BATCH, SEQ, D_MODEL), f32),
        "norm1_g": 1.0 + 0.05 * jax.random.normal(ks[1], (D_MODEL,), f32),
        "w_in": jax.random.normal(ks[2], (D_MODEL, d_in), f32) * D_MODEL ** -0.5,
        "q_norm_g": 1.0 + 0.05 * jax.random.normal(ks[3], (HEAD_DIM,), f32),
        "k_norm_g": 1.0 + 0.05 * jax.random.normal(ks[4], (HEAD_DIM,), f32),
        "rel_bias": 0.5 * jax.random.normal(ks[5], (N_HEADS, 2 * MAX_REL + 1), f32),
        "conv_w": jax.random.normal(ks[6], (CONV_WIDTH, D_CONV), f32) * CONV_WIDTH ** -0.5,
        "conv_b": 0.02 * jax.random.normal(ks[7], (D_CONV,), f32),
        "w_attn_proj": jax.random.normal(ks[8], (D_ATTN, D_MODEL), f32) * D_ATTN ** -0.5,
        "w_conv_proj": jax.random.normal(ks[9], (D_CONV, D_MODEL), f32) * D_CONV ** -0.5,
        "w_gate": jax.random.normal(ks[10], (D_MODEL, N_BRANCHES * D_MODEL), f32) * D_MODEL ** -0.5,
        "b_gate": 0.02 * jax.random.normal(ks[11], (N_BRANCHES * D_MODEL,), f32),
        "w_out": jax.random.normal(ks[12], (D_MODEL, D_MODEL), f32) * D_MODEL ** -0.5,
        "norm2_g": 1.0 + 0.05 * jax.random.normal(ks[13], (D_MODEL,), f32),
        "w_up": jax.random.normal(ks[14], (D_MODEL, D_FF), f32) * D_MODEL ** -0.5,
        "w_down": jax.random.normal(ks[15], (D_FF, D_MODEL), f32) * D_FF ** -0.5,
    }


def reference(x, norm1_g, w_in, q_norm_g, k_norm_g, rel_bias, conv_w, conv_b,
              w_attn_proj, w_conv_proj, w_gate, b_gate, w_out, norm2_g, w_up, w_down):
    for _ in range(DEPTH):
        h = rms_norm(x, norm1_g)
        proj = jnp.einsum('bsd,de->bse', h, w_in)
        q, k, v, bg, cg, xc = jnp.split(
            proj,
            [D_ATTN, 2 * D_ATTN, 3 * D_ATTN,
             3 * D_ATTN + D_CONV, 3 * D_ATTN + 2 * D_CONV],
            axis=-1)

        y_attn = chunked_relpos_attention(q, k, v, q_norm_g, k_norm_g, rel_bias)
        y_conv = gated_short_conv(bg, cg, xc, conv_w, conv_b)

        y_attn = jnp.einsum('bse,ed->bsd', y_attn, w_attn_proj)
        y_conv = jnp.einsum('bse,ed->bsd', y_conv, w_conv_proj)

        gates = jax.nn.sigmoid(jnp.einsum('bsd,de->bse', h, w_gate) + b_gate)
        g_attn, g_conv = jnp.split(gates, 2, axis=-1)
        merged = g_attn * y_attn + g_conv * y_conv
        x = x + jnp.einsum('bsd,de->bse', merged, w_out)

        h2 = rms_norm(x, norm2_g)
        u = jnp.square(jax.nn.relu(jnp.einsum('bsd,df->bsf', h2, w_up)))
        x = x + jnp.einsum('bsf,fd->bsd', u, w_down)
    return x
```

```python
import functools

import jax
import jax.numpy as jnp
from jax import lax
from jax.experimental import pallas as pl
from jax.experimental.pallas import tpu as pltpu

D_MODEL = 1024
N_HEADS = 16
HEAD_DIM = 64
CHUNK = 64
N_PREV_CHUNKS = 8
MAX_REL = 256
CONV_WIDTH = 3
D_FF = 4 * D_MODEL
EPS = 1e-6
NEG_INF = -1e30

LANES = 128
HEAD_PAIRS = D_MODEL // LANES
MXU_TILE = 256

ROW_TILE = 512
Q_TILE = 256
KEY_TILES = 1 + (N_PREV_CHUNKS * CHUNK) // Q_TILE
FF_TILE = 1024
CARRY_ROWS = 8
VMEM_LIMIT = 56 * 1024 * 1024

_BF16 = jnp.bfloat16
_F32 = jnp.float32


def _resident(shape):
    return pl.BlockSpec(shape, lambda *_: (0,) * len(shape), pipeline_mode=pl.Buffered(1))


def _rms_norm_rows(x, g):
    ms = jnp.mean(x * x, axis=-1, keepdims=True)
    return x * lax.rsqrt(ms + EPS) * g


def _proj_kernel(x_ref, g1_ref, w_ref, gq_ref, gk_ref, ones_ref, cw_ref, cb_ref,
                 q_ref, k_ref, v_ref, yc_ref, u_scr, *, tiles_per_seq):
    tm = x_ref.shape[0]
    h = _rms_norm_rows(x_ref[...], g1_ref[...]).astype(_BF16)

    def proj(j):
        return jnp.dot(h, w_ref[:, j * D_MODEL:(j + 1) * D_MODEL], preferred_element_type=_F32)

    def head_norm(p, g):
        p2 = (p * p).astype(_BF16)
        ss = jnp.concatenate(
            [jnp.dot(p2[:, c * MXU_TILE:(c + 1) * MXU_TILE], ones_ref[...],
                     preferred_element_type=_F32) for c in range(D_MODEL // MXU_TILE)], axis=-1)
        return p * lax.rsqrt(ss * (1.0 / HEAD_DIM) + EPS) * g

    def store_heads(ref, val):
        val = val.astype(_BF16)
        for hp in range(HEAD_PAIRS):
            ref[hp] = val[:, hp * LANES:(hp + 1) * LANES]

    store_heads(q_ref, head_norm(proj(0), gq_ref[...]))
    store_heads(k_ref, head_norm(proj(1), gk_ref[...]))
    store_heads(v_ref, proj(2))

    bg = proj(3)
    u = proj(4) * proj(5)

    @pl.when(pl.program_id(0) % tiles_per_seq == 0)
    def _():
        u_scr[0:CARRY_ROWS, :] = jnp.zeros((CARRY_ROWS, D_MODEL), _F32)

    u_scr[CARRY_ROWS:CARRY_ROWS + tm, :] = u
    u1 = u_scr[CARRY_ROWS - 1:CARRY_ROWS - 1 + tm, :]
    u2 = u_scr[CARRY_ROWS - 2:CARRY_ROWS - 2 + tm, :]
    conv = cb_ref[...] + cw_ref[0:1, :] * u2 + cw_ref[1:2, :] * u1 + cw_ref[2:3, :] * u
    yc_ref[...] = (bg * conv).astype(_BF16)
    u_scr[0:CARRY_ROWS, :] = u_scr[tm:tm + CARRY_ROWS, :]


def _attn_kernel(q_ref, k0_ref, k1_ref, k2_ref, v0_ref, v1_ref, v2_ref, bias_ref, o_ref):
    k_refs = (k0_ref, k1_ref, k2_ref)
    v_refs = (v0_ref, v1_ref, v2_ref)
    tq = q_ref.shape[1]
    lane = lax.broadcasted_iota(jnp.int32, (tq, LANES), 1)
    nt_dims = (((1,), (1,)), ((), ()))

    def run(tiles):
        for hp in range(HEAD_PAIRS):
            q = q_ref[hp]
            out = None
            for hh in range(2):
                sel = (lane < HEAD_DIM) if hh == 0 else (lane >= HEAD_DIM)
                qh = jnp.where(sel, q, jnp.zeros_like(q))
                head = 2 * hp + hh
                s = [lax.dot_general(qh, k_refs[t][hp], nt_dims, preferred_element_type=_F32)
                     + bias_ref[head, :, t * tq:(t + 1) * tq] for t in tiles]
                m = functools.reduce(jnp.maximum, [jnp.max(st, axis=-1, keepdims=True) for st in s])
                p = [jnp.exp(st - m) for st in s]
                l = functools.reduce(jnp.add, [jnp.sum(pt, axis=-1, keepdims=True) for pt in p])
                o = functools.reduce(jnp.add, [
                    jnp.dot(pt.astype(_BF16), v_refs[t][hp], preferred_element_type=_F32)
                    for pt, t in zip(p, tiles)])
                o = o * (1.0 / l)
                out = o if out is None else jnp.where(sel, o, out)
            o_ref[hp] = out.astype(_BF16)

    qb = pl.program_id(1)
    for first_tile in range(KEY_TILES):
        n_missing = first_tile
        cond = (qb == (KEY_TILES - 1 - n_missing)) if n_missing else (qb >= KEY_TILES - 1)
        pl.when(cond)(functools.partial(run, tuple(range(first_tile, KEY_TILES))))


def _mix_mlp_kernel(x_ref, a_ref, yc_ref, g1_ref, wg_ref, bgate_ref, wap_ref, wcp_ref,
                    wout_ref, g2_ref, wup_ref, wdown_ref, o_ref):
    x = x_ref[...]
    h = _rms_norm_rows(x, g1_ref[...]).astype(_BF16)
    gates = jax.nn.sigmoid(jnp.dot(h, wg_ref[...], preferred_element_type=_F32) + bgate_ref[...])
    a = jnp.concatenate([a_ref[hp] for hp in range(HEAD_PAIRS)], axis=-1)
    ya = jnp.dot(a, wap_ref[...], preferred_element_type=_F32)
    yc = jnp.dot(yc_ref[...], wcp_ref[...], preferred_element_type=_F32)
    merged = (gates[:, :D_MODEL] * ya + gates[:, D_MODEL:] * yc).astype(_BF16)
    x1 = x + jnp.dot(merged, wout_ref[...], preferred_element_type=_F32)
    h2 = _rms_norm_rows(x1, g2_ref[...]).astype(_BF16)
    acc = x1
    for f in range(D_FF // FF_TILE):
        up = jnp.dot(h2, wup_ref[:, f * FF_TILE:(f + 1) * FF_TILE], preferred_element_type=_F32)
        act = jnp.square(jnp.maximum(up, 0.0)).astype(_BF16)
        acc = acc + jnp.dot(act, wdown_ref[f * FF_TILE:(f + 1) * FF_TILE, :],
                            preferred_element_type=_F32)
    o_ref[...] = acc


def _bias_tiles(rel_bias):
    kw = KEY_TILES * Q_TILE
    qi = jnp.arange(Q_TILE)[:, None]
    kj = jnp.arange(kw)[None, :] - N_PREV_CHUNKS * CHUNK
    rel_idx = jnp.clip(qi - kj, -MAX_REL, MAX_REL) + MAX_REL
    dchunk = qi // CHUNK - jnp.floor_divide(kj, CHUNK)
    band = (dchunk >= 0) & (dchunk <= N_PREV_CHUNKS)
    return jnp.where(band[None], rel_bias[:, rel_idx].astype(_F32), NEG_INF)


def kernel(x, norm1_g, w_in, q_norm_g, k_norm_g, rel_bias, conv_w, conv_b, w_attn_proj,
           w_conv_proj, w_gate, b_gate, w_out, norm2_g, w_up, w_down):
    b, s, d = x.shape
    assert d == D_MODEL and s % ROW_TILE == 0 and s % Q_TILE == 0
    t = b * s
    xf = x.reshape(t, d)
    row = lambda v: v.reshape(1, -1).astype(_F32)
    cparams = functools.partial(pltpu.CompilerParams, vmem_limit_bytes=VMEM_LIMIT)

    gq = row(jnp.tile(q_norm_g, N_HEADS) * HEAD_DIM ** -0.5)
    gk = row(jnp.tile(k_norm_g, N_HEADS))
    r = jnp.arange(MXU_TILE)
    head_ones = (r[:, None] // HEAD_DIM == r[None, :] // HEAD_DIM).astype(_BF16)
    n_row_tiles = t // ROW_TILE
    heads_shape = jax.ShapeDtypeStruct((HEAD_PAIRS, t, LANES), _BF16)
    heads_spec = pl.BlockSpec((HEAD_PAIRS, ROW_TILE, LANES), lambda i: (0, i, 0))
    q, k, v, yc = pl.pallas_call(
        functools.partial(_proj_kernel, tiles_per_seq=s // ROW_TILE),
        grid=(n_row_tiles,),
        in_specs=[
            pl.BlockSpec((ROW_TILE, d), lambda i: (i, 0)),
            _resident((1, d)),
            _resident((d, 6 * d)),
            _resident((1, d)),
            _resident((1, d)),
            _resident((MXU_TILE, MXU_TILE)),
            _resident((CONV_WIDTH, d)),
            _resident((1, d)),
        ],
        out_specs=[heads_spec, heads_spec, heads_spec,
                   pl.BlockSpec((ROW_TILE, d), lambda i: (i, 0))],
        out_shape=[heads_shape, heads_shape, heads_shape, jax.ShapeDtypeStruct((t, d), _BF16)],
        scratch_shapes=[pltpu.VMEM((ROW_TILE + CARRY_ROWS, d), _F32)],
        compiler_params=cparams(dimension_semantics=("arbitrary",)),
        name="proj",
    )(xf, row(norm1_g), w_in.astype(_BF16), gq, gk, head_ones, conv_w.astype(_F32), row(conv_b))

    nq = s // Q_TILE
    q_spec = pl.BlockSpec((HEAD_PAIRS, Q_TILE, LANES), lambda bi, qi: (0, bi * nq + qi, 0))

    def kv_spec(tile):
        back = KEY_TILES - 1 - tile
        return pl.BlockSpec((HEAD_PAIRS, Q_TILE, LANES),
                            lambda bi, qi: (0, bi * nq + jnp.maximum(qi - back, 0), 0))

    attn = pl.pallas_call(
        _attn_kernel,
        grid=(b, nq),
        in_specs=[q_spec] + [kv_spec(tl) for tl in range(KEY_TILES)] * 2
                 + [_resident((N_HEADS, Q_TILE, KEY_TILES * Q_TILE))],
        out_specs=q_spec,
        out_shape=heads_shape,
        compiler_params=cparams(dimension_semantics=("arbitrary", "arbitrary")),
        name="attn",
    )(q, k, k, k, v, v, v, _bias_tiles(rel_bias))

    out = pl.pallas_call(
        _mix_mlp_kernel,
        grid=(n_row_tiles,),
        in_specs=[
            pl.BlockSpec((ROW_TILE, d), lambda i: (i, 0)),
            heads_spec,
            pl.BlockSpec((ROW_TILE, d), lambda i: (i, 0)),
            _resident((1, d)),
            _resident((d, 2 * d)),
            _resident((1, 2 * d)),
            _resident((d, d)),
            _resident((d, d)),
            _resident((d, d)),
            _resident((1, d)),
            _resident((d, D_FF)),
            _resident((D_FF, d)),
        ],
        out_specs=pl.BlockSpec((ROW_TILE, d), lambda i: (i, 0)),
        out_shape=jax.ShapeDtypeStruct((t, d), _F32),
        compiler_params=cparams(dimension_semantics=("arbitrary",)),
        name="mix_mlp",
    )(xf, attn, yc, row(norm1_g), w_gate.astype(_BF16), row(b_gate), w_attn_proj.astype(_BF16),
      w_conv_proj.astype(_BF16), w_out.astype(_BF16), row(norm2_g), w_up.astype(_BF16),
      w_down.astype(_BF16))
    return out.reshape(b, s, d)
```

```python
import functools

import jax
import jax.numpy as jnp
from jax import lax
from jax.experimental import pallas as pl
from jax.experimental.pallas import tpu as pltpu

D_MODEL = 1024
N_HEADS = 16
HEAD_DIM = 64
CHUNK = 64
N_PREV_CHUNKS = 8
MAX_REL = 256
CONV_WIDTH = 3
D_FF = 4 * D_MODEL
EPS = 1e-6
NEG_INF = -1e30

LANES = 128
HEAD_PAIRS = D_MODEL // LANES
MXU_TILE = 256

ROW_TILE = 512
Q_TILE = 256
KEY_TILES = 1 + (N_PREV_CHUNKS * CHUNK) // Q_TILE
BIAS_LANES = (KEY_TILES + 1) * Q_TILE
assert 2 * MAX_REL + 1 <= BIAS_LANES and Q_TILE - 1 <= MAX_REL
FF_TILE = 1024
CARRY_ROWS = 8
VMEM_LIMIT = 56 * 1024 * 1024

_BF16 = jnp.bfloat16
_F32 = jnp.float32


def _resident(shape):
    return pl.BlockSpec(shape, lambda *_: (0,) * len(shape), pipeline_mode=pl.Buffered(1))


def _rms_norm_rows(x, g):
    ms = jnp.mean(x * x, axis=-1, keepdims=True)
    return x * lax.rsqrt(ms + EPS) * g


def _proj_kernel(x_ref, g1_ref, w_ref, gq_ref, gk_ref, ones_ref, cw_ref, cb_ref,
                 q_ref, k_ref, v_ref, yc_ref, u_scr, *, tiles_per_seq):
    tm = x_ref.shape[0]
    h = _rms_norm_rows(x_ref[...], g1_ref[...]).astype(_BF16)

    def proj(j):
        return jnp.dot(h, w_ref[:, j * D_MODEL:(j + 1) * D_MODEL], preferred_element_type=_F32)

    def head_norm(p, g):
        p2 = (p * p).astype(_BF16)
        ss = jnp.concatenate(
            [jnp.dot(p2[:, c * MXU_TILE:(c + 1) * MXU_TILE], ones_ref[...],
                     preferred_element_type=_F32) for c in range(D_MODEL // MXU_TILE)], axis=-1)
        return p * lax.rsqrt(ss * (1.0 / HEAD_DIM) + EPS) * g

    def store_heads(ref, val):
        val = val.astype(_BF16)
        for hp in range(HEAD_PAIRS):
            ref[hp] = val[:, hp * LANES:(hp + 1) * LANES]

    store_heads(q_ref, head_norm(proj(0), gq_ref[...]))
    store_heads(k_ref, head_norm(proj(1), gk_ref[...]))
    store_heads(v_ref, proj(2))

    bg = proj(3)
    u = proj(4) * proj(5)

    @pl.when(pl.program_id(0) % tiles_per_seq == 0)
    def _():
        u_scr[0:CARRY_ROWS, :] = jnp.zeros((CARRY_ROWS, D_MODEL), _F32)

    u_scr[CARRY_ROWS:CARRY_ROWS + tm, :] = u
    u1 = u_scr[CARRY_ROWS - 1:CARRY_ROWS - 1 + tm, :]
    u2 = u_scr[CARRY_ROWS - 2:CARRY_ROWS - 2 + tm, :]
    conv = cb_ref[...] + cw_ref[0:1, :] * u2 + cw_ref[1:2, :] * u1 + cw_ref[2:3, :] * u
    yc_ref[...] = (bg * conv).astype(_BF16)
    u_scr[0:CARRY_ROWS, :] = u_scr[tm:tm + CARRY_ROWS, :]


def _attn_kernel(q_ref, k0_ref, k1_ref, k2_ref, v0_ref, v1_ref, v2_ref, bias_ref, o_ref):
    k_refs = (k0_ref, k1_ref, k2_ref)
    v_refs = (v0_ref, v1_ref, v2_ref)
    tq = q_ref.shape[1]
    lane = lax.broadcasted_iota(jnp.int32, (tq, LANES), 1)
    nt_dims = (((1,), (1,)), ((), ()))

    def run(tiles):
        for hp in range(HEAD_PAIRS):
            q = q_ref[hp]
            out = None
            for hh in range(2):
                sel = (lane < HEAD_DIM) if hh == 0 else (lane >= HEAD_DIM)
                qh = jnp.where(sel, q, jnp.zeros_like(q))
                head = 2 * hp + hh
                s = [lax.dot_general(qh, k_refs[t][hp], nt_dims, preferred_element_type=_F32)
                     + bias_ref[head, :, t * tq:(t + 1) * tq] for t in tiles]
                m = functools.reduce(jnp.maximum, [jnp.max(st, axis=-1, keepdims=True) for st in s])
                p = [jnp.exp(st - m) for st in s]
                l = functools.reduce(jnp.add, [jnp.sum(pt, axis=-1, keepdims=True) for pt in p])
                o = functools.reduce(jnp.add, [
                    jnp.dot(pt.astype(_BF16), v_refs[t][hp], preferred_element_type=_F32)
                    for pt, t in zip(p, tiles)])
                o = o * (1.0 / l)
                out = o if out is None else jnp.where(sel, o, out)
            o_ref[hp] = out.astype(_BF16)

    qb = pl.program_id(1)
    for first_tile in range(KEY_TILES):
        n_missing = first_tile
        cond = (qb == (KEY_TILES - 1 - n_missing)) if n_missing else (qb >= KEY_TILES - 1)
        pl.when(cond)(functools.partial(run, tuple(range(first_tile, KEY_TILES))))


def _mix_mlp_kernel(x_ref, a_ref, yc_ref, g1_ref, wg_ref, bgate_ref, wap_ref, wcp_ref,
                    wout_ref, g2_ref, wup_ref, wdown_ref, o_ref):
    x = x_ref[...]
    h = _rms_norm_rows(x, g1_ref[...]).astype(_BF16)
    gates = jax.nn.sigmoid(jnp.dot(h, wg_ref[...], preferred_element_type=_F32) + bgate_ref[...])
    a = jnp.concatenate([a_ref[hp] for hp in range(HEAD_PAIRS)], axis=-1)
    ya = jnp.dot(a, wap_ref[...], preferred_element_type=_F32)
    yc = jnp.dot(yc_ref[...], wcp_ref[...], preferred_element_type=_F32)
    merged = (gates[:, :D_MODEL] * ya + gates[:, D_MODEL:] * yc).astype(_BF16)
    x1 = x + jnp.dot(merged, wout_ref[...], preferred_element_type=_F32)
    h2 = _rms_norm_rows(x1, g2_ref[...]).astype(_BF16)
    acc = x1
    for f in range(D_FF // FF_TILE):
        up = jnp.dot(h2, wup_ref[:, f * FF_TILE:(f + 1) * FF_TILE], preferred_element_type=_F32)
        act = jnp.square(jnp.maximum(up, 0.0)).astype(_BF16)
        acc = acc + jnp.dot(act, wdown_ref[f * FF_TILE:(f + 1) * FF_TILE, :],
                            preferred_element_type=_F32)
    o_ref[...] = acc


def _bias_kernel(rb_ref, o_ref):
    _, tq, kw = o_ref.shape
    rb = rb_ref[0]
    lane = lax.broadcasted_iota(jnp.int32, rb.shape, 1)
    top = rb[:, 2 * MAX_REL:2 * MAX_REL + 1]
    fwd = jnp.where(lane > 2 * MAX_REL, top, rb)
    rolled = pltpu.roll(jnp.broadcast_to(fwd, (kw, BIAS_LANES)), tq, axis=1,
                        stride=1, stride_axis=0)
    bias = rolled[:, :tq].T
    qi = lax.broadcasted_iota(jnp.int32, (tq, kw), 0)
    kj = lax.broadcasted_iota(jnp.int32, (tq, kw), 1)
    dchunk = (qi // CHUNK + N_PREV_CHUNKS) - kj // CHUNK
    band = (dchunk >= 0) & (dchunk <= N_PREV_CHUNKS)
    o_ref[0] = jnp.where(band, bias, NEG_INF)


def kernel(x, norm1_g, w_in, q_norm_g, k_norm_g, rel_bias, conv_w, conv_b, w_attn_proj,
           w_conv_proj, w_gate, b_gate, w_out, norm2_g, w_up, w_down):
    b, s, d = x.shape
    assert d == D_MODEL and s % ROW_TILE == 0 and s % Q_TILE == 0
    t = b * s
    xf = x.reshape(t, d)
    row = lambda v: v.reshape(1, -1).astype(_F32)
    cparams = functools.partial(pltpu.CompilerParams, vmem_limit_bytes=VMEM_LIMIT)

    gq = row(jnp.tile(q_norm_g, N_HEADS) * HEAD_DIM ** -0.5)
    gk = row(jnp.tile(k_norm_g, N_HEADS))
    r = jnp.arange(MXU_TILE)
    head_ones = (r[:, None] // HEAD_DIM == r[None, :] // HEAD_DIM).astype(_BF16)
    n_row_tiles = t // ROW_TILE
    heads_shape = jax.ShapeDtypeStruct((HEAD_PAIRS, t, LANES), _BF16)
    heads_spec = pl.BlockSpec((HEAD_PAIRS, ROW_TILE, LANES), lambda i: (0, i, 0))
    q, k, v, yc = pl.pallas_call(
        functools.partial(_proj_kernel, tiles_per_seq=s // ROW_TILE),
        grid=(n_row_tiles,),
        in_specs=[
            pl.BlockSpec((ROW_TILE, d), lambda i: (i, 0)),
            _resident((1, d)),
            _resident((d, 6 * d)),
            _resident((1, d)),
            _resident((1, d)),
            _resident((MXU_TILE, MXU_TILE)),
            _resident((CONV_WIDTH, d)),
            _resident((1, d)),
        ],
        out_specs=[heads_spec, heads_spec, heads_spec,
                   pl.BlockSpec((ROW_TILE, d), lambda i: (i, 0))],
        out_shape=[heads_shape, heads_shape, heads_shape, jax.ShapeDtypeStruct((t, d), _BF16)],
        scratch_shapes=[pltpu.VMEM((ROW_TILE + CARRY_ROWS, d), _F32)],
        compiler_params=cparams(dimension_semantics=("arbitrary",)),
        name="proj",
    )(xf, row(norm1_g), w_in.astype(_BF16), gq, gk, head_ones, conv_w.astype(_F32), row(conv_b))

    n_rel = rel_bias.shape[1]
    rb_rows = jnp.pad(rel_bias.astype(_F32), ((0, 0), (0, BIAS_LANES - n_rel)))
    bias_tiles = pl.pallas_call(
        _bias_kernel,
        grid=(N_HEADS,),
        in_specs=[pl.BlockSpec((1, 1, BIAS_LANES), lambda hd: (hd, 0, 0))],
        out_specs=pl.BlockSpec((1, Q_TILE, KEY_TILES * Q_TILE), lambda hd: (hd, 0, 0)),
        out_shape=jax.ShapeDtypeStruct((N_HEADS, Q_TILE, KEY_TILES * Q_TILE), _F32),
        compiler_params=cparams(dimension_semantics=("arbitrary",)),
        name="bias_tiles",
    )(rb_rows.reshape(N_HEADS, 1, BIAS_LANES))

    nq = s // Q_TILE
    q_spec = pl.BlockSpec((HEAD_PAIRS, Q_TILE, LANES), lambda bi, qi: (0, bi * nq + qi, 0))

    def kv_spec(tile):
        back = KEY_TILES - 1 - tile
        return pl.BlockSpec((HEAD_PAIRS, Q_TILE, LANES),
                            lambda bi, qi: (0, bi * nq + jnp.maximum(qi - back, 0), 0))

    attn = pl.pallas_call(
        _attn_kernel,
        grid=(b, nq),
        in_specs=[q_spec] + [kv_spec(tl) for tl in range(KEY_TILES)] * 2
                 + [_resident((N_HEADS, Q_TILE, KEY_TILES * Q_TILE))],
        out_specs=q_spec,
        out_shape=heads_shape,
        compiler_params=cparams(dimension_semantics=("arbitrary", "arbitrary")),
        name="attn",
    )(q, k, k, k, v, v, v, bias_tiles)

    out = pl.pallas_call(
        _mix_mlp_kernel,
        grid=(n_row_tiles,),
        in_specs=[
            pl.BlockSpec((ROW_TILE, d), lambda i: (i, 0)),
            heads_spec,
            pl.BlockSpec((ROW_TILE, d), lambda i: (i, 0)),
            _resident((1, d)),
            _resident((d, 2 * d)),
            _resident((1, 2 * d)),
            _resident((d, d)),
            _resident((d, d)),
            _resident((d, d)),
            _resident((1, d)),
            _resident((d, D_FF)),
            _resident((D_FF, d)),
        ],
        out_specs=pl.BlockSpec((ROW_TILE, d), lambda i: (i, 0)),
        out_shape=jax.ShapeDtypeStruct((t, d), _F32),
        compiler_params=cparams(dimension_semantics=("arbitrary",)),
        name="mix_mlp",
    )(xf, attn, yc, row(norm1_g), w_gate.astype(_BF16), row(b_gate), w_attn_proj.astype(_BF16),
      w_conv_proj.astype(_BF16), w_out.astype(_BF16), row(norm2_g), w_up.astype(_BF16),
      w_down.astype(_BF16))
    return out.reshape(b, s, d)
```

```python
import functools
import math

import jax
import jax.numpy as jnp
from jax import lax
from jax.experimental import pallas as pl
from jax.experimental.pallas import tpu as pltpu

D_MODEL = 1024
N_HEADS = 16
HEAD_DIM = 64
CHUNK = 64
N_PREV_CHUNKS = 8
MAX_REL = 256
CONV_WIDTH = 3
D_FF = 4 * D_MODEL
EPS = 1e-6
NEG_INF = -1e30
LOG2E = math.log2(math.e)

LANES = 128
BF16_ROWS = 16
HEAD_PAIRS = D_MODEL // LANES
MXU_TILE = 256

ROW_TILE = 512
Q_TILE = 256
KEY_TILES = 1 + (N_PREV_CHUNKS * CHUNK) // Q_TILE
BIAS_LANES = (KEY_TILES + 1) * Q_TILE
assert 2 * MAX_REL + 1 <= BIAS_LANES and Q_TILE - 1 <= MAX_REL
SCORE_LOOKAHEAD = 3
FF_TILE = 1024
CARRY_ROWS = 8
VMEM_LIMIT = 56 * 1024 * 1024

_BF16 = jnp.bfloat16
_F32 = jnp.float32
_NT_DIMS = (((1,), (1,)), ((), ()))


def _resident(shape):
    return pl.BlockSpec(shape, lambda *_: (0,) * len(shape), pipeline_mode=pl.Buffered(1))


def _rms_norm_rows(x, g):
    ms = jnp.mean(x * x, axis=-1, keepdims=True)
    return x * lax.rsqrt(ms + EPS) * g


def _proj_kernel(x_ref, g1_ref, wqk_ref, wvt_ref, wc_ref, gq_ref, gk_ref, ones_ref, cw_ref,
                 cb_ref, q_ref, k_ref, vt_ref, yc_ref, u_scr, *, tiles_per_seq):
    tm = x_ref.shape[0]
    h = _rms_norm_rows(x_ref[...], g1_ref[...]).astype(_BF16)

    def proj(w_ref, j):
        return jnp.dot(h, w_ref[:, j * D_MODEL:(j + 1) * D_MODEL], preferred_element_type=_F32)

    def head_norm(p, g):
        p2 = (p * p).astype(_BF16)
        ss = jnp.concatenate(
            [jnp.dot(p2[:, c * MXU_TILE:(c + 1) * MXU_TILE], ones_ref[...],
                     preferred_element_type=_F32) for c in range(D_MODEL // MXU_TILE)], axis=-1)
        return p * lax.rsqrt(ss * (1.0 / HEAD_DIM) + EPS) * g

    def store_heads(ref, val):
        val = val.astype(_BF16)
        for hp in range(HEAD_PAIRS):
            ref[hp] = val[:, hp * LANES:(hp + 1) * LANES]

    store_heads(q_ref, head_norm(proj(wqk_ref, 0), gq_ref[...]))
    store_heads(k_ref, head_norm(proj(wqk_ref, 1), gk_ref[...]))

    vt = lax.dot_general(wvt_ref[...], h, _NT_DIMS, preferred_element_type=_F32).astype(_BF16)
    for hp in range(HEAD_PAIRS):
        vt_ref[hp] = vt[hp * LANES:(hp + 1) * LANES, :]

    bg = proj(wc_ref, 0)
    u = proj(wc_ref, 1) * proj(wc_ref, 2)

    @pl.when(pl.program_id(0) % tiles_per_seq == 0)
    def _():
        u_scr[0:CARRY_ROWS, :] = jnp.zeros((CARRY_ROWS, D_MODEL), _F32)

    u_scr[CARRY_ROWS:CARRY_ROWS + tm, :] = u
    u1 = u_scr[CARRY_ROWS - 1:CARRY_ROWS - 1 + tm, :]
    u2 = u_scr[CARRY_ROWS - 2:CARRY_ROWS - 2 + tm, :]
    conv = cb_ref[...] + cw_ref[0:1, :] * u2 + cw_ref[1:2, :] * u1 + cw_ref[2:3, :] * u
    yc_ref[...] = (bg * conv).astype(_BF16)
    u_scr[0:CARRY_ROWS, :] = u_scr[tm:tm + CARRY_ROWS, :]


def _bias_kernel(rb_ref, o_ref):
    _, kw, tq = o_ref.shape
    rb = rb_ref[0]
    lane = lax.broadcasted_iota(jnp.int32, rb.shape, 1)
    top = rb[:, 2 * MAX_REL:2 * MAX_REL + 1]
    fwd = jnp.where(lane > 2 * MAX_REL, top, rb) * LOG2E
    rolled = pltpu.roll(jnp.broadcast_to(fwd, (kw, BIAS_LANES)), tq, axis=1,
                        stride=1, stride_axis=0)
    kj = lax.broadcasted_iota(jnp.int32, (kw, tq), 0)
    qi = lax.broadcasted_iota(jnp.int32, (kw, tq), 1)
    dchunk = (qi // CHUNK + N_PREV_CHUNKS) - kj // CHUNK
    band = (dchunk >= 0) & (dchunk <= N_PREV_CHUNKS)
    o_ref[0] = jnp.where(band, rolled[:, :tq], NEG_INF)


def _attn_kernel(q_ref, k0_ref, k1_ref, k2_ref, v0_ref, v1_ref, v2_ref, bias_ref, o_ref):
    k_refs = (k0_ref, k1_ref, k2_ref)
    v_refs = (v0_ref, v1_ref, v2_ref)
    tq = q_ref.shape[1]
    lane = lax.broadcasted_iota(jnp.int32, (tq, LANES), 1)
    ones_rows = jnp.ones((BF16_ROWS, tq), _BF16)

    def run(tiles):
        lo, hi = tiles[0] * tq, (tiles[-1] + 1) * tq

        def scores(head):
            hp, hh = divmod(head, 2)
            sel = (lane < HEAD_DIM) if hh == 0 else (lane >= HEAD_DIM)
            q = q_ref[hp]
            qh = jnp.where(sel, q, jnp.zeros_like(q))
            keys = jnp.concatenate([k_refs[t][hp] for t in tiles], axis=0)
            s = lax.dot_general(keys, qh, _NT_DIMS, preferred_element_type=_F32)
            return s + bias_ref[head, lo:hi, :]

        def attend(head, s):
            hp, hh = divmod(head, 2)
            m = jnp.max(s, axis=0, keepdims=True)
            p = jnp.exp2(s - m).astype(_BF16)
            v_ext = jnp.concatenate(
                [jnp.concatenate([v_refs[t][hp, hh * HEAD_DIM:(hh + 1) * HEAD_DIM, :],
                                  ones_rows], axis=0) for t in tiles], axis=1)
            acc = jnp.dot(v_ext, p, preferred_element_type=_F32)
            return acc[:HEAD_DIM] * (1.0 / acc[HEAD_DIM:HEAD_DIM + 1])

        pending = [scores(head) for head in range(SCORE_LOOKAHEAD)]
        halves = []
        for head in range(N_HEADS):
            if head + SCORE_LOOKAHEAD < N_HEADS:
                pending.append(scores(head + SCORE_LOOKAHEAD))
            halves.append(attend(head, pending.pop(0)))
            if head % 2:
                o_ref[head // 2] = jnp.concatenate(halves, axis=0).T.astype(_BF16)
                halves = []

    qb = pl.program_id(1)
    for first_tile in range(KEY_TILES):
        n_missing = first_tile
        cond = (qb == (KEY_TILES - 1 - n_missing)) if n_missing else (qb >= KEY_TILES - 1)
        pl.when(cond)(functools.partial(run, tuple(range(first_tile, KEY_TILES))))


def _mix_mlp_kernel(x_ref, a_ref, yc_ref, g1_ref, wg_ref, bgate_ref, wap_ref, wcp_ref,
                    wout_ref, g2_ref, wup_ref, wdown_ref, o_ref):
    x = x_ref[...]
    h = _rms_norm_rows(x, g1_ref[...]).astype(_BF16)
    gates = jax.nn.sigmoid(jnp.dot(h, wg_ref[...], preferred_element_type=_F32) + bgate_ref[...])
    a = jnp.concatenate([a_ref[hp] for hp in range(HEAD_PAIRS)], axis=-1)
    ya = jnp.dot(a, wap_ref[...], preferred_element_type=_F32)
    yc = jnp.dot(yc_ref[...], wcp_ref[...], preferred_element_type=_F32)
    merged = (gates[:, :D_MODEL] * ya + gates[:, D_MODEL:] * yc).astype(_BF16)
    x1 = x + jnp.dot(merged, wout_ref[...], preferred_element_type=_F32)
    h2 = _rms_norm_rows(x1, g2_ref[...]).astype(_BF16)
    acc = x1
    for f in range(D_FF // FF_TILE):
        up = jnp.dot(h2, wup_ref[:, f * FF_TILE:(f + 1) * FF_TILE], preferred_element_type=_F32)
        act = jnp.square(jnp.maximum(up, 0.0)).astype(_BF16)
        acc = acc + jnp.dot(act, wdown_ref[f * FF_TILE:(f + 1) * FF_TILE, :],
                            preferred_element_type=_F32)
    o_ref[...] = acc


def kernel(x, norm1_g, w_in, q_norm_g, k_norm_g, rel_bias, conv_w, conv_b, w_attn_proj,
           w_conv_proj, w_gate, b_gate, w_out, norm2_g, w_up, w_down):
    b, s, d = x.shape
    assert d == D_MODEL and s % ROW_TILE == 0 and s % Q_TILE == 0
    t = b * s
    xf = x.reshape(t, d)
    row = lambda v: v.reshape(1, -1).astype(_F32)
    cparams = functools.partial(pltpu.CompilerParams, vmem_limit_bytes=VMEM_LIMIT)

    gq = row(jnp.tile(q_norm_g, N_HEADS) * (HEAD_DIM ** -0.5 * LOG2E))
    gk = row(jnp.tile(k_norm_g, N_HEADS))
    r = jnp.arange(MXU_TILE)
    head_ones = (r[:, None] // HEAD_DIM == r[None, :] // HEAD_DIM).astype(_BF16)
    n_row_tiles = t // ROW_TILE
    heads_shape = jax.ShapeDtypeStruct((HEAD_PAIRS, t, LANES), _BF16)
    heads_spec = pl.BlockSpec((HEAD_PAIRS, ROW_TILE, LANES), lambda i: (0, i, 0))
    q, k, vt, yc = pl.pallas_call(
        functools.partial(_proj_kernel, tiles_per_seq=s // ROW_TILE),
        grid=(n_row_tiles,),
        in_specs=[
            pl.BlockSpec((ROW_TILE, d), lambda i: (i, 0)),
            _resident((1, d)),
            _resident((d, 2 * d)),
            _resident((d, d)),
            _resident((d, 3 * d)),
            _resident((1, d)),
            _resident((1, d)),
            _resident((MXU_TILE, MXU_TILE)),
            _resident((CONV_WIDTH, d)),
            _resident((1, d)),
        ],
        out_specs=[heads_spec, heads_spec,
                   pl.BlockSpec((HEAD_PAIRS, LANES, ROW_TILE), lambda i: (0, 0, i)),
                   pl.BlockSpec((ROW_TILE, d), lambda i: (i, 0))],
        out_shape=[heads_shape, heads_shape,
                   jax.ShapeDtypeStruct((HEAD_PAIRS, LANES, t), _BF16),
                   jax.ShapeDtypeStruct((t, d), _BF16)],
        scratch_shapes=[pltpu.VMEM((ROW_TILE + CARRY_ROWS, d), _F32)],
        compiler_params=cparams(dimension_semantics=("arbitrary",)),
        name="proj",
    )(xf, row(norm1_g), w_in[:, :2 * d].astype(_BF16), w_in[:, 2 * d:3 * d].T.astype(_BF16),
      w_in[:, 3 * d:].astype(_BF16), gq, gk, head_ones, conv_w.astype(_F32), row(conv_b))

    n_rel = rel_bias.shape[1]
    rb_rows = jnp.pad(rel_bias.astype(_F32), ((0, 0), (0, BIAS_LANES - n_rel)))
    bias_tiles = pl.pallas_call(
        _bias_kernel,
        grid=(N_HEADS,),
        in_specs=[pl.BlockSpec((1, 1, BIAS_LANES), lambda hd: (hd, 0, 0))],
        out_specs=pl.BlockSpec((1, KEY_TILES * Q_TILE, Q_TILE), lambda hd: (hd, 0, 0)),
        out_shape=jax.ShapeDtypeStruct((N_HEADS, KEY_TILES * Q_TILE, Q_TILE), _F32),
        compiler_params=cparams(dimension_semantics=("arbitrary",)),
        name="bias_tiles",
    )(rb_rows.reshape(N_HEADS, 1, BIAS_LANES))

    nq = s // Q_TILE
    q_spec = pl.BlockSpec((HEAD_PAIRS, Q_TILE, LANES), lambda bi, qi: (0, bi * nq + qi, 0))

    def key_tile(qi, tile):
        return jnp.maximum(qi - (KEY_TILES - 1 - tile), 0)

    def k_spec(tile):
        return pl.BlockSpec((HEAD_PAIRS, Q_TILE, LANES),
                            lambda bi, qi: (0, bi * nq + key_tile(qi, tile), 0))

    def vt_spec(tile):
        return pl.BlockSpec((HEAD_PAIRS, LANES, Q_TILE),
                            lambda bi, qi: (0, 0, bi * nq + key_tile(qi, tile)))

    attn = pl.pallas_call(
        _attn_kernel,
        grid=(b, nq),
        in_specs=[q_spec] + [k_spec(tl) for tl in range(KEY_TILES)]
                 + [vt_spec(tl) for tl in range(KEY_TILES)]
                 + [_resident((N_HEADS, KEY_TILES * Q_TILE, Q_TILE))],
        out_specs=q_spec,
        out_shape=heads_shape,
        compiler_params=cparams(dimension_semantics=("arbitrary", "arbitrary")),
        name="attn",
    )(q, k, k, k, vt, vt, vt, bias_tiles)

    out = pl.pallas_call(
        _mix_mlp_kernel,
        grid=(n_row_tiles,),
        in_specs=[
            pl.BlockSpec((ROW_TILE, d), lambda i: (i, 0)),
            heads_spec,
            pl.BlockSpec((ROW_TILE, d), lambda i: (i, 0)),
            _resident((1, d)),
            _resident((d, 2 * d)),
            _resident((1, 2 * d)),
            _resident((d, d)),
            _resident((d, d)),
            _resident((d, d)),
            _resident((1, d)),
            _resident((d, D_FF)),
            _resident((D_FF, d)),
        ],
        out_specs=pl.BlockSpec((ROW_TILE, d), lambda i: (i, 0)),
        out_shape=jax.ShapeDtypeStruct((t, d), _F32),
        compiler_params=cparams(dimension_semantics=("arbitrary",)),
        name="mix_mlp",
    )(xf, attn, yc, row(norm1_g), w_gate.astype(_BF16), row(b_gate), w_attn_proj.astype(_BF16),
      w_conv_proj.astype(_BF16), w_out.astype(_BF16), row(norm2_g), w_up.astype(_BF16),
      w_down.astype(_BF16))
    return out.reshape(b, s, d)
```

```python
import functools
import math

import jax
import jax.numpy as jnp
from jax import lax
from jax.experimental import pallas as pl
from jax.experimental.pallas import tpu as pltpu

D_MODEL = 1024
N_HEADS = 16
HEAD_DIM = 64
CHUNK = 64
N_PREV_CHUNKS = 8
MAX_REL = 256
CONV_WIDTH = 3
D_FF = 4 * D_MODEL
EPS = 1e-6
NEG_INF = -1e30
LOG2E = math.log2(math.e)

LANES = 128
BF16_ROWS = 16
HEAD_PAIRS = D_MODEL // LANES
MXU_TILE = 256

ROW_TILE = 512
PROJ_TILE = 2 * MXU_TILE
CONV_TILE = MXU_TILE
Q_TILE = 256
KEY_TILES = 1 + (N_PREV_CHUNKS * CHUNK) // Q_TILE
BIAS_LANES = (KEY_TILES + 1) * Q_TILE
assert 2 * MAX_REL + 1 <= BIAS_LANES and Q_TILE - 1 <= MAX_REL
SCORE_LOOKAHEAD = 3
FF_TILE = 1024
CARRY_ROWS = 8
VMEM_LIMIT = 56 * 1024 * 1024

_BF16 = jnp.bfloat16
_F32 = jnp.float32
_NT_DIMS = (((1,), (1,)), ((), ()))


def _resident(shape):
    return pl.BlockSpec(shape, lambda *_: (0,) * len(shape), pipeline_mode=pl.Buffered(1))


def _pack_rows(w):
    k, n = w.shape
    pairs = w.astype(_BF16).reshape(k // 2, 2, n).swapaxes(-1, -2)
    return lax.bitcast_convert_type(pairs, jnp.uint32)


def _unpack_rows(w32):
    return pltpu.bitcast(w32, _BF16)


def _rms_norm_rows(x, g):
    ms = jnp.mean(x * x, axis=-1, keepdims=True)
    return x * lax.rsqrt(ms + EPS) * g


def _proj_kernel(x_ref, g1_ref, wqk_ref, wvt_ref, wc_ref, gq_ref, gk_ref, ones_ref, cw_ref,
                 cb_ref, q_ref, k_ref, vt_ref, yc_ref, u_scr, *, tiles_per_seq):
    tm = x_ref.shape[0]
    h = _rms_norm_rows(x_ref[...], g1_ref[...]).astype(_BF16)
    n_col_tiles = D_MODEL // PROJ_TILE
    slabs = PROJ_TILE // LANES

    def proj(w_ref, j, c, width=PROJ_TILE):
        lo = j * D_MODEL + c * width
        return jnp.dot(h, _unpack_rows(w_ref[:, lo:lo + width]), preferred_element_type=_F32)

    @pl.when(pl.program_id(0) % tiles_per_seq == 0)
    def _():
        u_scr[0:CARRY_ROWS, :] = jnp.zeros((CARRY_ROWS, D_MODEL), _F32)

    for c in range(D_MODEL // CONV_TILE):
        cols = slice(c * CONV_TILE, (c + 1) * CONV_TILE)
        bg = proj(wc_ref, 0, c, CONV_TILE)
        u = proj(wc_ref, 1, c, CONV_TILE) * proj(wc_ref, 2, c, CONV_TILE)
        u_scr[CARRY_ROWS:CARRY_ROWS + tm, cols] = u
        u1 = u_scr[CARRY_ROWS - 1:CARRY_ROWS - 1 + tm, cols]
        u2 = u_scr[CARRY_ROWS - 2:CARRY_ROWS - 2 + tm, cols]
        conv = (cb_ref[:, cols] + cw_ref[0:1, cols] * u2 + cw_ref[1:2, cols] * u1
                + cw_ref[2:3, cols] * u)
        yc_ref[:, cols] = (bg * conv).astype(_BF16)
    u_scr[0:CARRY_ROWS, :] = u_scr[tm:tm + CARRY_ROWS, :]

    def finish_head_norm(p, g_ref, o_ref, c):
        p2 = (p * p).astype(_BF16)
        ss = jnp.concatenate(
            [jnp.dot(p2[:, t * MXU_TILE:(t + 1) * MXU_TILE], ones_ref[...],
                     preferred_element_type=_F32) for t in range(PROJ_TILE // MXU_TILE)], axis=-1)
        g = g_ref[:, c * PROJ_TILE:(c + 1) * PROJ_TILE]
        pn = (p * lax.rsqrt(ss * (1.0 / HEAD_DIM) + EPS) * g).astype(_BF16)
        for sl in range(slabs):
            o_ref[c * slabs + sl] = pn[:, sl * LANES:(sl + 1) * LANES]

    waiting = None
    for j, (g_ref, o_ref) in enumerate(((gq_ref, q_ref), (gk_ref, k_ref))):
        for c in range(n_col_tiles):
            p = proj(wqk_ref, j, c)
            if waiting is not None:
                finish_head_norm(*waiting)
            waiting = (p, g_ref, o_ref, c)

    for c in range(n_col_tiles):
        rows = slice(c * PROJ_TILE // 2, (c + 1) * PROJ_TILE // 2)
        vt = lax.dot_general(_unpack_rows(wvt_ref[rows, :]), h, _NT_DIMS,
                             preferred_element_type=_F32).astype(_BF16)
        if waiting is not None:
            finish_head_norm(*waiting)
            waiting = None
        for sl in range(slabs):
            vt_ref[c * slabs + sl] = vt[sl * LANES:(sl + 1) * LANES, :]


def _bias_kernel(rb_ref, o_ref):
    _, kw, tq = o_ref.shape
    rb = rb_ref[0]
    lane = lax.broadcasted_iota(jnp.int32, rb.shape, 1)
    top = rb[:, 2 * MAX_REL:2 * MAX_REL + 1]
    fwd = jnp.where(lane > 2 * MAX_REL, top, rb) * LOG2E
    rolled = pltpu.roll(jnp.broadcast_to(fwd, (kw, BIAS_LANES)), tq, axis=1,
                        stride=1, stride_axis=0)
    kj = lax.broadcasted_iota(jnp.int32, (kw, tq), 0)
    qi = lax.broadcasted_iota(jnp.int32, (kw, tq), 1)
    dchunk = (qi // CHUNK + N_PREV_CHUNKS) - kj // CHUNK
    band = (dchunk >= 0) & (dchunk <= N_PREV_CHUNKS)
    o_ref[0] = jnp.where(band, rolled[:, :tq], NEG_INF)


def _attn_kernel(q_ref, k0_ref, k1_ref, k2_ref, v0_ref, v1_ref, v2_ref, bias_ref, o_ref):
    k_refs = (k0_ref, k1_ref, k2_ref)
    v_refs = (v0_ref, v1_ref, v2_ref)
    tq = q_ref.shape[1]
    lane = lax.broadcasted_iota(jnp.int32, (tq, LANES), 1)
    ones_rows = jnp.ones((BF16_ROWS, tq), _BF16)

    def run(tiles):
        lo, hi = tiles[0] * tq, (tiles[-1] + 1) * tq

        def scores(head):
            hp, hh = divmod(head, 2)
            sel = (lane < HEAD_DIM) if hh == 0 else (lane >= HEAD_DIM)
            q = q_ref[hp]
            qh = jnp.where(sel, q, jnp.zeros_like(q))
            keys = jnp.concatenate([k_refs[t][hp] for t in tiles], axis=0)
            s = lax.dot_general(keys, qh, _NT_DIMS, preferred_element_type=_F32)
            return s + bias_ref[head, lo:hi, :]

        def attend(head, s):
            hp, hh = divmod(head, 2)
            m = jnp.max(s, axis=0, keepdims=True)
            p = jnp.exp2(s - m).astype(_BF16)
            v_ext = jnp.concatenate(
                [jnp.concatenate([v_refs[t][hp, hh * HEAD_DIM:(hh + 1) * HEAD_DIM, :],
                                  ones_rows], axis=0) for t in tiles], axis=1)
            acc = jnp.dot(v_ext, p, preferred_element_type=_F32)
            return acc[:HEAD_DIM] * (1.0 / acc[HEAD_DIM:HEAD_DIM + 1])

        pending = [scores(head) for head in range(SCORE_LOOKAHEAD)]
        halves = []
        for head in range(N_HEADS):
            if head + SCORE_LOOKAHEAD < N_HEADS:
                pending.append(scores(head + SCORE_LOOKAHEAD))
            halves.append(attend(head, pending.pop(0)))
            if head % 2:
                o_ref[head // 2] = jnp.concatenate(halves, axis=0).T.astype(_BF16)
                halves = []

    qb = pl.program_id(1)
    for first_tile in range(KEY_TILES):
        n_missing = first_tile
        cond = (qb == (KEY_TILES - 1 - n_missing)) if n_missing else (qb >= KEY_TILES - 1)
        pl.when(cond)(functools.partial(run, tuple(range(first_tile, KEY_TILES))))


def _mix_mlp_kernel(x_ref, a_ref, yc_ref, g1_ref, wg_ref, bgate_ref, wap_ref, wcp_ref,
                    wout_ref, g2_ref, wup_ref, wdown_ref, o_ref):
    x = x_ref[...]
    h = _rms_norm_rows(x, g1_ref[...]).astype(_BF16)
    gates = jax.nn.sigmoid(jnp.dot(h, wg_ref[...], preferred_element_type=_F32) + bgate_ref[...])
    a = jnp.concatenate([a_ref[hp] for hp in range(HEAD_PAIRS)], axis=-1)
    ya = jnp.dot(a, wap_ref[...], preferred_element_type=_F32)
    yc = jnp.dot(yc_ref[...], wcp_ref[...], preferred_element_type=_F32)
    merged = (gates[:, :D_MODEL] * ya + gates[:, D_MODEL:] * yc).astype(_BF16)
    x1 = x + jnp.dot(merged, wout_ref[...], preferred_element_type=_F32)
    h2 = _rms_norm_rows(x1, g2_ref[...]).astype(_BF16)
    acc = x1
    for f in range(D_FF // FF_TILE):
        up = jnp.dot(h2, wup_ref[:, f * FF_TILE:(f + 1) * FF_TILE], preferred_element_type=_F32)
        act = jnp.square(jnp.maximum(up, 0.0)).astype(_BF16)
        acc = acc + jnp.dot(act, wdown_ref[f * FF_TILE:(f + 1) * FF_TILE, :],
                            preferred_element_type=_F32)
    o_ref[...] = acc


def kernel(x, norm1_g, w_in, q_norm_g, k_norm_g, rel_bias, conv_w, conv_b, w_attn_proj,
           w_conv_proj, w_gate, b_gate, w_out, norm2_g, w_up, w_down):
    b, s, d = x.shape
    assert d == D_MODEL and s % ROW_TILE == 0 and s % Q_TILE == 0
    t = b * s
    xf = x.reshape(t, d)
    row = lambda v: v.reshape(1, -1).astype(_F32)
    cparams = functools.partial(pltpu.CompilerParams, vmem_limit_bytes=VMEM_LIMIT)

    gq = row(jnp.tile(q_norm_g, N_HEADS) * (HEAD_DIM ** -0.5 * LOG2E))
    gk = row(jnp.tile(k_norm_g, N_HEADS))
    r = jnp.arange(MXU_TILE)
    head_ones = (r[:, None] // HEAD_DIM == r[None, :] // HEAD_DIM).astype(_BF16)
    n_row_tiles = t // ROW_TILE
    heads_shape = jax.ShapeDtypeStruct((HEAD_PAIRS, t, LANES), _BF16)
    heads_spec = pl.BlockSpec((HEAD_PAIRS, ROW_TILE, LANES), lambda i: (0, i, 0))
    q, k, vt, yc = pl.pallas_call(
        functools.partial(_proj_kernel, tiles_per_seq=s // ROW_TILE),
        grid=(n_row_tiles,),
        in_specs=[
            pl.BlockSpec((ROW_TILE, d), lambda i: (i, 0)),
            _resident((1, d)),
            _resident((d // 2, 2 * d)),
            _resident((d // 2, d)),
            _resident((d // 2, 3 * d)),
            _resident((1, d)),
            _resident((1, d)),
            _resident((MXU_TILE, MXU_TILE)),
            _resident((CONV_WIDTH, d)),
            _resident((1, d)),
        ],
        out_specs=[heads_spec, heads_spec,
                   pl.BlockSpec((HEAD_PAIRS, LANES, ROW_TILE), lambda i: (0, 0, i)),
                   pl.BlockSpec((ROW_TILE, d), lambda i: (i, 0))],
        out_shape=[heads_shape, heads_shape,
                   jax.ShapeDtypeStruct((HEAD_PAIRS, LANES, t), _BF16),
                   jax.ShapeDtypeStruct((t, d), _BF16)],
        scratch_shapes=[pltpu.VMEM((ROW_TILE + CARRY_ROWS, d), _F32)],
        compiler_params=cparams(dimension_semantics=("arbitrary",)),
        name="proj",
    )(xf, row(norm1_g), _pack_rows(w_in[:, :2 * d]), _pack_rows(w_in[:, 2 * d:3 * d].T),
      _pack_rows(w_in[:, 3 * d:]), gq, gk, head_ones, conv_w.astype(_F32), row(conv_b))

    n_rel = rel_bias.shape[1]
    rb_rows = jnp.pad(rel_bias.astype(_F32), ((0, 0), (0, BIAS_LANES - n_rel)))
    bias_tiles = pl.pallas_call(
        _bias_kernel,
        grid=(N_HEADS,),
        in_specs=[pl.BlockSpec((1, 1, BIAS_LANES), lambda hd: (hd, 0, 0))],
        out_specs=pl.BlockSpec((1, KEY_TILES * Q_TILE, Q_TILE), lambda hd: (hd, 0, 0)),
        out_shape=jax.ShapeDtypeStruct((N_HEADS, KEY_TILES * Q_TILE, Q_TILE), _F32),
        compiler_params=cparams(dimension_semantics=("arbitrary",)),
        name="bias_tiles",
    )(rb_rows.reshape(N_HEADS, 1, BIAS_LANES))

    nq = s // Q_TILE
    q_spec = pl.BlockSpec((HEAD_PAIRS, Q_TILE, LANES), lambda bi, qi: (0, bi * nq + qi, 0))

    def key_tile(qi, tile):
        return jnp.maximum(qi - (KEY_TILES - 1 - tile), 0)

    def k_spec(tile):
        return pl.BlockSpec((HEAD_PAIRS, Q_TILE, LANES),
                            lambda bi, qi: (0, bi * nq + key_tile(qi, tile), 0))

    def vt_spec(tile):
        return pl.BlockSpec((HEAD_PAIRS, LANES, Q_TILE),
                            lambda bi, qi: (0, 0, bi * nq + key_tile(qi, tile)))

    attn = pl.pallas_call(
        _attn_kernel,
        grid=(b, nq),
        in_specs=[q_spec] + [k_spec(tl) for tl in range(KEY_TILES)]
                 + [vt_spec(tl) for tl in range(KEY_TILES)]
                 + [_resident((N_HEADS, KEY_TILES * Q_TILE, Q_TILE))],
        out_specs=q_spec,
        out_shape=heads_shape,
        compiler_params=cparams(dimension_semantics=("arbitrary", "arbitrary")),
        name="attn",
    )(q, k, k, k, vt, vt, vt, bias_tiles)

    out = pl.pallas_call(
        _mix_mlp_kernel,
        grid=(n_row_tiles,),
        in_specs=[
            pl.BlockSpec((ROW_TILE, d), lambda i: (i, 0)),
            heads_spec,
            pl.BlockSpec((ROW_TILE, d), lambda i: (i, 0)),
            _resident((1, d)),
            _resident((d, 2 * d)),
            _resident((1, 2 * d)),
            _resident((d, d)),
            _resident((d, d)),
            _resident((d, d)),
            _resident((1, d)),
            _resident((d, D_FF)),
            _resident((D_FF, d)),
        ],
        out_specs=pl.BlockSpec((ROW_TILE, d), lambda i: (i, 0)),
        out_shape=jax.ShapeDtypeStruct((t, d), _F32),
        compiler_params=cparams(dimension_semantics=("arbitrary",)),
        name="mix_mlp",
    )(xf, attn, yc, row(norm1_g), w_gate.astype(_BF16), row(b_gate), w_attn_proj.astype(_BF16),
      w_conv_proj.astype(_BF16), w_out.astype(_BF16), row(norm2_g), w_up.astype(_BF16),
      w_down.astype(_BF16))
    return out.reshape(b, s, d)
```

```python
import functools
import math

import jax
import jax.numpy as jnp
from jax import lax
from jax.experimental import pallas as pl
from jax.experimental.pallas import tpu as pltpu

D_MODEL = 1024
N_HEADS = 16
HEAD_DIM = 64
CHUNK = 64
N_PREV_CHUNKS = 8
MAX_REL = 256
CONV_WIDTH = 3
D_FF = 4 * D_MODEL
EPS = 1e-6
NEG_INF = -1e30
LOG2E = math.log2(math.e)

LANES = 128
BF16_ROWS = 16
HEAD_PAIRS = D_MODEL // LANES
MXU_TILE = 256

ROW_TILE = 512
PROJ_TILE = 2 * MXU_TILE
CONV_TILE = MXU_TILE
Q_TILE = 256
KEY_TILES = 1 + (N_PREV_CHUNKS * CHUNK) // Q_TILE
BAND_ROWS = (N_PREV_CHUNKS + LANES // CHUNK) * CHUNK
BIAS_LANES = (KEY_TILES + 1) * Q_TILE
assert 2 * MAX_REL + 1 <= BIAS_LANES and Q_TILE - 1 <= MAX_REL
SCORE_LOOKAHEAD = 3
FF_TILE = 1024
CARRY_ROWS = 8
VMEM_LIMIT = 56 * 1024 * 1024

_BF16 = jnp.bfloat16
_F32 = jnp.float32
_NT_DIMS = (((1,), (1,)), ((), ()))


def _resident(shape):
    return pl.BlockSpec(shape, lambda *_: (0,) * len(shape), pipeline_mode=pl.Buffered(1))


def _pack_rows(w):
    bits = lax.bitcast_convert_type(w.astype(_BF16), jnp.uint16).astype(jnp.uint32)
    return bits[0::2] | (bits[1::2] << 16)


def _unpack_rows(w32):
    return pltpu.bitcast(w32, _BF16)


def _rms_norm_rows(x, g):
    ms = jnp.mean(x * x, axis=-1, keepdims=True)
    return x * lax.rsqrt(ms + EPS) * g


def _proj_kernel(x_ref, g1_ref, wqk_ref, wvt_ref, wc_ref, gq_ref, gk_ref, ones_ref, cw_ref,
                 cb_ref, q_ref, k_ref, vt_ref, yc_ref, u_scr, *, tiles_per_seq):
    tm = x_ref.shape[0]
    h = _rms_norm_rows(x_ref[...], g1_ref[...]).astype(_BF16)
    n_col_tiles = D_MODEL // PROJ_TILE
    slabs = PROJ_TILE // LANES

    def proj(w_ref, j, c, width=PROJ_TILE):
        lo = j * D_MODEL + c * width
        return jnp.dot(h, _unpack_rows(w_ref[:, lo:lo + width]), preferred_element_type=_F32)

    @pl.when(pl.program_id(0) % tiles_per_seq == 0)
    def _():
        u_scr[0:CARRY_ROWS, :] = jnp.zeros((CARRY_ROWS, D_MODEL), _F32)

    for c in range(D_MODEL // CONV_TILE):
        cols = slice(c * CONV_TILE, (c + 1) * CONV_TILE)
        bg = proj(wc_ref, 0, c, CONV_TILE)
        u = proj(wc_ref, 1, c, CONV_TILE) * proj(wc_ref, 2, c, CONV_TILE)
        u_scr[CARRY_ROWS:CARRY_ROWS + tm, cols] = u
        u1 = u_scr[CARRY_ROWS - 1:CARRY_ROWS - 1 + tm, cols]
        u2 = u_scr[CARRY_ROWS - 2:CARRY_ROWS - 2 + tm, cols]
        conv = (cb_ref[:, cols] + cw_ref[0:1, cols] * u2 + cw_ref[1:2, cols] * u1
                + cw_ref[2:3, cols] * u)
        yc_ref[:, cols] = (bg * conv).astype(_BF16)
    u_scr[0:CARRY_ROWS, :] = u_scr[tm:tm + CARRY_ROWS, :]

    def finish_head_norm(p, g_ref, o_ref, c):
        p2 = (p * p).astype(_BF16)
        ss = jnp.concatenate(
            [jnp.dot(p2[:, t * MXU_TILE:(t + 1) * MXU_TILE], ones_ref[...],
                     preferred_element_type=_F32) for t in range(PROJ_TILE // MXU_TILE)], axis=-1)
        g = g_ref[:, c * PROJ_TILE:(c + 1) * PROJ_TILE]
        pn = (p * lax.rsqrt(ss * (1.0 / HEAD_DIM) + EPS) * g).astype(_BF16)
        for sl in range(slabs):
            o_ref[c * slabs + sl] = pn[:, sl * LANES:(sl + 1) * LANES]

    waiting = None
    for j, (g_ref, o_ref) in enumerate(((gq_ref, q_ref), (gk_ref, k_ref))):
        for c in range(n_col_tiles):
            p = proj(wqk_ref, j, c)
            if waiting is not None:
                finish_head_norm(*waiting)
            waiting = (p, g_ref, o_ref, c)

    for c in range(n_col_tiles):
        rows = slice(c * PROJ_TILE // 2, (c + 1) * PROJ_TILE // 2)
        vt = lax.dot_general(_unpack_rows(wvt_ref[rows, :]), h, _NT_DIMS,
                             preferred_element_type=_F32).astype(_BF16)
        if waiting is not None:
            finish_head_norm(*waiting)
            waiting = None
        for sl in range(slabs):
            vt_ref[c * slabs + sl] = vt[sl * LANES:(sl + 1) * LANES, :]


def _bias_kernel(rb_ref, o_ref):
    _, kw, tq = o_ref.shape
    rb = rb_ref[0]
    lane = lax.broadcasted_iota(jnp.int32, rb.shape, 1)
    top = rb[:, 2 * MAX_REL:2 * MAX_REL + 1]
    fwd = jnp.where(lane > 2 * MAX_REL, top, rb) * LOG2E
    rolled = pltpu.roll(jnp.broadcast_to(fwd, (kw, BIAS_LANES)), tq, axis=1,
                        stride=1, stride_axis=0)
    kj = lax.broadcasted_iota(jnp.int32, (kw, tq), 0)
    qi = lax.broadcasted_iota(jnp.int32, (kw, tq), 1)
    dchunk = (qi // CHUNK + N_PREV_CHUNKS) - kj // CHUNK
    band = (dchunk >= 0) & (dchunk <= N_PREV_CHUNKS)
    o_ref[0] = jnp.where(band, rolled[:, :tq], NEG_INF)


def _attn_kernel(q_ref, k0_ref, k1_ref, k2_ref, v0_ref, v1_ref, v2_ref, bias_ref, o_ref):
    k_refs = (k0_ref, k1_ref, k2_ref)
    v_refs = (v0_ref, v1_ref, v2_ref)
    tq = q_ref.shape[1]
    lane = lax.broadcasted_iota(jnp.int32, (tq, LANES), 1)
    ones_rows = jnp.ones((BF16_ROWS, tq), _BF16)

    def run(tiles):
        lo, hi = tiles[0] * tq, (tiles[-1] + 1) * tq

        def scores(head):
            hp, hh = divmod(head, 2)
            sel = (lane < HEAD_DIM) if hh == 0 else (lane >= HEAD_DIM)
            q = q_ref[hp]
            qh = jnp.where(sel, q, jnp.zeros_like(q))
            keys = jnp.concatenate([k_refs[t][hp] for t in tiles], axis=0)
            s = lax.dot_general(keys, qh, _NT_DIMS, preferred_element_type=_F32)
            cols = []
            for c in range(tq // LANES):
                r0 = max(lo, c * LANES)
                r1 = min(hi, c * LANES + BAND_ROWS)
                cols.append((r0, r1, s[r0 - lo:r1 - lo, c * LANES:(c + 1) * LANES]
                             + bias_ref[head, r0:r1, c * LANES:(c + 1) * LANES]))
            return cols

        def attend(head, cols):
            hp, hh = divmod(head, 2)
            p_cols = []
            for r0, r1, s in cols:
                m = jnp.max(s, axis=0, keepdims=True)
                p = jnp.exp2(s - m).astype(_BF16)
                pads = [jnp.zeros((n, LANES), _BF16) for n in (r0 - lo, hi - r1)]
                p_cols.append(jnp.concatenate(
                    [blk for blk in (pads[0], p, pads[1]) if blk.shape[0]], axis=0))
            p = jnp.concatenate(p_cols, axis=1)
            v_ext = jnp.concatenate(
                [jnp.concatenate([v_refs[t][hp, hh * HEAD_DIM:(hh + 1) * HEAD_DIM, :],
                                  ones_rows], axis=0) for t in tiles], axis=1)
            acc = jnp.dot(v_ext, p, preferred_element_type=_F32)
            return acc[:HEAD_DIM] * (1.0 / acc[HEAD_DIM:HEAD_DIM + 1])

        pending = [scores(head) for head in range(SCORE_LOOKAHEAD)]
        halves = []
        for head in range(N_HEADS):
            if head + SCORE_LOOKAHEAD < N_HEADS:
                pending.append(scores(head + SCORE_LOOKAHEAD))
            halves.append(attend(head, pending.pop(0)))
            if head % 2:
                o_ref[head // 2] = jnp.concatenate(halves, axis=0).T.astype(_BF16)
                halves = []

    qb = pl.program_id(1)
    for first_tile in range(KEY_TILES):
        n_missing = first_tile
        cond = (qb == (KEY_TILES - 1 - n_missing)) if n_missing else (qb >= KEY_TILES - 1)
        pl.when(cond)(functools.partial(run, tuple(range(first_tile, KEY_TILES))))


def _mix_mlp_kernel(x_ref, a_ref, yc_ref, g1_ref, wg_ref, bgate_ref, wap_ref, wcp_ref,
                    wout_ref, g2_ref, wup_ref, wdown_ref, o_ref):
    x = x_ref[...]
    h = _rms_norm_rows(x, g1_ref[...]).astype(_BF16)
    gates = jax.nn.sigmoid(jnp.dot(h, wg_ref[...], preferred_element_type=_F32) + bgate_ref[...])
    a = jnp.concatenate([a_ref[hp] for hp in range(HEAD_PAIRS)], axis=-1)
    ya = jnp.dot(a, wap_ref[...], preferred_element_type=_F32)
    yc = jnp.dot(yc_ref[...], wcp_ref[...], preferred_element_type=_F32)
    merged = (gates[:, :D_MODEL] * ya + gates[:, D_MODEL:] * yc).astype(_BF16)
    x1 = x + jnp.dot(merged, wout_ref[...], preferred_element_type=_F32)
    h2 = _rms_norm_rows(x1, g2_ref[...]).astype(_BF16)
    acc = x1
    for f in range(D_FF // FF_TILE):
        up = jnp.dot(h2, wup_ref[:, f * FF_TILE:(f + 1) * FF_TILE], preferred_element_type=_F32)
        act = jnp.square(jnp.maximum(up, 0.0)).astype(_BF16)
        acc = acc + jnp.dot(act, wdown_ref[f * FF_TILE:(f + 1) * FF_TILE, :],
                            preferred_element_type=_F32)
    o_ref[...] = acc


def kernel(x, norm1_g, w_in, q_norm_g, k_norm_g, rel_bias, conv_w, conv_b, w_attn_proj,
           w_conv_proj, w_gate, b_gate, w_out, norm2_g, w_up, w_down):
    b, s, d = x.shape
    assert d == D_MODEL and s % ROW_TILE == 0 and s % Q_TILE == 0
    t = b * s
    xf = x.reshape(t, d)
    row = lambda v: v.reshape(1, -1).astype(_F32)
    cparams = functools.partial(pltpu.CompilerParams, vmem_limit_bytes=VMEM_LIMIT)

    gq = row(jnp.tile(q_norm_g, N_HEADS) * (HEAD_DIM ** -0.5 * LOG2E))
    gk = row(jnp.tile(k_norm_g, N_HEADS))
    r = jnp.arange(MXU_TILE)
    head_ones = (r[:, None] // HEAD_DIM == r[None, :] // HEAD_DIM).astype(_BF16)
    n_row_tiles = t // ROW_TILE
    heads_shape = jax.ShapeDtypeStruct((HEAD_PAIRS, t, LANES), _BF16)
    heads_spec = pl.BlockSpec((HEAD_PAIRS, ROW_TILE, LANES), lambda i: (0, i, 0))
    q, k, vt, yc = pl.pallas_call(
        functools.partial(_proj_kernel, tiles_per_seq=s // ROW_TILE),
        grid=(n_row_tiles,),
        in_specs=[
            pl.BlockSpec((ROW_TILE, d), lambda i: (i, 0)),
            _resident((1, d)),
            _resident((d // 2, 2 * d)),
            _resident((d // 2, d)),
            _resident((d // 2, 3 * d)),
            _resident((1, d)),
            _resident((1, d)),
            _resident((MXU_TILE, MXU_TILE)),
            _resident((CONV_WIDTH, d)),
            _resident((1, d)),
        ],
        out_specs=[heads_spec, heads_spec,
                   pl.BlockSpec((HEAD_PAIRS, LANES, ROW_TILE), lambda i: (0, 0, i)),
                   pl.BlockSpec((ROW_TILE, d), lambda i: (i, 0))],
        out_shape=[heads_shape, heads_shape,
                   jax.ShapeDtypeStruct((HEAD_PAIRS, LANES, t), _BF16),
                   jax.ShapeDtypeStruct((t, d), _BF16)],
        scratch_shapes=[pltpu.VMEM((ROW_TILE + CARRY_ROWS, d), _F32)],
        compiler_params=cparams(dimension_semantics=("arbitrary",)),
        name="proj",
    )(xf, row(norm1_g), _pack_rows(w_in[:, :2 * d]), _pack_rows(w_in[:, 2 * d:3 * d].T),
      _pack_rows(w_in[:, 3 * d:]), gq, gk, head_ones, conv_w.astype(_F32), row(conv_b))

    n_rel = rel_bias.shape[1]
    rb_rows = jnp.pad(rel_bias.astype(_F32), ((0, 0), (0, BIAS_LANES - n_rel)))
    bias_tiles = pl.pallas_call(
        _bias_kernel,
        grid=(N_HEADS,),
        in_specs=[pl.BlockSpec((1, 1, BIAS_LANES), lambda hd: (hd, 0, 0))],
        out_specs=pl.BlockSpec((1, KEY_TILES * Q_TILE, Q_TILE), lambda hd: (hd, 0, 0)),
        out_shape=jax.ShapeDtypeStruct((N_HEADS, KEY_TILES * Q_TILE, Q_TILE), _F32),
        compiler_params=cparams(dimension_semantics=("arbitrary",)),
        name="bias_tiles",
    )(rb_rows.reshape(N_HEADS, 1, BIAS_LANES))

    nq = s // Q_TILE
    q_spec = pl.BlockSpec((HEAD_PAIRS, Q_TILE, LANES), lambda bi, qi: (0, bi * nq + qi, 0))

    def key_tile(qi, tile):
        return jnp.maximum(qi - (KEY_TILES - 1 - tile), 0)

    def k_spec(tile):
        return pl.BlockSpec((HEAD_PAIRS, Q_TILE, LANES),
                            lambda bi, qi: (0, bi * nq + key_tile(qi, tile), 0))

    def vt_spec(tile):
        return pl.BlockSpec((HEAD_PAIRS, LANES, Q_TILE),
                            lambda bi, qi: (0, 0, bi * nq + key_tile(qi, tile)))

    attn = pl.pallas_call(
        _attn_kernel,
        grid=(b, nq),
        in_specs=[q_spec] + [k_spec(tl) for tl in range(KEY_TILES)]
                 + [vt_spec(tl) for tl in range(KEY_TILES)]
                 + [_resident((N_HEADS, KEY_TILES * Q_TILE, Q_TILE))],
        out_specs=q_spec,
        out_shape=heads_shape,
        compiler_params=cparams(dimension_semantics=("arbitrary", "arbitrary")),
        name="attn",
    )(q, k, k, k, vt, vt, vt, bias_tiles)

    out = pl.pallas_call(
        _mix_mlp_kernel,
        grid=(n_row_tiles,),
        in_specs=[
            pl.BlockSpec((ROW_TILE, d), lambda i: (i, 0)),
            heads_spec,
            pl.BlockSpec((ROW_TILE, d), lambda i: (i, 0)),
            _resident((1, d)),
            _resident((d, 2 * d)),
            _resident((1, 2 * d)),
            _resident((d, d)),
            _resident((d, d)),
            _resident((d, d)),
            _resident((1, d)),
            _resident((d, D_FF)),
            _resident((D_FF, d)),
        ],
        out_specs=pl.BlockSpec((ROW_TILE, d), lambda i: (i, 0)),
        out_shape=jax.ShapeDtypeStruct((t, d), _F32),
        compiler_params=cparams(dimension_semantics=("arbitrary",)),
        name="mix_mlp",
    )(xf, attn, yc, row(norm1_g), w_gate.astype(_BF16), row(b_gate), w_attn_proj.astype(_BF16),
      w_conv_proj.astype(_BF16), w_out.astype(_BF16), row(norm2_g), w_up.astype(_BF16),
      w_down.astype(_BF16))
    return out.reshape(b, s, d)
```

```python
import functools
import math

import jax
import jax.numpy as jnp
from jax import lax
from jax.experimental import pallas as pl
from jax.experimental.pallas import tpu as pltpu

D_MODEL = 1024
N_HEADS = 16
HEAD_DIM = 64
CHUNK = 64
N_PREV_CHUNKS = 8
MAX_REL = 256
CONV_WIDTH = 3
D_FF = 4 * D_MODEL
EPS = 1e-6
NEG_INF = -1e30
LOG2E = math.log2(math.e)

LANES = 128
BF16_ROWS = 16
HEAD_PAIRS = D_MODEL // LANES
MXU_TILE = 256

ROW_TILE = 512
PROJ_TILE = 2 * MXU_TILE
CONV_TILE = MXU_TILE
Q_TILE = 256
KEY_TILES = 1 + (N_PREV_CHUNKS * CHUNK) // Q_TILE
BAND_ROWS = (N_PREV_CHUNKS + LANES // CHUNK) * CHUNK
BIAS_LANES = (KEY_TILES + 1) * Q_TILE
assert 2 * MAX_REL + 1 <= BIAS_LANES and Q_TILE - 1 <= MAX_REL
SCORE_LOOKAHEAD = 3
FF_TILE = 1024
CARRY_ROWS = 8
VMEM_LIMIT = 56 * 1024 * 1024

_BF16 = jnp.bfloat16
_F32 = jnp.float32
_NT_DIMS = (((1,), (1,)), ((), ()))


def _resident(shape):
    return pl.BlockSpec(shape, lambda *_: (0,) * len(shape), pipeline_mode=pl.Buffered(1))


def _rms_norm_rows(x, g):
    ms = jnp.mean(x * x, axis=-1, keepdims=True)
    return x * lax.rsqrt(ms + EPS) * g


def _proj_kernel(x_ref, g1_ref, wqk_ref, wvt_ref, wc_ref, gq_ref, gk_ref, ones_ref, cw_ref,
                 cb_ref, q_ref, k_ref, vt_ref, yc_ref, u_scr, *, tiles_per_seq):
    tm = x_ref.shape[0]
    h = _rms_norm_rows(x_ref[...], g1_ref[...]).astype(_BF16)
    n_col_tiles = D_MODEL // PROJ_TILE
    slabs = PROJ_TILE // LANES

    def proj(w_ref, j, c, width=PROJ_TILE):
        lo = j * D_MODEL + c * width
        return jnp.dot(h, w_ref[:, lo:lo + width], preferred_element_type=_F32)

    @pl.when(pl.program_id(0) % tiles_per_seq == 0)
    def _():
        u_scr[0:CARRY_ROWS, :] = jnp.zeros((CARRY_ROWS, D_MODEL), _F32)

    for c in range(D_MODEL // CONV_TILE):
        cols = slice(c * CONV_TILE, (c + 1) * CONV_TILE)
        bg = proj(wc_ref, 0, c, CONV_TILE)
        u = proj(wc_ref, 1, c, CONV_TILE) * proj(wc_ref, 2, c, CONV_TILE)
        u_scr[CARRY_ROWS:CARRY_ROWS + tm, cols] = u
        u1 = u_scr[CARRY_ROWS - 1:CARRY_ROWS - 1 + tm, cols]
        u2 = u_scr[CARRY_ROWS - 2:CARRY_ROWS - 2 + tm, cols]
        conv = (cb_ref[:, cols] + cw_ref[0:1, cols] * u2 + cw_ref[1:2, cols] * u1
                + cw_ref[2:3, cols] * u)
        yc_ref[:, cols] = (bg * conv).astype(_BF16)
    u_scr[0:CARRY_ROWS, :] = u_scr[tm:tm + CARRY_ROWS, :]

    def finish_head_norm(p, g_ref, o_ref, c):
        p2 = (p * p).astype(_BF16)
        ss = jnp.concatenate(
            [jnp.dot(p2[:, t * MXU_TILE:(t + 1) * MXU_TILE], ones_ref[...],
                     preferred_element_type=_F32) for t in range(PROJ_TILE // MXU_TILE)], axis=-1)
        g = g_ref[:, c * PROJ_TILE:(c + 1) * PROJ_TILE]
        pn = (p * lax.rsqrt(ss * (1.0 / HEAD_DIM) + EPS) * g).astype(_BF16)
        for sl in range(slabs):
            o_ref[c * slabs + sl] = pn[:, sl * LANES:(sl + 1) * LANES]

    waiting = None
    for j, (g_ref, o_ref) in enumerate(((gq_ref, q_ref), (gk_ref, k_ref))):
        for c in range(n_col_tiles):
            p = proj(wqk_ref, j, c)
            if waiting is not None:
                finish_head_norm(*waiting)
            waiting = (p, g_ref, o_ref, c)

    for c in range(n_col_tiles):
        rows = slice(c * PROJ_TILE, (c + 1) * PROJ_TILE)
        vt = lax.dot_general(wvt_ref[rows, :], h, _NT_DIMS,
                             preferred_element_type=_F32).astype(_BF16)
        if waiting is not None:
            finish_head_norm(*waiting)
            waiting = None
        for sl in range(slabs):
            vt_ref[c * slabs + sl] = vt[sl * LANES:(sl + 1) * LANES, :]


def _bias_kernel(rb_ref, o_ref):
    _, kw, tq = o_ref.shape
    rb = rb_ref[0]
    lane = lax.broadcasted_iota(jnp.int32, rb.shape, 1)
    top = rb[:, 2 * MAX_REL:2 * MAX_REL + 1]
    fwd = jnp.where(lane > 2 * MAX_REL, top, rb) * LOG2E
    rolled = pltpu.roll(jnp.broadcast_to(fwd, (kw, BIAS_LANES)), tq, axis=1,
                        stride=1, stride_axis=0)
    kj = lax.broadcasted_iota(jnp.int32, (kw, tq), 0)
    qi = lax.broadcasted_iota(jnp.int32, (kw, tq), 1)
    dchunk = (qi // CHUNK + N_PREV_CHUNKS) - kj // CHUNK
    band = (dchunk >= 0) & (dchunk <= N_PREV_CHUNKS)
    o_ref[0] = jnp.where(band, rolled[:, :tq], NEG_INF)


def _attn_kernel(q_ref, k0_ref, k1_ref, k2_ref, v0_ref, v1_ref, v2_ref, bias_ref, o_ref):
    k_refs = (k0_ref, k1_ref, k2_ref)
    v_refs = (v0_ref, v1_ref, v2_ref)
    tq = q_ref.shape[1]
    lane = lax.broadcasted_iota(jnp.int32, (tq, LANES), 1)
    ones_rows = jnp.ones((BF16_ROWS, tq), _BF16)

    def run(tiles):
        lo, hi = tiles[0] * tq, (tiles[-1] + 1) * tq

        def scores(head):
            hp, hh = divmod(head, 2)
            sel = (lane < HEAD_DIM) if hh == 0 else (lane >= HEAD_DIM)
            q = q_ref[hp]
            qh = jnp.where(sel, q, jnp.zeros_like(q))
            keys = jnp.concatenate([k_refs[t][hp] for t in tiles], axis=0)
            s = lax.dot_general(keys, qh, _NT_DIMS, preferred_element_type=_F32)
            cols = []
            for c in range(tq // LANES):
                r0 = max(lo, c * LANES)
                r1 = min(hi, c * LANES + BAND_ROWS)
                cols.append((r0, r1, s[r0 - lo:r1 - lo, c * LANES:(c + 1) * LANES]
                             + bias_ref[head, r0:r1, c * LANES:(c + 1) * LANES]))
            return cols

        def attend(head, cols):
            hp, hh = divmod(head, 2)
            p_cols = []
            for r0, r1, s in cols:
                m = jnp.max(s, axis=0, keepdims=True)
                p = jnp.exp2(s - m).astype(_BF16)
                pads = [jnp.zeros((n, LANES), _BF16) for n in (r0 - lo, hi - r1)]
                p_cols.append(jnp.concatenate(
                    [blk for blk in (pads[0], p, pads[1]) if blk.shape[0]], axis=0))
            p = jnp.concatenate(p_cols, axis=1)
            v_ext = jnp.concatenate(
                [jnp.concatenate([v_refs[t][hp, hh * HEAD_DIM:(hh + 1) * HEAD_DIM, :],
                                  ones_rows], axis=0) for t in tiles], axis=1)
            acc = jnp.dot(v_ext, p, preferred_element_type=_F32)
            return acc[:HEAD_DIM] * (1.0 / acc[HEAD_DIM:HEAD_DIM + 1])

        pending = [scores(head) for head in range(SCORE_LOOKAHEAD)]
        halves = []
        for head in range(N_HEADS):
            if head + SCORE_LOOKAHEAD < N_HEADS:
                pending.append(scores(head + SCORE_LOOKAHEAD))
            halves.append(attend(head, pending.pop(0)))
            if head % 2:
                o_ref[head // 2] = jnp.concatenate(halves, axis=0).T.astype(_BF16)
                halves = []

    qb = pl.program_id(1)
    for first_tile in range(KEY_TILES):
        n_missing = first_tile
        cond = (qb == (KEY_TILES - 1 - n_missing)) if n_missing else (qb >= KEY_TILES - 1)
        pl.when(cond)(functools.partial(run, tuple(range(first_tile, KEY_TILES))))


def _mix_mlp_kernel(x_ref, a_ref, yc_ref, g1_ref, wg_ref, bgate_ref, wap_ref, wcp_ref,
                    wout_ref, g2_ref, wup_ref, wdown_ref, o_ref):
    x = x_ref[...]
    h = _rms_norm_rows(x, g1_ref[...]).astype(_BF16)
    gates = jax.nn.sigmoid(jnp.dot(h, wg_ref[...], preferred_element_type=_F32) + bgate_ref[...])
    a = jnp.concatenate([a_ref[hp] for hp in range(HEAD_PAIRS)], axis=-1)
    ya = jnp.dot(a, wap_ref[...], preferred_element_type=_F32)
    yc = jnp.dot(yc_ref[...], wcp_ref[...], preferred_element_type=_F32)
    merged = (gates[:, :D_MODEL] * ya + gates[:, D_MODEL:] * yc).astype(_BF16)
    x1 = x + jnp.dot(merged, wout_ref[...], preferred_element_type=_F32)
    h2 = _rms_norm_rows(x1, g2_ref[...]).astype(_BF16)
    acc = x1
    for f in range(D_FF // FF_TILE):
        up = jnp.dot(h2, wup_ref[:, f * FF_TILE:(f + 1) * FF_TILE], preferred_element_type=_F32)
        act = jnp.square(jnp.maximum(up, 0.0)).astype(_BF16)
        acc = acc + jnp.dot(act, wdown_ref[f * FF_TILE:(f + 1) * FF_TILE, :],
                            preferred_element_type=_F32)
    o_ref[...] = acc


def kernel(x, norm1_g, w_in, q_norm_g, k_norm_g, rel_bias, conv_w, conv_b, w_attn_proj,
           w_conv_proj, w_gate, b_gate, w_out, norm2_g, w_up, w_down):
    b, s, d = x.shape
    assert d == D_MODEL and s % ROW_TILE == 0 and s % Q_TILE == 0
    t = b * s
    xf = x.reshape(t, d)
    row = lambda v: v.reshape(1, -1).astype(_F32)
    cparams = functools.partial(pltpu.CompilerParams, vmem_limit_bytes=VMEM_LIMIT)

    gq = row(jnp.tile(q_norm_g, N_HEADS) * (HEAD_DIM ** -0.5 * LOG2E))
    gk = row(jnp.tile(k_norm_g, N_HEADS))
    r = jnp.arange(MXU_TILE)
    head_ones = (r[:, None] // HEAD_DIM == r[None, :] // HEAD_DIM).astype(_BF16)
    n_row_tiles = t // ROW_TILE
    heads_shape = jax.ShapeDtypeStruct((HEAD_PAIRS, t, LANES), _BF16)
    heads_spec = pl.BlockSpec((HEAD_PAIRS, ROW_TILE, LANES), lambda i: (0, i, 0))
    q, k, vt, yc = pl.pallas_call(
        functools.partial(_proj_kernel, tiles_per_seq=s // ROW_TILE),
        grid=(n_row_tiles,),
        in_specs=[
            pl.BlockSpec((ROW_TILE, d), lambda i: (i, 0)),
            _resident((1, d)),
            _resident((d, 2 * d)),
            _resident((d, d)),
            _resident((d, 3 * d)),
            _resident((1, d)),
            _resident((1, d)),
            _resident((MXU_TILE, MXU_TILE)),
            _resident((CONV_WIDTH, d)),
            _resident((1, d)),
        ],
        out_specs=[heads_spec, heads_spec,
                   pl.BlockSpec((HEAD_PAIRS, LANES, ROW_TILE), lambda i: (0, 0, i)),
                   pl.BlockSpec((ROW_TILE, d), lambda i: (i, 0))],
        out_shape=[heads_shape, heads_shape,
                   jax.ShapeDtypeStruct((HEAD_PAIRS, LANES, t), _BF16),
                   jax.ShapeDtypeStruct((t, d), _BF16)],
        scratch_shapes=[pltpu.VMEM((ROW_TILE + CARRY_ROWS, d), _F32)],
        compiler_params=cparams(dimension_semantics=("arbitrary",)),
        name="proj",
    )(xf, row(norm1_g), w_in[:, :2 * d].astype(_BF16), w_in[:, 2 * d:3 * d].T.astype(_BF16),
      w_in[:, 3 * d:].astype(_BF16), gq, gk, head_ones, conv_w.astype(_F32), row(conv_b))

    n_rel = rel_bias.shape[1]
    rb_rows = jnp.pad(rel_bias.astype(_F32), ((0, 0), (0, BIAS_LANES - n_rel)))
    bias_tiles = pl.pallas_call(
        _bias_kernel,
        grid=(N_HEADS,),
        in_specs=[pl.BlockSpec((1, 1, BIAS_LANES), lambda hd: (hd, 0, 0))],
        out_specs=pl.BlockSpec((1, KEY_TILES * Q_TILE, Q_TILE), lambda hd: (hd, 0, 0)),
        out_shape=jax.ShapeDtypeStruct((N_HEADS, KEY_TILES * Q_TILE, Q_TILE), _F32),
        compiler_params=cparams(dimension_semantics=("arbitrary",)),
        name="bias_tiles",
    )(rb_rows.reshape(N_HEADS, 1, BIAS_LANES))

    nq = s // Q_TILE
    q_spec = pl.BlockSpec((HEAD_PAIRS, Q_TILE, LANES), lambda bi, qi: (0, bi * nq + qi, 0))

    def key_tile(qi, tile):
        return jnp.maximum(qi - (KEY_TILES - 1 - tile), 0)

    def k_spec(tile):
        return pl.BlockSpec((HEAD_PAIRS, Q_TILE, LANES),
                            lambda bi, qi: (0, bi * nq + key_tile(qi, tile), 0))

    def vt_spec(tile):
        return pl.BlockSpec((HEAD_PAIRS, LANES, Q_TILE),
                            lambda bi, qi: (0, 0, bi * nq + key_tile(qi, tile)))

    attn = pl.pallas_call(
        _attn_kernel,
        grid=(b, nq),
        in_specs=[q_spec] + [k_spec(tl) for tl in range(KEY_TILES)]
                 + [vt_spec(tl) for tl in range(KEY_TILES)]
                 + [_resident((N_HEADS, KEY_TILES * Q_TILE, Q_TILE))],
        out_specs=q_spec,
        out_shape=heads_shape,
        compiler_params=cparams(dimension_semantics=("arbitrary", "arbitrary")),
        name="attn",
    )(q, k, k, k, vt, vt, vt, bias_tiles)

    out = pl.pallas_call(
        _mix_mlp_kernel,
        grid=(n_row_tiles,),
        in_specs=[
            pl.BlockSpec((ROW_TILE, d), lambda i: (i, 0)),
            heads_spec,
            pl.BlockSpec((ROW_TILE, d), lambda i: (i, 0)),
            _resident((1, d)),
            _resident((d, 2 * d)),
            _resident((1, 2 * d)),
            _resident((d, d)),
            _resident((d, d)),
            _resident((d, d)),
            _resident((1, d)),
            _resident((d, D_FF)),
            _resident((D_FF, d)),
        ],
        out_specs=pl.BlockSpec((ROW_TILE, d), lambda i: (i, 0)),
        out_shape=jax.ShapeDtypeStruct((t, d), _F32),
        compiler_params=cparams(dimension_semantics=("arbitrary",)),
        name="mix_mlp",
    )(xf, attn, yc, row(norm1_g), w_gate.astype(_BF16), row(b_gate), w_attn_proj.astype(_BF16),
      w_conv_proj.astype(_BF16), w_out.astype(_BF16), row(norm2_g), w_up.astype(_BF16),
      w_down.astype(_BF16))
    return out.reshape(b, s, d)
```

```python
import functools
import math

import jax
import jax.numpy as jnp
from jax import lax
from jax.experimental import pallas as pl
from jax.experimental.pallas import tpu as pltpu

D_MODEL = 1024
N_HEADS = 16
HEAD_DIM = 64
CHUNK = 64
N_PREV_CHUNKS = 8
MAX_REL = 256
CONV_WIDTH = 3
D_FF = 4 * D_MODEL
EPS = 1e-6
NEG_INF = -1e30
LOG2E = math.log2(math.e)

LANES = 128
BF16_ROWS = 16
HEAD_PAIRS = D_MODEL // LANES
MXU_TILE = 256

ROW_TILE = 512
PROJ_TILE = 2 * MXU_TILE
CONV_TILE = MXU_TILE
Q_TILE = 256
KEY_TILES = 1 + (N_PREV_CHUNKS * CHUNK) // Q_TILE
Q_TILES_PER_STEP = 2
KEY_REFS = KEY_TILES + Q_TILES_PER_STEP - 1
BAND_ROWS = (N_PREV_CHUNKS + LANES // CHUNK) * CHUNK
BIAS_LANES = (KEY_TILES + 1) * Q_TILE
assert 2 * MAX_REL + 1 <= BIAS_LANES and Q_TILE - 1 <= MAX_REL
SCORE_LOOKAHEAD = 3
FF_TILE = 1024
CARRY_ROWS = 8
VMEM_LIMIT = 56 * 1024 * 1024

_BF16 = jnp.bfloat16
_F32 = jnp.float32
_NT_DIMS = (((1,), (1,)), ((), ()))


def _resident(shape):
    return pl.BlockSpec(shape, lambda *_: (0,) * len(shape), pipeline_mode=pl.Buffered(1))


def _rms_norm_rows(x, g):
    ms = jnp.mean(x * x, axis=-1, keepdims=True)
    return x * lax.rsqrt(ms + EPS) * g


def _proj_kernel(x_ref, g1_ref, wqk_ref, wvt_ref, wc_ref, gq_ref, gk_ref, ones_ref, cw_ref,
                 cb_ref, q_ref, k_ref, vt_ref, yc_ref, u_scr, *, tiles_per_seq):
    tm = x_ref.shape[0]
    h = _rms_norm_rows(x_ref[...], g1_ref[...]).astype(_BF16)
    n_col_tiles = D_MODEL // PROJ_TILE
    slabs = PROJ_TILE // LANES

    def proj(w_ref, j, c, width=PROJ_TILE):
        lo = j * D_MODEL + c * width
        return jnp.dot(h, w_ref[:, lo:lo + width], preferred_element_type=_F32)

    @pl.when(pl.program_id(0) % tiles_per_seq == 0)
    def _():
        u_scr[0:CARRY_ROWS, :] = jnp.zeros((CARRY_ROWS, D_MODEL), _F32)

    for c in range(D_MODEL // CONV_TILE):
        cols = slice(c * CONV_TILE, (c + 1) * CONV_TILE)
        bg = proj(wc_ref, 0, c, CONV_TILE)
        u = proj(wc_ref, 1, c, CONV_TILE) * proj(wc_ref, 2, c, CONV_TILE)
        u_scr[CARRY_ROWS:CARRY_ROWS + tm, cols] = u
        u1 = u_scr[CARRY_ROWS - 1:CARRY_ROWS - 1 + tm, cols]
        u2 = u_scr[CARRY_ROWS - 2:CARRY_ROWS - 2 + tm, cols]
        conv = (cb_ref[:, cols] + cw_ref[0:1, cols] * u2 + cw_ref[1:2, cols] * u1
                + cw_ref[2:3, cols] * u)
        yc_ref[:, cols] = (bg * conv).astype(_BF16)
    u_scr[0:CARRY_ROWS, :] = u_scr[tm:tm + CARRY_ROWS, :]

    def finish_head_norm(p, g_ref, o_ref, c):
        p2 = (p * p).astype(_BF16)
        ss = jnp.concatenate(
            [jnp.dot(p2[:, t * MXU_TILE:(t + 1) * MXU_TILE], ones_ref[...],
                     preferred_element_type=_F32) for t in range(PROJ_TILE // MXU_TILE)], axis=-1)
        g = g_ref[:, c * PROJ_TILE:(c + 1) * PROJ_TILE]
        pn = (p * lax.rsqrt(ss * (1.0 / HEAD_DIM) + EPS) * g).astype(_BF16)
        for sl in range(slabs):
            o_ref[c * slabs + sl] = pn[:, sl * LANES:(sl + 1) * LANES]

    waiting = None
    for j, (g_ref, o_ref) in enumerate(((gq_ref, q_ref), (gk_ref, k_ref))):
        for c in range(n_col_tiles):
            p = proj(wqk_ref, j, c)
            if waiting is not None:
                finish_head_norm(*waiting)
            waiting = (p, g_ref, o_ref, c)

    for c in range(n_col_tiles):
        rows = slice(c * PROJ_TILE, (c + 1) * PROJ_TILE)
        vt = lax.dot_general(wvt_ref[rows, :], h, _NT_DIMS,
                             preferred_element_type=_F32).astype(_BF16)
        if waiting is not None:
            finish_head_norm(*waiting)
            waiting = None
        for sl in range(slabs):
            vt_ref[c * slabs + sl] = vt[sl * LANES:(sl + 1) * LANES, :]


def _bias_kernel(rb_ref, o_ref):
    _, kw, tq = o_ref.shape
    rb = rb_ref[0]
    lane = lax.broadcasted_iota(jnp.int32, rb.shape, 1)
    top = rb[:, 2 * MAX_REL:2 * MAX_REL + 1]
    fwd = jnp.where(lane > 2 * MAX_REL, top, rb) * LOG2E
    rolled = pltpu.roll(jnp.broadcast_to(fwd, (kw, BIAS_LANES)), tq, axis=1,
                        stride=1, stride_axis=0)
    kj = lax.broadcasted_iota(jnp.int32, (kw, tq), 0)
    qi = lax.broadcasted_iota(jnp.int32, (kw, tq), 1)
    dchunk = (qi // CHUNK + N_PREV_CHUNKS) - kj // CHUNK
    band = (dchunk >= 0) & (dchunk <= N_PREV_CHUNKS)
    o_ref[0] = jnp.where(band, rolled[:, :tq], NEG_INF)


def _attn_kernel(q_ref, *refs):
    k_refs, v_refs = refs[:KEY_REFS], refs[KEY_REFS:2 * KEY_REFS]
    bias_ref, o_ref = refs[2 * KEY_REFS:]
    tq = Q_TILE
    lane = lax.broadcasted_iota(jnp.int32, (tq, LANES), 1)
    ones_rows = jnp.ones((BF16_ROWS, tq), _BF16)

    def scores(sub, tiles, head):
        lo, hi = tiles[0] * tq, (tiles[-1] + 1) * tq
        hp, hh = divmod(head, 2)
        sel = (lane < HEAD_DIM) if hh == 0 else (lane >= HEAD_DIM)
        q = q_ref[hp, sub * tq:(sub + 1) * tq, :]
        qh = jnp.where(sel, q, jnp.zeros_like(q))
        keys = jnp.concatenate([k_refs[sub + t][hp] for t in tiles], axis=0)
        s = lax.dot_general(keys, qh, _NT_DIMS, preferred_element_type=_F32)
        cols = []
        for c in range(tq // LANES):
            r0 = max(lo, c * LANES)
            r1 = min(hi, c * LANES + BAND_ROWS)
            cols.append((r0, r1, s[r0 - lo:r1 - lo, c * LANES:(c + 1) * LANES]
                         + bias_ref[head, r0:r1, c * LANES:(c + 1) * LANES]))
        return cols

    def attend(sub, tiles, head, cols):
        lo, hi = tiles[0] * tq, (tiles[-1] + 1) * tq
        hp, hh = divmod(head, 2)
        p_cols = []
        for r0, r1, s in cols:
            m = jnp.max(s, axis=0, keepdims=True)
            p = jnp.exp2(s - m).astype(_BF16)
            pads = [jnp.zeros((n, LANES), _BF16) for n in (r0 - lo, hi - r1)]
            p_cols.append(jnp.concatenate(
                [blk for blk in (pads[0], p, pads[1]) if blk.shape[0]], axis=0))
        p = jnp.concatenate(p_cols, axis=1)
        v_ext = jnp.concatenate(
            [jnp.concatenate([v_refs[sub + t][hp, hh * HEAD_DIM:(hh + 1) * HEAD_DIM, :],
                              ones_rows], axis=0) for t in tiles], axis=1)
        acc = jnp.dot(v_ext, p, preferred_element_type=_F32)
        return acc[:HEAD_DIM] * (1.0 / acc[HEAD_DIM:HEAD_DIM + 1])

    def run(tiles_per_sub):
        work = [(sub, tiles, head) for sub, tiles in enumerate(tiles_per_sub)
                for head in range(N_HEADS)]
        pending = [scores(*w) for w in work[:SCORE_LOOKAHEAD]]
        halves = []
        for n, (sub, tiles, head) in enumerate(work):
            if n + SCORE_LOOKAHEAD < len(work):
                pending.append(scores(*work[n + SCORE_LOOKAHEAD]))
            halves.append(attend(sub, tiles, head, pending.pop(0)))
            if head % 2:
                o_ref[head // 2, sub * tq:(sub + 1) * tq, :] = (
                    jnp.concatenate(halves, axis=0).T.astype(_BF16))
                halves = []

    all_tiles = tuple(range(KEY_TILES))
    first_step = tuple(all_tiles[max(KEY_TILES - 1 - j, 0):] for j in range(Q_TILES_PER_STEP))
    qs = pl.program_id(1)
    pl.when(qs == 0)(functools.partial(run, first_step))
    pl.when(qs > 0)(functools.partial(run, (all_tiles,) * Q_TILES_PER_STEP))


def _mix_mlp_kernel(x_ref, a_ref, yc_ref, g1_ref, wg_ref, bgate_ref, wap_ref, wcp_ref,
                    wout_ref, g2_ref, wup_ref, wdown_ref, o_ref):
    x = x_ref[...]
    h = _rms_norm_rows(x, g1_ref[...]).astype(_BF16)
    gates = jax.nn.sigmoid(jnp.dot(h, wg_ref[...], preferred_element_type=_F32) + bgate_ref[...])
    a = jnp.concatenate([a_ref[hp] for hp in range(HEAD_PAIRS)], axis=-1)
    ya = jnp.dot(a, wap_ref[...], preferred_element_type=_F32)
    yc = jnp.dot(yc_ref[...], wcp_ref[...], preferred_element_type=_F32)
    merged = (gates[:, :D_MODEL] * ya + gates[:, D_MODEL:] * yc).astype(_BF16)
    x1 = x + jnp.dot(merged, wout_ref[...], preferred_element_type=_F32)
    h2 = _rms_norm_rows(x1, g2_ref[...]).astype(_BF16)
    acc = x1
    for f in range(D_FF // FF_TILE):
        up = jnp.dot(h2, wup_ref[:, f * FF_TILE:(f + 1) * FF_TILE], preferred_element_type=_F32)
        act = jnp.square(jnp.maximum(up, 0.0)).astype(_BF16)
        acc = acc + jnp.dot(act, wdown_ref[f * FF_TILE:(f + 1) * FF_TILE, :],
                            preferred_element_type=_F32)
    o_ref[...] = acc


def kernel(x, norm1_g, w_in, q_norm_g, k_norm_g, rel_bias, conv_w, conv_b, w_attn_proj,
           w_conv_proj, w_gate, b_gate, w_out, norm2_g, w_up, w_down):
    b, s, d = x.shape
    assert d == D_MODEL and s % ROW_TILE == 0 and s % (Q_TILE * Q_TILES_PER_STEP) == 0
    t = b * s
    xf = x.reshape(t, d)
    row = lambda v: v.reshape(1, -1).astype(_F32)
    cparams = functools.partial(pltpu.CompilerParams, vmem_limit_bytes=VMEM_LIMIT)

    gq = row(jnp.tile(q_norm_g, N_HEADS) * (HEAD_DIM ** -0.5 * LOG2E))
    gk = row(jnp.tile(k_norm_g, N_HEADS))
    r = jnp.arange(MXU_TILE)
    head_ones = (r[:, None] // HEAD_DIM == r[None, :] // HEAD_DIM).astype(_BF16)
    n_row_tiles = t // ROW_TILE
    heads_shape = jax.ShapeDtypeStruct((HEAD_PAIRS, t, LANES), _BF16)
    heads_spec = pl.BlockSpec((HEAD_PAIRS, ROW_TILE, LANES), lambda i: (0, i, 0))
    q, k, vt, yc = pl.pallas_call(
        functools.partial(_proj_kernel, tiles_per_seq=s // ROW_TILE),
        grid=(n_row_tiles,),
        in_specs=[
            pl.BlockSpec((ROW_TILE, d), lambda i: (i, 0)),
            _resident((1, d)),
            _resident((d, 2 * d)),
            _resident((d, d)),
            _resident((d, 3 * d)),
            _resident((1, d)),
            _resident((1, d)),
            _resident((MXU_TILE, MXU_TILE)),
            _resident((CONV_WIDTH, d)),
            _resident((1, d)),
        ],
        out_specs=[heads_spec, heads_spec,
                   pl.BlockSpec((HEAD_PAIRS, LANES, ROW_TILE), lambda i: (0, 0, i)),
                   pl.BlockSpec((ROW_TILE, d), lambda i: (i, 0))],
        out_shape=[heads_shape, heads_shape,
                   jax.ShapeDtypeStruct((HEAD_PAIRS, LANES, t), _BF16),
                   jax.ShapeDtypeStruct((t, d), _BF16)],
        scratch_shapes=[pltpu.VMEM((ROW_TILE + CARRY_ROWS, d), _F32)],
        compiler_params=cparams(dimension_semantics=("arbitrary",)),
        name="proj",
    )(xf, row(norm1_g), w_in[:, :2 * d].astype(_BF16), w_in[:, 2 * d:3 * d].T.astype(_BF16),
      w_in[:, 3 * d:].astype(_BF16), gq, gk, head_ones, conv_w.astype(_F32), row(conv_b))

    n_rel = rel_bias.shape[1]
    rb_rows = jnp.pad(rel_bias.astype(_F32), ((0, 0), (0, BIAS_LANES - n_rel)))
    bias_tiles = pl.pallas_call(
        _bias_kernel,
        grid=(N_HEADS,),
        in_specs=[pl.BlockSpec((1, 1, BIAS_LANES), lambda hd: (hd, 0, 0))],
        out_specs=pl.BlockSpec((1, KEY_TILES * Q_TILE, Q_TILE), lambda hd: (hd, 0, 0)),
        out_shape=jax.ShapeDtypeStruct((N_HEADS, KEY_TILES * Q_TILE, Q_TILE), _F32),
        compiler_params=cparams(dimension_semantics=("arbitrary",)),
        name="bias_tiles",
    )(rb_rows.reshape(N_HEADS, 1, BIAS_LANES))

    nq = s // Q_TILE
    ns = nq // Q_TILES_PER_STEP
    q_spec = pl.BlockSpec((HEAD_PAIRS, Q_TILES_PER_STEP * Q_TILE, LANES),
                          lambda bi, si: (0, bi * ns + si, 0))

    def key_tile(si, ref_idx):
        return jnp.maximum(si * Q_TILES_PER_STEP - (KEY_TILES - 1) + ref_idx, 0)

    def k_spec(ref_idx):
        return pl.BlockSpec((HEAD_PAIRS, Q_TILE, LANES),
                            lambda bi, si: (0, bi * nq + key_tile(si, ref_idx), 0))

    def vt_spec(ref_idx):
        return pl.BlockSpec((HEAD_PAIRS, LANES, Q_TILE),
                            lambda bi, si: (0, 0, bi * nq + key_tile(si, ref_idx)))

    attn = pl.pallas_call(
        _attn_kernel,
        grid=(b, ns),
        in_specs=[q_spec] + [k_spec(r) for r in range(KEY_REFS)]
                 + [vt_spec(r) for r in range(KEY_REFS)]
                 + [_resident((N_HEADS, KEY_TILES * Q_TILE, Q_TILE))],
        out_specs=q_spec,
        out_shape=heads_shape,
        compiler_params=cparams(dimension_semantics=("arbitrary", "arbitrary")),
        name="attn",
    )(q, *([k] * KEY_REFS), *([vt] * KEY_REFS), bias_tiles)

    out = pl.pallas_call(
        _mix_mlp_kernel,
        grid=(n_row_tiles,),
        in_specs=[
            pl.BlockSpec((ROW_TILE, d), lambda i: (i, 0)),
            heads_spec,
            pl.BlockSpec((ROW_TILE, d), lambda i: (i, 0)),
            _resident((1, d)),
            _resident((d, 2 * d)),
            _resident((1, 2 * d)),
            _resident((d, d)),
            _resident((d, d)),
            _resident((d, d)),
            _resident((1, d)),
            _resident((d, D_FF)),
            _resident((D_FF, d)),
        ],
        out_specs=pl.BlockSpec((ROW_TILE, d), lambda i: (i, 0)),
        out_shape=jax.ShapeDtypeStruct((t, d), _F32),
        compiler_params=cparams(dimension_semantics=("arbitrary",)),
        name="mix_mlp",
    )(xf, attn, yc, row(norm1_g), w_gate.astype(_BF16), row(b_gate), w_attn_proj.astype(_BF16),
      w_conv_proj.astype(_BF16), w_out.astype(_BF16), row(norm2_g), w_up.astype(_BF16),
      w_down.astype(_BF16))
    return out.reshape(b, s, d)
```

```python
import functools
import math

import jax
import jax.numpy as jnp
from jax import lax
from jax.experimental import pallas as pl
from jax.experimental.pallas import tpu as pltpu

D_MODEL = 1024
N_HEADS = 16
HEAD_DIM = 64
CHUNK = 64
N_PREV_CHUNKS = 8
MAX_REL = 256
CONV_WIDTH = 3
D_FF = 4 * D_MODEL
EPS = 1e-6
NEG_INF = -1e30
LOG2E = math.log2(math.e)

LANES = 128
BF16_ROWS = 16
HEAD_PAIRS = D_MODEL // LANES
MXU_TILE = 256

ROW_TILE = 512
PROJ_TILE = 2 * MXU_TILE
CONV_TILE = MXU_TILE
Q_TILE = 256
KEY_TILES = 1 + (N_PREV_CHUNKS * CHUNK) // Q_TILE
Q_TILES_PER_STEP = 2
KEY_REFS = KEY_TILES + Q_TILES_PER_STEP - 1
BAND_ROWS = (N_PREV_CHUNKS + LANES // CHUNK) * CHUNK
BIAS_LANES = (KEY_TILES + 1) * Q_TILE
assert 2 * MAX_REL + 1 <= BIAS_LANES and Q_TILE - 1 <= MAX_REL
SCORE_LOOKAHEAD = 3
FF_TILE = 1024
CARRY_ROWS = 8
STAGE_ROWS, STAGE_COLS = 256, D_MODEL
VMEM_LIMIT = 56 * 1024 * 1024

_BF16 = jnp.bfloat16
_F32 = jnp.float32
_NT_DIMS = (((1,), (1,)), ((), ()))


def _resident(shape):
    return pl.BlockSpec(shape, lambda *_: (0,) * len(shape), pipeline_mode=pl.Buffered(1))


def _rms_norm_rows(x, g):
    ms = jnp.mean(x * x, axis=-1, keepdims=True)
    return x * lax.rsqrt(ms + EPS) * g


def _packed_weight(shape):
    k, n = shape
    return pltpu.VMEM((k // 2, n), jnp.uint32)


def _pack(block):
    return pltpu.bitcast(block.astype(_BF16), jnp.uint32)


def _unpack(words):
    return pltpu.bitcast(words, _BF16)


def _stage_weights(jobs, stage, sem):
    def copy(n):
        src, r0, c0, _ = jobs[n]
        return pltpu.make_async_copy(src.at[pl.ds(r0, STAGE_ROWS), pl.ds(c0, STAGE_COLS)],
                                     stage.at[n % 2], sem.at[n % 2])

    copy(0).start()
    for n, job in enumerate(jobs):
        if n + 1 < len(jobs):
            copy(n + 1).start()
        copy(n).wait()
        job[3](stage[n % 2])


def _weight_jobs(src, dst, row_lo=0, col_lo=0, shape=None):
    k, n = shape if shape is not None else src.shape
    jobs = []
    for r in range(0, k, STAGE_ROWS):
        for c in range(0, n, STAGE_COLS):
            def store(block, r=r, c=c):
                dst[r // 2:(r + STAGE_ROWS) // 2, c:c + STAGE_COLS] = _pack(block)
            jobs.append((src, row_lo + r, col_lo + c, store))
    return jobs


def _proj_kernel(x_ref, g1_ref, w_in_hbm, gq_ref, gk_ref, ones_ref, cw_ref, cb_ref,
                 q_ref, k_ref, vt_ref, yc_ref,
                 u_scr, wqk_ref, wvt_ref, wc_ref, stage, sem, *, tiles_per_seq):
    tm = x_ref.shape[0]

    @pl.when(pl.program_id(0) == 0)
    def _():
        def store_vt(block, r):
            wvt_ref[:, r:r + STAGE_ROWS] = _pack(block.T)

        jobs = _weight_jobs(w_in_hbm, wqk_ref, 0, 0, (D_MODEL, 2 * D_MODEL))
        jobs += _weight_jobs(w_in_hbm, wc_ref, 0, 3 * D_MODEL, (D_MODEL, 3 * D_MODEL))
        jobs += [(w_in_hbm, r, 2 * D_MODEL, functools.partial(store_vt, r=r))
                 for r in range(0, D_MODEL, STAGE_ROWS)]
        _stage_weights(jobs, stage, sem)

    h = _rms_norm_rows(x_ref[...], g1_ref[...]).astype(_BF16)
    n_col_tiles = D_MODEL // PROJ_TILE
    slabs = PROJ_TILE // LANES

    def proj(w_ref, j, c, width=PROJ_TILE):
        lo = j * D_MODEL + c * width
        return jnp.dot(h, _unpack(w_ref[:, lo:lo + width]), preferred_element_type=_F32)

    @pl.when(pl.program_id(0) % tiles_per_seq == 0)
    def _():
        u_scr[0:CARRY_ROWS, :] = jnp.zeros((CARRY_ROWS, D_MODEL), _F32)

    for c in range(D_MODEL // CONV_TILE):
        cols = slice(c * CONV_TILE, (c + 1) * CONV_TILE)
        bg = proj(wc_ref, 0, c, CONV_TILE)
        u = proj(wc_ref, 1, c, CONV_TILE) * proj(wc_ref, 2, c, CONV_TILE)
        u_scr[CARRY_ROWS:CARRY_ROWS + tm, cols] = u
        u1 = u_scr[CARRY_ROWS - 1:CARRY_ROWS - 1 + tm, cols]
        u2 = u_scr[CARRY_ROWS - 2:CARRY_ROWS - 2 + tm, cols]
        conv = (cb_ref[:, cols] + cw_ref[0:1, cols] * u2 + cw_ref[1:2, cols] * u1
                + cw_ref[2:3, cols] * u)
        yc_ref[:, cols] = (bg * conv).astype(_BF16)
    u_scr[0:CARRY_ROWS, :] = u_scr[tm:tm + CARRY_ROWS, :]

    def finish_head_norm(p, g_ref, o_ref, c):
        p2 = (p * p).astype(_BF16)
        ss = jnp.concatenate(
            [jnp.dot(p2[:, t * MXU_TILE:(t + 1) * MXU_TILE], ones_ref[...],
                     preferred_element_type=_F32) for t in range(PROJ_TILE // MXU_TILE)], axis=-1)
        g = g_ref[:, c * PROJ_TILE:(c + 1) * PROJ_TILE]
        pn = (p * lax.rsqrt(ss * (1.0 / HEAD_DIM) + EPS) * g).astype(_BF16)
        for sl in range(slabs):
            o_ref[c * slabs + sl] = pn[:, sl * LANES:(sl + 1) * LANES]

    waiting = None
    for j, (g_ref, o_ref) in enumerate(((gq_ref, q_ref), (gk_ref, k_ref))):
        for c in range(n_col_tiles):
            p = proj(wqk_ref, j, c)
            if waiting is not None:
                finish_head_norm(*waiting)
            waiting = (p, g_ref, o_ref, c)

    for c in range(n_col_tiles):
        rows = slice(c * PROJ_TILE // 2, (c + 1) * PROJ_TILE // 2)
        vt = lax.dot_general(_unpack(wvt_ref[rows, :]), h, _NT_DIMS,
                             preferred_element_type=_F32).astype(_BF16)
        if waiting is not None:
            finish_head_norm(*waiting)
            waiting = None
        for sl in range(slabs):
            vt_ref[c * slabs + sl] = vt[sl * LANES:(sl + 1) * LANES, :]


def _bias_kernel(rb_ref, o_ref):
    _, kw, tq = o_ref.shape
    rb = rb_ref[0]
    lane = lax.broadcasted_iota(jnp.int32, rb.shape, 1)
    top = rb[:, 2 * MAX_REL:2 * MAX_REL + 1]
    fwd = jnp.where(lane > 2 * MAX_REL, top, rb) * LOG2E
    rolled = pltpu.roll(jnp.broadcast_to(fwd, (kw, BIAS_LANES)), tq, axis=1,
                        stride=1, stride_axis=0)
    kj = lax.broadcasted_iota(jnp.int32, (kw, tq), 0)
    qi = lax.broadcasted_iota(jnp.int32, (kw, tq), 1)
    dchunk = (qi // CHUNK + N_PREV_CHUNKS) - kj // CHUNK
    band = (dchunk >= 0) & (dchunk <= N_PREV_CHUNKS)
    o_ref[0] = jnp.where(band, rolled[:, :tq], NEG_INF)


def _attn_kernel(q_ref, *refs):
    k_refs, v_refs = refs[:KEY_REFS], refs[KEY_REFS:2 * KEY_REFS]
    bias_ref, o_ref = refs[2 * KEY_REFS:]
    tq = Q_TILE
    lane = lax.broadcasted_iota(jnp.int32, (tq, LANES), 1)
    ones_rows = jnp.ones((BF16_ROWS, tq), _BF16)

    def scores(sub, tiles, head):
        lo, hi = tiles[0] * tq, (tiles[-1] + 1) * tq
        hp, hh = divmod(head, 2)
        sel = (lane < HEAD_DIM) if hh == 0 else (lane >= HEAD_DIM)
        q = q_ref[hp, sub * tq:(sub + 1) * tq, :]
        qh = jnp.where(sel, q, jnp.zeros_like(q))
        keys = jnp.concatenate([k_refs[sub + t][hp] for t in tiles], axis=0)
        s = lax.dot_general(keys, qh, _NT_DIMS, preferred_element_type=_F32)
        cols = []
        for c in range(tq // LANES):
            r0 = max(lo, c * LANES)
            r1 = min(hi, c * LANES + BAND_ROWS)
            cols.append((r0, r1, s[r0 - lo:r1 - lo, c * LANES:(c + 1) * LANES]
                         + bias_ref[head, r0:r1, c * LANES:(c + 1) * LANES]))
        return cols

    def attend(sub, tiles, head, cols):
        lo, hi = tiles[0] * tq, (tiles[-1] + 1) * tq
        hp, hh = divmod(head, 2)
        p_cols = []
        for r0, r1, s in cols:
            m = jnp.max(s, axis=0, keepdims=True)
            p = jnp.exp2(s - m).astype(_BF16)
            pads = [jnp.zeros((n, LANES), _BF16) for n in (r0 - lo, hi - r1)]
            p_cols.append(jnp.concatenate(
                [blk for blk in (pads[0], p, pads[1]) if blk.shape[0]], axis=0))
        p = jnp.concatenate(p_cols, axis=1)
        v_ext = jnp.concatenate(
            [jnp.concatenate([v_refs[sub + t][hp, hh * HEAD_DIM:(hh + 1) * HEAD_DIM, :],
                              ones_rows], axis=0) for t in tiles], axis=1)
        acc = jnp.dot(v_ext, p, preferred_element_type=_F32)
        return acc[:HEAD_DIM] * (1.0 / acc[HEAD_DIM:HEAD_DIM + 1])

    def run(tiles_per_sub):
        work = [(sub, tiles, head) for sub, tiles in enumerate(tiles_per_sub)
                for head in range(N_HEADS)]
        pending = [scores(*w) for w in work[:SCORE_LOOKAHEAD]]
        halves = []
        for n, (sub, tiles, head) in enumerate(work):
            if n + SCORE_LOOKAHEAD < len(work):
                pending.append(scores(*work[n + SCORE_LOOKAHEAD]))
            halves.append(attend(sub, tiles, head, pending.pop(0)))
            if head % 2:
                o_ref[head // 2, sub * tq:(sub + 1) * tq, :] = (
                    jnp.concatenate(halves, axis=0).T.astype(_BF16))
                halves = []

    all_tiles = tuple(range(KEY_TILES))
    first_step = tuple(all_tiles[max(KEY_TILES - 1 - j, 0):] for j in range(Q_TILES_PER_STEP))
    qs = pl.program_id(1)
    pl.when(qs == 0)(functools.partial(run, first_step))
    pl.when(qs > 0)(functools.partial(run, (all_tiles,) * Q_TILES_PER_STEP))


def _mix_mlp_kernel(x_ref, a_ref, yc_ref, g1_ref, bgate_ref, g2_ref,
                    wg_hbm, wap_hbm, wcp_hbm, wout_hbm, wup_hbm, wdown_hbm, o_ref,
                    wg_ref, wap_ref, wcp_ref, wout_ref, wup_ref, wdown_ref, stage, sem):
    @pl.when(pl.program_id(0) == 0)
    def _():
        pairs = ((wg_hbm, wg_ref), (wap_hbm, wap_ref), (wcp_hbm, wcp_ref), (wout_hbm, wout_ref),
                 (wup_hbm, wup_ref), (wdown_hbm, wdown_ref))
        _stage_weights([job for src, dst in pairs for job in _weight_jobs(src, dst)], stage, sem)

    def dot(lhs, w_ref, rows=slice(None), cols=slice(None)):
        return jnp.dot(lhs, _unpack(w_ref[rows, cols]), preferred_element_type=_F32)

    x = x_ref[...]
    h = _rms_norm_rows(x, g1_ref[...]).astype(_BF16)
    gates = jax.nn.sigmoid(dot(h, wg_ref) + bgate_ref[...])
    a = jnp.concatenate([a_ref[hp] for hp in range(HEAD_PAIRS)], axis=-1)
    ya = dot(a, wap_ref)
    yc = dot(yc_ref[...], wcp_ref)
    merged = (gates[:, :D_MODEL] * ya + gates[:, D_MODEL:] * yc).astype(_BF16)
    x1 = x + dot(merged, wout_ref)
    h2 = _rms_norm_rows(x1, g2_ref[...]).astype(_BF16)
    acc = x1
    for f in range(D_FF // FF_TILE):
        up = dot(h2, wup_ref, cols=slice(f * FF_TILE, (f + 1) * FF_TILE))
        act = jnp.square(jnp.maximum(up, 0.0)).astype(_BF16)
        acc = acc + dot(act, wdown_ref, rows=slice(f * FF_TILE // 2, (f + 1) * FF_TILE // 2))
    o_ref[...] = acc


def kernel(x, norm1_g, w_in, q_norm_g, k_norm_g, rel_bias, conv_w, conv_b, w_attn_proj,
           w_conv_proj, w_gate, b_gate, w_out, norm2_g, w_up, w_down):
    b, s, d = x.shape
    assert d == D_MODEL and s % ROW_TILE == 0 and s % (Q_TILE * Q_TILES_PER_STEP) == 0
    t = b * s
    xf = x.reshape(t, d)
    row = lambda v: v.reshape(1, -1).astype(_F32)
    cparams = functools.partial(pltpu.CompilerParams, vmem_limit_bytes=VMEM_LIMIT)

    gq = row(jnp.tile(q_norm_g, N_HEADS) * (HEAD_DIM ** -0.5 * LOG2E))
    gk = row(jnp.tile(k_norm_g, N_HEADS))
    r = jnp.arange(MXU_TILE)
    head_ones = (r[:, None] // HEAD_DIM == r[None, :] // HEAD_DIM).astype(_BF16)
    n_row_tiles = t // ROW_TILE
    stage_shape = pltpu.VMEM((2, STAGE_ROWS, STAGE_COLS), _F32)
    heads_shape =jax.ShapeDtypeStruct((HEAD_PAIRS, t, LANES), _BF16)
    heads_spec = pl.BlockSpec((HEAD_PAIRS, ROW_TILE, LANES), lambda i: (0, i, 0))
    q, k, vt, yc = pl.pallas_call(
        functools.partial(_proj_kernel, tiles_per_seq=s // ROW_TILE),
        grid=(n_row_tiles,),
        in_specs=[
            pl.BlockSpec((ROW_TILE, d), lambda i: (i, 0)),
            _resident((1, d)),
            pl.BlockSpec(memory_space=pl.ANY),
            _resident((1, d)),
            _resident((1, d)),
            _resident((MXU_TILE, MXU_TILE)),
            _resident((CONV_WIDTH, d)),
            _resident((1, d)),
        ],
        out_specs=[heads_spec, heads_spec,
                   pl.BlockSpec((HEAD_PAIRS, LANES, ROW_TILE), lambda i: (0, 0, i)),
                   pl.BlockSpec((ROW_TILE, d), lambda i: (i, 0))],
        out_shape=[heads_shape, heads_shape,
                   jax.ShapeDtypeStruct((HEAD_PAIRS, LANES, t), _BF16),
                   jax.ShapeDtypeStruct((t, d), _BF16)],
        scratch_shapes=[pltpu.VMEM((ROW_TILE + CARRY_ROWS, d), _F32),
                        _packed_weight((d, 2 * d)), _packed_weight((d, d)),
                        _packed_weight((d, 3 * d)), stage_shape, pltpu.SemaphoreType.DMA((2,))],
        compiler_params=cparams(dimension_semantics=("arbitrary",)),
        name="proj",
    )(xf, row(norm1_g), w_in.astype(_F32), gq, gk, head_ones, conv_w.astype(_F32), row(conv_b))

    n_rel = rel_bias.shape[1]
    rb_rows = jnp.pad(rel_bias.astype(_F32), ((0, 0), (0, BIAS_LANES - n_rel)))
    bias_tiles = pl.pallas_call(
        _bias_kernel,
        grid=(N_HEADS,),
        in_specs=[pl.BlockSpec((1, 1, BIAS_LANES), lambda hd: (hd, 0, 0))],
        out_specs=pl.BlockSpec((1, KEY_TILES * Q_TILE, Q_TILE), lambda hd: (hd, 0, 0)),
        out_shape=jax.ShapeDtypeStruct((N_HEADS, KEY_TILES * Q_TILE, Q_TILE), _F32),
        compiler_params=cparams(dimension_semantics=("arbitrary",)),
        name="bias_tiles",
    )(rb_rows.reshape(N_HEADS, 1, BIAS_LANES))

    nq = s // Q_TILE
    ns = nq // Q_TILES_PER_STEP
    q_spec = pl.BlockSpec((HEAD_PAIRS, Q_TILES_PER_STEP * Q_TILE, LANES),
                          lambda bi, si: (0, bi * ns + si, 0))

    def key_tile(si, ref_idx):
        return jnp.maximum(si * Q_TILES_PER_STEP - (KEY_TILES - 1) + ref_idx, 0)

    def k_spec(ref_idx):
        return pl.BlockSpec((HEAD_PAIRS, Q_TILE, LANES),
                            lambda bi, si: (0, bi * nq + key_tile(si, ref_idx), 0))

    def vt_spec(ref_idx):
        return pl.BlockSpec((HEAD_PAIRS, LANES, Q_TILE),
                            lambda bi, si: (0, 0, bi * nq + key_tile(si, ref_idx)))

    attn = pl.pallas_call(
        _attn_kernel,
        grid=(b, ns),
        in_specs=[q_spec] + [k_spec(r) for r in range(KEY_REFS)]
                 + [vt_spec(r) for r in range(KEY_REFS)]
                 + [_resident((N_HEADS, KEY_TILES * Q_TILE, Q_TILE))],
        out_specs=q_spec,
        out_shape=heads_shape,
        compiler_params=cparams(dimension_semantics=("arbitrary", "arbitrary")),
        name="attn",
    )(q, *([k] * KEY_REFS), *([vt] * KEY_REFS), bias_tiles)

    mlp_weights = [w.astype(_F32) for w in (w_gate, w_attn_proj, w_conv_proj, w_out, w_up, w_down)]
    out = pl.pallas_call(
        _mix_mlp_kernel,
        grid=(n_row_tiles,),
        in_specs=[
            pl.BlockSpec((ROW_TILE, d), lambda i: (i, 0)),
            heads_spec,
            pl.BlockSpec((ROW_TILE, d), lambda i: (i, 0)),
            _resident((1, d)),
            _resident((1, 2 * d)),
            _resident((1, d)),
        ] + [pl.BlockSpec(memory_space=pl.ANY)] * len(mlp_weights),
        out_specs=pl.BlockSpec((ROW_TILE, d), lambda i: (i, 0)),
        out_shape=jax.ShapeDtypeStruct((t, d), _F32),
        scratch_shapes=[_packed_weight(w.shape) for w in mlp_weights]
                       + [stage_shape, pltpu.SemaphoreType.DMA((2,))],
        compiler_params=cparams(dimension_semantics=("arbitrary",)),
        name="mix_mlp",
    )(xf, attn, yc, row(norm1_g), row(b_gate), row(norm2_g), *mlp_weights)
    return out.reshape(b, s, d)
```

```python
import functools
import math

import jax
import jax.numpy as jnp
from jax import lax
from jax.experimental import pallas as pl
from jax.experimental.pallas import tpu as pltpu

D_MODEL = 1024
N_HEADS = 16
HEAD_DIM = 64
CHUNK = 64
N_PREV_CHUNKS = 8
MAX_REL = 256
CONV_WIDTH = 3
D_FF = 4 * D_MODEL
EPS = 1e-6
NEG_INF = -1e30
LOG2E = math.log2(math.e)

LANES = 128
BF16_ROWS = 16
HEAD_PAIRS = D_MODEL // LANES
MXU_TILE = 256

ROW_TILE = 512
PROJ_TILE = 2 * MXU_TILE
CONV_TILE = MXU_TILE
Q_TILE = 256
KEY_TILES = 1 + (N_PREV_CHUNKS * CHUNK) // Q_TILE
Q_TILES_PER_STEP = 2
KEY_REFS = KEY_TILES + Q_TILES_PER_STEP - 1
BAND_ROWS = (N_PREV_CHUNKS + LANES // CHUNK) * CHUNK
BIAS_LANES = (KEY_TILES + 1) * Q_TILE
assert 2 * MAX_REL + 1 <= BIAS_LANES and Q_TILE - 1 <= MAX_REL
SCORE_LOOKAHEAD = 3
FF_TILE = 1024
CARRY_ROWS = 8
STAGE_ROWS, STAGE_COLS = 256, D_MODEL
STAGE_SLOTS = 4
VMEM_LIMIT = 56 * 1024 * 1024

_BF16 = jnp.bfloat16
_F32 = jnp.float32
_NT_DIMS = (((1,), (1,)), ((), ()))


def _resident(shape):
    return pl.BlockSpec(shape, lambda *_: (0,) * len(shape), pipeline_mode=pl.Buffered(1))


def _rms_norm_rows(x, g):
    ms = jnp.mean(x * x, axis=-1, keepdims=True)
    return x * lax.rsqrt(ms + EPS) * g


def _packed_weight(shape):
    k, n = shape
    return pltpu.VMEM((k // 2, n), jnp.uint32)


def _pack(block):
    return pltpu.bitcast(block.astype(_BF16), jnp.uint32)


def _unpack(words):
    return pltpu.bitcast(words, _BF16)


def _stage_weights(jobs, stage, sem):
    def copy(n):
        src, r0, c0, _ = jobs[n]
        return pltpu.make_async_copy(src.at[pl.ds(r0, STAGE_ROWS), pl.ds(c0, STAGE_COLS)],
                                     stage.at[n % STAGE_SLOTS], sem.at[n % STAGE_SLOTS])

    ahead = STAGE_SLOTS - 1
    for n in range(min(ahead, len(jobs))):
        copy(n).start(priority=n % 2)
    for n, job in enumerate(jobs):
        if n + ahead < len(jobs):
            copy(n + ahead).start(priority=(n + ahead) % 2)
        copy(n).wait()
        job[3](stage[n % STAGE_SLOTS])


def _weight_jobs(src, dst, row_lo=0, col_lo=0, shape=None):
    k, n = shape if shape is not None else src.shape
    jobs = []
    for r in range(0, k, STAGE_ROWS):
        for c in range(0, n, STAGE_COLS):
            def store(block, r=r, c=c):
                dst[r // 2:(r + STAGE_ROWS) // 2, c:c + STAGE_COLS] = _pack(block)
            jobs.append((src, row_lo + r, col_lo + c, store))
    return jobs


def _proj_kernel(x_ref, g1_ref, w_in_hbm, gq_ref, gk_ref, ones_ref, cw_ref, cb_ref,
                 q_ref, k_ref, vt_ref, yc_ref,
                 u_scr, wqk_ref, wvt_ref, wc_ref, stage, sem, *, tiles_per_seq):
    tm = x_ref.shape[0]

    @pl.when(pl.program_id(0) == 0)
    def _():
        def store_vt(block, r):
            wvt_ref[:, r:r + STAGE_ROWS] = _pack(block.T)

        jobs = _weight_jobs(w_in_hbm, wqk_ref, 0, 0, (D_MODEL, 2 * D_MODEL))
        jobs += _weight_jobs(w_in_hbm, wc_ref, 0, 3 * D_MODEL, (D_MODEL, 3 * D_MODEL))
        jobs += [(w_in_hbm, r, 2 * D_MODEL, functools.partial(store_vt, r=r))
                 for r in range(0, D_MODEL, STAGE_ROWS)]
        _stage_weights(jobs, stage, sem)

    h = _rms_norm_rows(x_ref[...], g1_ref[...]).astype(_BF16)
    n_col_tiles = D_MODEL // PROJ_TILE
    slabs = PROJ_TILE // LANES

    def proj(w_ref, j, c, width=PROJ_TILE):
        lo = j * D_MODEL + c * width
        return jnp.dot(h, _unpack(w_ref[:, lo:lo + width]), preferred_element_type=_F32)

    @pl.when(pl.program_id(0) % tiles_per_seq == 0)
    def _():
        u_scr[0:CARRY_ROWS, :] = jnp.zeros((CARRY_ROWS, D_MODEL), _F32)

    for c in range(D_MODEL // CONV_TILE):
        cols = slice(c * CONV_TILE, (c + 1) * CONV_TILE)
        bg = proj(wc_ref, 0, c, CONV_TILE)
        u = proj(wc_ref, 1, c, CONV_TILE) * proj(wc_ref, 2, c, CONV_TILE)
        u_scr[CARRY_ROWS:CARRY_ROWS + tm, cols] = u
        u1 = u_scr[CARRY_ROWS - 1:CARRY_ROWS - 1 + tm, cols]
        u2 = u_scr[CARRY_ROWS - 2:CARRY_ROWS - 2 + tm, cols]
        conv = (cb_ref[:, cols] + cw_ref[0:1, cols] * u2 + cw_ref[1:2, cols] * u1
                + cw_ref[2:3, cols] * u)
        yc_ref[:, cols] = (bg * conv).astype(_BF16)
    u_scr[0:CARRY_ROWS, :] = u_scr[tm:tm + CARRY_ROWS, :]

    def finish_head_norm(p, g_ref, o_ref, c):
        p2 = (p * p).astype(_BF16)
        ss = jnp.concatenate(
            [jnp.dot(p2[:, t * MXU_TILE:(t + 1) * MXU_TILE], ones_ref[...],
                     preferred_element_type=_F32) for t in range(PROJ_TILE // MXU_TILE)], axis=-1)
        g = g_ref[:, c * PROJ_TILE:(c + 1) * PROJ_TILE]
        pn = (p * lax.rsqrt(ss * (1.0 / HEAD_DIM) + EPS) * g).astype(_BF16)
        for sl in range(slabs):
            o_ref[c * slabs + sl] = pn[:, sl * LANES:(sl + 1) * LANES]

    waiting = None
    for j, (g_ref, o_ref) in enumerate(((gq_ref, q_ref), (gk_ref, k_ref))):
        for c in range(n_col_tiles):
            p = proj(wqk_ref, j, c)
            if waiting is not None:
                finish_head_norm(*waiting)
            waiting = (p, g_ref, o_ref, c)

    for c in range(n_col_tiles):
        rows = slice(c * PROJ_TILE // 2, (c + 1) * PROJ_TILE // 2)
        vt = lax.dot_general(_unpack(wvt_ref[rows, :]), h, _NT_DIMS,
                             preferred_element_type=_F32).astype(_BF16)
        if waiting is not None:
            finish_head_norm(*waiting)
            waiting = None
        for sl in range(slabs):
            vt_ref[c * slabs + sl] = vt[sl * LANES:(sl + 1) * LANES, :]


def _bias_kernel(rb_ref, o_ref):
    _, kw, tq = o_ref.shape
    rb = rb_ref[0]
    lane = lax.broadcasted_iota(jnp.int32, rb.shape, 1)
    top = rb[:, 2 * MAX_REL:2 * MAX_REL + 1]
    fwd = jnp.where(lane > 2 * MAX_REL, top, rb) * LOG2E
    rolled = pltpu.roll(jnp.broadcast_to(fwd, (kw, BIAS_LANES)), tq, axis=1,
                        stride=1, stride_axis=0)
    kj = lax.broadcasted_iota(jnp.int32, (kw, tq), 0)
    qi = lax.broadcasted_iota(jnp.int32, (kw, tq), 1)
    dchunk = (qi // CHUNK + N_PREV_CHUNKS) - kj // CHUNK
    band = (dchunk >= 0) & (dchunk <= N_PREV_CHUNKS)
    o_ref[0] = jnp.where(band, rolled[:, :tq], NEG_INF)


def _attn_kernel(q_ref, *refs):
    k_refs, v_refs = refs[:KEY_REFS], refs[KEY_REFS:2 * KEY_REFS]
    bias_ref, o_ref = refs[2 * KEY_REFS:]
    tq = Q_TILE
    lane = lax.broadcasted_iota(jnp.int32, (tq, LANES), 1)
    ones_rows = jnp.ones((BF16_ROWS, tq), _BF16)

    def scores(sub, tiles, head):
        lo, hi = tiles[0] * tq, (tiles[-1] + 1) * tq
        hp, hh = divmod(head, 2)
        sel = (lane < HEAD_DIM) if hh == 0 else (lane >= HEAD_DIM)
        q = q_ref[hp, sub * tq:(sub + 1) * tq, :]
        qh = jnp.where(sel, q, jnp.zeros_like(q))
        keys = jnp.concatenate([k_refs[sub + t][hp] for t in tiles], axis=0)
        s = lax.dot_general(keys, qh, _NT_DIMS, preferred_element_type=_F32)
        cols = []
        for c in range(tq // LANES):
            r0 = max(lo, c * LANES)
            r1 = min(hi, c * LANES + BAND_ROWS)
            cols.append((r0, r1, s[r0 - lo:r1 - lo, c * LANES:(c + 1) * LANES]
                         + bias_ref[head, r0:r1, c * LANES:(c + 1) * LANES]))
        return cols

    def attend(sub, tiles, head, cols):
        lo, hi = tiles[0] * tq, (tiles[-1] + 1) * tq
        hp, hh = divmod(head, 2)
        p_cols = []
        for r0, r1, s in cols:
            m = jnp.max(s, axis=0, keepdims=True)
            p = jnp.exp2(s - m).astype(_BF16)
            pads = [jnp.zeros((n, LANES), _BF16) for n in (r0 - lo, hi - r1)]
            p_cols.append(jnp.concatenate(
                [blk for blk in (pads[0], p, pads[1]) if blk.shape[0]], axis=0))
        p = jnp.concatenate(p_cols, axis=1)
        v_ext = jnp.concatenate(
            [jnp.concatenate([v_refs[sub + t][hp, hh * HEAD_DIM:(hh + 1) * HEAD_DIM, :],
                              ones_rows], axis=0) for t in tiles], axis=1)
        acc = jnp.dot(v_ext, p, preferred_element_type=_F32)
        return acc[:HEAD_DIM] * (1.0 / acc[HEAD_DIM:HEAD_DIM + 1])

    def run(tiles_per_sub):
        work = [(sub, tiles, head) for sub, tiles in enumerate(tiles_per_sub)
                for head in range(N_HEADS)]
        pending = [scores(*w) for w in work[:SCORE_LOOKAHEAD]]
        halves = []
        for n, (sub, tiles, head) in enumerate(work):
            if n + SCORE_LOOKAHEAD < len(work):
                pending.append(scores(*work[n + SCORE_LOOKAHEAD]))
            halves.append(attend(sub, tiles, head, pending.pop(0)))
            if head % 2:
                o_ref[head // 2, sub * tq:(sub + 1) * tq, :] = (
                    jnp.concatenate(halves, axis=0).T.astype(_BF16))
                halves = []

    all_tiles = tuple(range(KEY_TILES))
    first_step = tuple(all_tiles[max(KEY_TILES - 1 - j, 0):] for j in range(Q_TILES_PER_STEP))
    qs = pl.program_id(1)
    pl.when(qs == 0)(functools.partial(run, first_step))
    pl.when(qs > 0)(functools.partial(run, (all_tiles,) * Q_TILES_PER_STEP))


def _mix_mlp_kernel(x_ref, a_ref, yc_ref, g1_ref, bgate_ref, g2_ref,
                    wg_hbm, wap_hbm, wcp_hbm, wout_hbm, wup_hbm, wdown_hbm, o_ref,
                    wg_ref, wap_ref, wcp_ref, wout_ref, wup_ref, wdown_ref, stage, sem):
    @pl.when(pl.program_id(0) == 0)
    def _():
        pairs = ((wg_hbm, wg_ref), (wap_hbm, wap_ref), (wcp_hbm, wcp_ref), (wout_hbm, wout_ref),
                 (wup_hbm, wup_ref), (wdown_hbm, wdown_ref))
        _stage_weights([job for src, dst in pairs for job in _weight_jobs(src, dst)], stage, sem)

    def dot(lhs, w_ref, rows=slice(None), cols=slice(None)):
        return jnp.dot(lhs, _unpack(w_ref[rows, cols]), preferred_element_type=_F32)

    x = x_ref[...]
    h = _rms_norm_rows(x, g1_ref[...]).astype(_BF16)
    gates = jax.nn.sigmoid(dot(h, wg_ref) + bgate_ref[...])
    a = jnp.concatenate([a_ref[hp] for hp in range(HEAD_PAIRS)], axis=-1)
    ya = dot(a, wap_ref)
    yc = dot(yc_ref[...], wcp_ref)
    merged = (gates[:, :D_MODEL] * ya + gates[:, D_MODEL:] * yc).astype(_BF16)
    x1 = x + dot(merged, wout_ref)
    h2 = _rms_norm_rows(x1, g2_ref[...]).astype(_BF16)
    acc = x1
    for f in range(D_FF // FF_TILE):
        up = dot(h2, wup_ref, cols=slice(f * FF_TILE, (f + 1) * FF_TILE))
        act = jnp.square(jnp.maximum(up, 0.0)).astype(_BF16)
        acc = acc + dot(act, wdown_ref, rows=slice(f * FF_TILE // 2, (f + 1) * FF_TILE // 2))
    o_ref[...] = acc


def kernel(x, norm1_g, w_in, q_norm_g, k_norm_g, rel_bias, conv_w, conv_b, w_attn_proj,
           w_conv_proj, w_gate, b_gate, w_out, norm2_g, w_up, w_down):
    b, s, d = x.shape
    assert d == D_MODEL and s % ROW_TILE == 0 and s % (Q_TILE * Q_TILES_PER_STEP) == 0
    t = b * s
    xf = x.reshape(t, d)
    row = lambda v: v.reshape(1, -1).astype(_F32)
    cparams = functools.partial(pltpu.CompilerParams, vmem_limit_bytes=VMEM_LIMIT)

    gq = row(jnp.tile(q_norm_g, N_HEADS) * (HEAD_DIM ** -0.5 * LOG2E))
    gk = row(jnp.tile(k_norm_g, N_HEADS))
    r = jnp.arange(MXU_TILE)
    head_ones = (r[:, None] // HEAD_DIM == r[None, :] // HEAD_DIM).astype(_BF16)
    n_row_tiles = t // ROW_TILE
    staging = [pltpu.VMEM((STAGE_SLOTS, STAGE_ROWS, STAGE_COLS), _F32),
               pltpu.SemaphoreType.DMA((STAGE_SLOTS,))]
    heads_shape = jax.ShapeDtypeStruct((HEAD_PAIRS, t, LANES), _BF16)
    heads_spec = pl.BlockSpec((HEAD_PAIRS, ROW_TILE, LANES), lambda i: (0, i, 0))
    q, k, vt, yc = pl.pallas_call(
        functools.partial(_proj_kernel, tiles_per_seq=s // ROW_TILE),
        grid=(n_row_tiles,),
        in_specs=[
            pl.BlockSpec((ROW_TILE, d), lambda i: (i, 0)),
            _resident((1, d)),
            pl.BlockSpec(memory_space=pl.ANY),
            _resident((1, d)),
            _resident((1, d)),
            _resident((MXU_TILE, MXU_TILE)),
            _resident((CONV_WIDTH, d)),
            _resident((1, d)),
        ],
        out_specs=[heads_spec, heads_spec,
                   pl.BlockSpec((HEAD_PAIRS, LANES, ROW_TILE), lambda i: (0, 0, i)),
                   pl.BlockSpec((ROW_TILE, d), lambda i: (i, 0))],
        out_shape=[heads_shape, heads_shape,
                   jax.ShapeDtypeStruct((HEAD_PAIRS, LANES, t), _BF16),
                   jax.ShapeDtypeStruct((t, d), _BF16)],
        scratch_shapes=[pltpu.VMEM((ROW_TILE + CARRY_ROWS, d), _F32),
                        _packed_weight((d, 2 * d)), _packed_weight((d, d)),
                        _packed_weight((d, 3 * d))] + staging,
        compiler_params=cparams(dimension_semantics=("arbitrary",)),
        name="proj",
    )(xf, row(norm1_g), w_in.astype(_F32), gq, gk, head_ones, conv_w.astype(_F32), row(conv_b))

    n_rel = rel_bias.shape[1]
    rb_rows = jnp.pad(rel_bias.astype(_F32), ((0, 0), (0, BIAS_LANES - n_rel)))
    bias_tiles = pl.pallas_call(
        _bias_kernel,
        grid=(N_HEADS,),
        in_specs=[pl.BlockSpec((1, 1, BIAS_LANES), lambda hd: (hd, 0, 0))],
        out_specs=pl.BlockSpec((1, KEY_TILES * Q_TILE, Q_TILE), lambda hd: (hd, 0, 0)),
        out_shape=jax.ShapeDtypeStruct((N_HEADS, KEY_TILES * Q_TILE, Q_TILE), _F32),
        compiler_params=cparams(dimension_semantics=("arbitrary",)),
        name="bias_tiles",
    )(rb_rows.reshape(N_HEADS, 1, BIAS_LANES))

    nq = s // Q_TILE
    ns = nq // Q_TILES_PER_STEP
    q_spec = pl.BlockSpec((HEAD_PAIRS, Q_TILES_PER_STEP * Q_TILE, LANES),
                          lambda bi, si: (0, bi * ns + si, 0))

    def key_tile(si, ref_idx):
        return jnp.maximum(si * Q_TILES_PER_STEP - (KEY_TILES - 1) + ref_idx, 0)

    def k_spec(ref_idx):
        return pl.BlockSpec((HEAD_PAIRS, Q_TILE, LANES),
                            lambda bi, si: (0, bi * nq + key_tile(si, ref_idx), 0))

    def vt_spec(ref_idx):
        return pl.BlockSpec((HEAD_PAIRS, LANES, Q_TILE),
                            lambda bi, si: (0, 0, bi * nq + key_tile(si, ref_idx)))

    attn = pl.pallas_call(
        _attn_kernel,
        grid=(b, ns),
        in_specs=[q_spec] + [k_spec(r) for r in range(KEY_REFS)]
                 + [vt_spec(r) for r in range(KEY_REFS)]
                 + [_resident((N_HEADS, KEY_TILES * Q_TILE, Q_TILE))],
        out_specs=q_spec,
        out_shape=heads_shape,
        compiler_params=cparams(dimension_semantics=("arbitrary", "arbitrary")),
        name="attn",
    )(q, *([k] * KEY_REFS), *([vt] * KEY_REFS), bias_tiles)

    mlp_weights = [w.astype(_F32) for w in (w_gate, w_attn_proj, w_conv_proj, w_out, w_up, w_down)]
    out = pl.pallas_call(
        _mix_mlp_kernel,
        grid=(n_row_tiles,),
        in_specs=[
            pl.BlockSpec((ROW_TILE, d), lambda i: (i, 0)),
            heads_spec,
            pl.BlockSpec((ROW_TILE, d), lambda i: (i, 0)),
            _resident((1, d)),
            _resident((1, 2 * d)),
            _resident((1, d)),
        ] + [pl.BlockSpec(memory_space=pl.ANY)] * len(mlp_weights),
        out_specs=pl.BlockSpec((ROW_TILE, d), lambda i: (i, 0)),
        out_shape=jax.ShapeDtypeStruct((t, d), _F32),
        scratch_shapes=[_packed_weight(w.shape) for w in mlp_weights] + staging,
        compiler_params=cparams(dimension_semantics=("arbitrary",)),
        name="mix_mlp",
    )(xf, attn, yc, row(norm1_g), row(b_gate), row(norm2_g), *mlp_weights)
    return out.reshape(b, s, d)
```

```python
import functools
import math

import jax
import jax.numpy as jnp
from jax import lax
from jax.experimental import pallas as pl
from jax.experimental.pallas import tpu as pltpu

D_MODEL = 1024
N_HEADS = 16
HEAD_DIM = 64
CHUNK = 64
N_PREV_CHUNKS = 8
MAX_REL = 256
CONV_WIDTH = 3
D_FF = 4 * D_MODEL
EPS = 1e-6
NEG_INF = -1e30
LOG2E = math.log2(math.e)

LANES = 128
BF16_ROWS = 16
HEAD_PAIRS = D_MODEL // LANES
MXU_TILE = 256

PROJ_ROWS = 1024
ROW_TILE = 512
PROJ_TILE = 2 * MXU_TILE
CONV_TILE = MXU_TILE
Q_TILE = 256
KEY_TILES = 1 + (N_PREV_CHUNKS * CHUNK) // Q_TILE
Q_TILES_PER_STEP = 2
KEY_REFS = KEY_TILES + Q_TILES_PER_STEP - 1
BAND_ROWS = (N_PREV_CHUNKS + LANES // CHUNK) * CHUNK
BIAS_LANES = (KEY_TILES + 1) * Q_TILE
assert 2 * MAX_REL + 1 <= BIAS_LANES and Q_TILE - 1 <= MAX_REL
SCORE_LOOKAHEAD = 3
FF_TILE = 1024
CARRY_ROWS = 8
STAGE_ROWS, STAGE_COLS = 256, D_MODEL
STAGE_SLOTS = 4
VMEM_LIMIT = 56 * 1024 * 1024

_BF16 = jnp.bfloat16
_F32 = jnp.float32
_NT_DIMS = (((1,), (1,)), ((), ()))


def _resident(shape):
    return pl.BlockSpec(shape, lambda *_: (0,) * len(shape), pipeline_mode=pl.Buffered(1))


def _rms_norm_rows(x, g):
    ms = jnp.mean(x * x, axis=-1, keepdims=True)
    return x * lax.rsqrt(ms + EPS) * g


def _packed_weight(shape):
    k, n = shape
    return pltpu.VMEM((k // 2, n), jnp.uint32)


def _pack(block):
    return pltpu.bitcast(block.astype(_BF16), jnp.uint32)


def _unpack(words):
    return pltpu.bitcast(words, _BF16)


def _stage_weights(jobs, stage, sem):
    def copy(n):
        src, r0, c0, _ = jobs[n]
        return pltpu.make_async_copy(src.at[pl.ds(r0, STAGE_ROWS), pl.ds(c0, STAGE_COLS)],
                                     stage.at[n % STAGE_SLOTS], sem.at[n % STAGE_SLOTS])

    ahead = STAGE_SLOTS - 1
    for n in range(min(ahead, len(jobs))):
        copy(n).start(priority=n % 2)
    for n, job in enumerate(jobs):
        if n + ahead < len(jobs):
            copy(n + ahead).start(priority=(n + ahead) % 2)
        copy(n).wait()
        job[3](stage[n % STAGE_SLOTS])


def _weight_jobs(src, dst, row_lo=0, col_lo=0, shape=None):
    k, n = shape if shape is not None else src.shape
    jobs = []
    for r in range(0, k, STAGE_ROWS):
        for c in range(0, n, STAGE_COLS):
            def store(block, r=r, c=c):
                dst[r // 2:(r + STAGE_ROWS) // 2, c:c + STAGE_COLS] = _pack(block)
            jobs.append((src, row_lo + r, col_lo + c, store))
    return jobs


def _proj_kernel(x_ref, g1_ref, w_in_hbm, gq_ref, gk_ref, ones_ref, cw_ref, cb_ref,
                 q_ref, k_ref, vt_ref, yc_ref,
                 u_scr, wqk_ref, wvt_ref, wc_ref, stage, sem, *, tiles_per_seq):
    tm = x_ref.shape[0]

    @pl.when(pl.program_id(0) == 0)
    def _():
        def store_vt(block, r):
            wvt_ref[:, r:r + STAGE_ROWS] = _pack(block.T)

        jobs = _weight_jobs(w_in_hbm, wqk_ref, 0, 0, (D_MODEL, 2 * D_MODEL))
        jobs += _weight_jobs(w_in_hbm, wc_ref, 0, 3 * D_MODEL, (D_MODEL, 3 * D_MODEL))
        jobs += [(w_in_hbm, r, 2 * D_MODEL, functools.partial(store_vt, r=r))
                 for r in range(0, D_MODEL, STAGE_ROWS)]
        _stage_weights(jobs, stage, sem)

    h = _rms_norm_rows(x_ref[...], g1_ref[...]).astype(_BF16)
    n_col_tiles = D_MODEL // PROJ_TILE
    slabs = PROJ_TILE // LANES

    def proj(w_ref, j, c, width=PROJ_TILE):
        lo = j * D_MODEL + c * width
        return jnp.dot(h, _unpack(w_ref[:, lo:lo + width]), preferred_element_type=_F32)

    @pl.when(pl.program_id(0) % tiles_per_seq == 0)
    def _():
        u_scr[0:CARRY_ROWS, :] = jnp.zeros((CARRY_ROWS, D_MODEL), _F32)

    for c in range(D_MODEL // CONV_TILE):
        cols = slice(c * CONV_TILE, (c + 1) * CONV_TILE)
        bg = proj(wc_ref, 0, c, CONV_TILE)
        u = proj(wc_ref, 1, c, CONV_TILE) * proj(wc_ref, 2, c, CONV_TILE)
        u_scr[CARRY_ROWS:CARRY_ROWS + tm, cols] = u
        u1 = u_scr[CARRY_ROWS - 1:CARRY_ROWS - 1 + tm, cols]
        u2 = u_scr[CARRY_ROWS - 2:CARRY_ROWS - 2 + tm, cols]
        conv = (cb_ref[:, cols] + cw_ref[0:1, cols] * u2 + cw_ref[1:2, cols] * u1
                + cw_ref[2:3, cols] * u)
        yc_ref[:, cols] = (bg * conv).astype(_BF16)
    u_scr[0:CARRY_ROWS, :] = u_scr[tm:tm + CARRY_ROWS, :]

    def finish_head_norm(p, g_ref, o_ref, c):
        p2 = (p * p).astype(_BF16)
        ss = jnp.concatenate(
            [jnp.dot(p2[:, t * MXU_TILE:(t + 1) * MXU_TILE], ones_ref[...],
                     preferred_element_type=_F32) for t in range(PROJ_TILE // MXU_TILE)], axis=-1)
        g = g_ref[:, c * PROJ_TILE:(c + 1) * PROJ_TILE]
        pn = (p * lax.rsqrt(ss * (1.0 / HEAD_DIM) + EPS) * g).astype(_BF16)
        for sl in range(slabs):
            o_ref[c * slabs + sl] = pn[:, sl * LANES:(sl + 1) * LANES]

    waiting = None
    for j, (g_ref, o_ref) in enumerate(((gq_ref, q_ref), (gk_ref, k_ref))):
        for c in range(n_col_tiles):
            p = proj(wqk_ref, j, c)
            if waiting is not None:
                finish_head_norm(*waiting)
            waiting = (p, g_ref, o_ref, c)

    for c in range(n_col_tiles):
        rows = slice(c * PROJ_TILE // 2, (c + 1) * PROJ_TILE // 2)
        vt = lax.dot_general(_unpack(wvt_ref[rows, :]), h, _NT_DIMS,
                             preferred_element_type=_F32).astype(_BF16)
        if waiting is not None:
            finish_head_norm(*waiting)
            waiting = None
        for sl in range(slabs):
            vt_ref[c * slabs + sl] = vt[sl * LANES:(sl + 1) * LANES, :]


def _bias_kernel(rb_ref, o_ref):
    _, kw, tq = o_ref.shape
    rb = rb_ref[0]
    lane = lax.broadcasted_iota(jnp.int32, rb.shape, 1)
    top = rb[:, 2 * MAX_REL:2 * MAX_REL + 1]
    fwd = jnp.where(lane > 2 * MAX_REL, top, rb) * LOG2E
    rolled = pltpu.roll(jnp.broadcast_to(fwd, (kw, BIAS_LANES)), tq, axis=1,
                        stride=1, stride_axis=0)
    kj = lax.broadcasted_iota(jnp.int32, (kw, tq), 0)
    qi = lax.broadcasted_iota(jnp.int32, (kw, tq), 1)
    dchunk = (qi // CHUNK + N_PREV_CHUNKS) - kj // CHUNK
    band = (dchunk >= 0) & (dchunk <= N_PREV_CHUNKS)
    o_ref[0] = jnp.where(band, rolled[:, :tq], NEG_INF)


def _attn_kernel(q_ref, *refs):
    k_refs, v_refs = refs[:KEY_REFS], refs[KEY_REFS:2 * KEY_REFS]
    bias_ref, o_ref = refs[2 * KEY_REFS:]
    tq = Q_TILE
    lane = lax.broadcasted_iota(jnp.int32, (tq, LANES), 1)
    ones_rows = jnp.ones((BF16_ROWS, tq), _BF16)

    def scores(sub, tiles, head):
        lo, hi = tiles[0] * tq, (tiles[-1] + 1) * tq
        hp, hh = divmod(head, 2)
        sel = (lane < HEAD_DIM) if hh == 0 else (lane >= HEAD_DIM)
        q = q_ref[hp, sub * tq:(sub + 1) * tq, :]
        qh = jnp.where(sel, q, jnp.zeros_like(q))
        keys = jnp.concatenate([k_refs[sub + t][hp] for t in tiles], axis=0)
        s = lax.dot_general(keys, qh, _NT_DIMS, preferred_element_type=_F32)
        cols = []
        for c in range(tq // LANES):
            r0 = max(lo, c * LANES)
            r1 = min(hi, c * LANES + BAND_ROWS)
            cols.append((r0, r1, s[r0 - lo:r1 - lo, c * LANES:(c + 1) * LANES]
                         + bias_ref[head, r0:r1, c * LANES:(c + 1) * LANES]))
        return cols

    def attend(sub, tiles, head, cols):
        lo, hi = tiles[0] * tq, (tiles[-1] + 1) * tq
        hp, hh = divmod(head, 2)
        p_cols = []
        for r0, r1, s in cols:
            m = jnp.max(s, axis=0, keepdims=True)
            p = jnp.exp2(s - m).astype(_BF16)
            pads = [jnp.zeros((n, LANES), _BF16) for n in (r0 - lo, hi - r1)]
            p_cols.append(jnp.concatenate(
                [blk for blk in (pads[0], p, pads[1]) if blk.shape[0]], axis=0))
        p = jnp.concatenate(p_cols, axis=1)
        v_ext = jnp.concatenate(
            [jnp.concatenate([v_refs[sub + t][hp, hh * HEAD_DIM:(hh + 1) * HEAD_DIM, :],
                              ones_rows], axis=0) for t in tiles], axis=1)
        acc = jnp.dot(v_ext, p, preferred_element_type=_F32)
        return acc[:HEAD_DIM] * (1.0 / acc[HEAD_DIM:HEAD_DIM + 1])

    def run(tiles_per_sub):
        work = [(sub, tiles, head) for sub, tiles in enumerate(tiles_per_sub)
                for head in range(N_HEADS)]
        pending = [scores(*w) for w in work[:SCORE_LOOKAHEAD]]
        halves = []
        for n, (sub, tiles, head) in enumerate(work):
            if n + SCORE_LOOKAHEAD < len(work):
                pending.append(scores(*work[n + SCORE_LOOKAHEAD]))
            halves.append(attend(sub, tiles, head, pending.pop(0)))
            if head % 2:
                o_ref[head // 2, sub * tq:(sub + 1) * tq, :] = (
                    jnp.concatenate(halves, axis=0).T.astype(_BF16))
                halves = []

    all_tiles = tuple(range(KEY_TILES))
    first_step = tuple(all_tiles[max(KEY_TILES - 1 - j, 0):] for j in range(Q_TILES_PER_STEP))
    qs = pl.program_id(1)
    pl.when(qs == 0)(functools.partial(run, first_step))
    pl.when(qs > 0)(functools.partial(run, (all_tiles,) * Q_TILES_PER_STEP))


def _mix_mlp_kernel(x_ref, a_ref, yc_ref, g1_ref, bgate_ref, g2_ref,
                    wg_hbm, wap_hbm, wcp_hbm, wout_hbm, wup_hbm, wdown_hbm, o_ref,
                    wg_ref, wap_ref, wcp_ref, wout_ref, wup_ref, wdown_ref, stage, sem):
    @pl.when(pl.program_id(0) == 0)
    def _():
        pairs = ((wg_hbm, wg_ref), (wap_hbm, wap_ref), (wcp_hbm, wcp_ref), (wout_hbm, wout_ref),
                 (wup_hbm, wup_ref), (wdown_hbm, wdown_ref))
        _stage_weights([job for src, dst in pairs for job in _weight_jobs(src, dst)], stage, sem)

    def dot(lhs, w_ref, rows=slice(None), cols=slice(None)):
        return jnp.dot(lhs, _unpack(w_ref[rows, cols]), preferred_element_type=_F32)

    x = x_ref[...]
    h = _rms_norm_rows(x, g1_ref[...]).astype(_BF16)
    gates = jax.nn.sigmoid(dot(h, wg_ref) + bgate_ref[...])
    a = jnp.concatenate([a_ref[hp] for hp in range(HEAD_PAIRS)], axis=-1)
    ya = dot(a, wap_ref)
    yc = dot(yc_ref[...], wcp_ref)
    merged = (gates[:, :D_MODEL] * ya + gates[:, D_MODEL:] * yc).astype(_BF16)
    x1 = x + dot(merged, wout_ref)
    h2 = _rms_norm_rows(x1, g2_ref[...]).astype(_BF16)
    acc = x1
    for f in range(D_FF // FF_TILE):
        up = dot(h2, wup_ref, cols=slice(f * FF_TILE, (f + 1) * FF_TILE))
        act = jnp.square(jnp.maximum(up, 0.0)).astype(_BF16)
        acc = acc + dot(act, wdown_ref, rows=slice(f * FF_TILE // 2, (f + 1) * FF_TILE // 2))
    o_ref[...] = acc


def kernel(x, norm1_g, w_in, q_norm_g, k_norm_g, rel_bias, conv_w, conv_b, w_attn_proj,
           w_conv_proj, w_gate, b_gate, w_out, norm2_g, w_up, w_down):
    b, s, d = x.shape
    assert d == D_MODEL and s % ROW_TILE == 0 and s % PROJ_ROWS == 0
    assert s % (Q_TILE * Q_TILES_PER_STEP) == 0
    t = b * s
    xf = x.reshape(t, d)
    row = lambda v: v.reshape(1, -1).astype(_F32)
    cparams = functools.partial(pltpu.CompilerParams, vmem_limit_bytes=VMEM_LIMIT)

    gq = row(jnp.tile(q_norm_g, N_HEADS) * (HEAD_DIM ** -0.5 * LOG2E))
    gk = row(jnp.tile(k_norm_g, N_HEADS))
    r = jnp.arange(MXU_TILE)
    head_ones = (r[:, None] // HEAD_DIM == r[None, :] // HEAD_DIM).astype(_BF16)
    staging = [pltpu.VMEM((STAGE_SLOTS, STAGE_ROWS, STAGE_COLS), _F32),
               pltpu.SemaphoreType.DMA((STAGE_SLOTS,))]
    heads_shape = jax.ShapeDtypeStruct((HEAD_PAIRS, t, LANES), _BF16)

    def heads_spec(rows):
        return pl.BlockSpec((HEAD_PAIRS, rows, LANES), lambda i: (0, i, 0))

    q, k, vt, yc = pl.pallas_call(
        functools.partial(_proj_kernel, tiles_per_seq=s // PROJ_ROWS),
        grid=(t // PROJ_ROWS,),
        in_specs=[
            pl.BlockSpec((PROJ_ROWS, d), lambda i: (i, 0)),
            _resident((1, d)),
            pl.BlockSpec(memory_space=pl.ANY),
            _resident((1, d)),
            _resident((1, d)),
            _resident((MXU_TILE, MXU_TILE)),
            _resident((CONV_WIDTH, d)),
            _resident((1, d)),
        ],
        out_specs=[heads_spec(PROJ_ROWS), heads_spec(PROJ_ROWS),
                   pl.BlockSpec((HEAD_PAIRS, LANES, PROJ_ROWS), lambda i: (0, 0, i)),
                   pl.BlockSpec((PROJ_ROWS, d), lambda i: (i, 0))],
        out_shape=[heads_shape, heads_shape,
                   jax.ShapeDtypeStruct((HEAD_PAIRS, LANES, t), _BF16),
                   jax.ShapeDtypeStruct((t, d), _BF16)],
        scratch_shapes=[pltpu.VMEM((PROJ_ROWS + CARRY_ROWS, d), _F32),
                        _packed_weight((d, 2 * d)), _packed_weight((d, d)),
                        _packed_weight((d, 3 * d))] + staging,
        compiler_params=cparams(dimension_semantics=("arbitrary",)),
        name="proj",
    )(xf, row(norm1_g), w_in.astype(_F32), gq, gk, head_ones, conv_w.astype(_F32), row(conv_b))

    n_rel = rel_bias.shape[1]
    rb_rows = jnp.pad(rel_bias.astype(_F32), ((0, 0), (0, BIAS_LANES - n_rel)))
    bias_tiles = pl.pallas_call(
        _bias_kernel,
        grid=(N_HEADS,),
        in_specs=[pl.BlockSpec((1, 1, BIAS_LANES), lambda hd: (hd, 0, 0))],
        out_specs=pl.BlockSpec((1, KEY_TILES * Q_TILE, Q_TILE), lambda hd: (hd, 0, 0)),
        out_shape=jax.ShapeDtypeStruct((N_HEADS, KEY_TILES * Q_TILE, Q_TILE), _F32),
        compiler_params=cparams(dimension_semantics=("arbitrary",)),
        name="bias_tiles",
    )(rb_rows.reshape(N_HEADS, 1, BIAS_LANES))

    nq = s // Q_TILE
    ns = nq // Q_TILES_PER_STEP
    q_spec = pl.BlockSpec((HEAD_PAIRS, Q_TILES_PER_STEP * Q_TILE, LANES),
                          lambda bi, si: (0, bi * ns + si, 0))

    def key_tile(si, ref_idx):
        return jnp.maximum(si * Q_TILES_PER_STEP - (KEY_TILES - 1) + ref_idx, 0)

    def k_spec(ref_idx):
        return pl.BlockSpec((HEAD_PAIRS, Q_TILE, LANES),
                            lambda bi, si: (0, bi * nq + key_tile(si, ref_idx), 0))

    def vt_spec(ref_idx):
        return pl.BlockSpec((HEAD_PAIRS, LANES, Q_TILE),
                            lambda bi, si: (0, 0, bi * nq + key_tile(si, ref_idx)))

    attn = pl.pallas_call(
        _attn_kernel,
        grid=(b, ns),
        in_specs=[q_spec] + [k_spec(r) for r in range(KEY_REFS)]
                 + [vt_spec(r) for r in range(KEY_REFS)]
                 + [_resident((N_HEADS, KEY_TILES * Q_TILE, Q_TILE))],
        out_specs=q_spec,
        out_shape=heads_shape,
        compiler_params=cparams(dimension_semantics=("arbitrary", "arbitrary")),
        name="attn",
    )(q, *([k] * KEY_REFS), *([vt] * KEY_REFS), bias_tiles)

    mlp_weights = [w.astype(_F32) for w in (w_gate, w_attn_proj, w_conv_proj, w_out, w_up, w_down)]
    out = pl.pallas_call(
        _mix_mlp_kernel,
        grid=(t // ROW_TILE,),
        in_specs=[
            pl.BlockSpec((ROW_TILE, d), lambda i: (i, 0)),
            heads_spec(ROW_TILE),
            pl.BlockSpec((ROW_TILE, d), lambda i: (i, 0)),
            _resident((1, d)),
            _resident((1, 2 * d)),
            _resident((1, d)),
        ] + [pl.BlockSpec(memory_space=pl.ANY)] * len(mlp_weights),
        out_specs=pl.BlockSpec((ROW_TILE, d), lambda i: (i, 0)),
        out_shape=jax.ShapeDtypeStruct((t, d), _F32),
        scratch_shapes=[_packed_weight(w.shape) for w in mlp_weights] + staging,
        compiler_params=cparams(dimension_semantics=("arbitrary",)),
        name="mix_mlp",
    )(xf, attn, yc, row(norm1_g), row(b_gate), row(norm2_g), *mlp_weights)
    return out.reshape(b, s, d)
```

```python
import functools
import math

import jax
import jax.numpy as jnp
from jax import lax
from jax.experimental import pallas as pl
from jax.experimental.pallas import tpu as pltpu

D_MODEL = 1024
N_HEADS = 16
HEAD_DIM = 64
CHUNK = 64
N_PREV_CHUNKS = 8
MAX_REL = 256
CONV_WIDTH = 3
D_FF = 4 * D_MODEL
EPS = 1e-6
NEG_INF = -1e30
LOG2E = math.log2(math.e)

LANES = 128
BF16_ROWS = 16
HEAD_PAIRS = D_MODEL // LANES
MXU_TILE = 256

PROJ_ROWS = 512
ROW_TILE = 512
PROJ_TILE = 2 * MXU_TILE
CONV_TILE = MXU_TILE
Q_TILE = 256
KEY_TILES = 1 + (N_PREV_CHUNKS * CHUNK) // Q_TILE
Q_TILES_PER_STEP = 2
KEY_REFS = KEY_TILES + Q_TILES_PER_STEP - 1
BAND_ROWS = (N_PREV_CHUNKS + LANES // CHUNK) * CHUNK
BIAS_LANES = (KEY_TILES + 1) * Q_TILE
assert 2 * MAX_REL + 1 <= BIAS_LANES and Q_TILE - 1 <= MAX_REL
SCORE_LOOKAHEAD = 3
FF_TILE = 1024
CARRY_ROWS = 8
STAGE_ROWS, STAGE_COLS = 256, D_MODEL
STAGE_SLOTS = 4
VMEM_LIMIT = 56 * 1024 * 1024

_BF16 = jnp.bfloat16
_F32 = jnp.float32
_NT_DIMS = (((1,), (1,)), ((), ()))


def _resident(shape):
    return pl.BlockSpec(shape, lambda *_: (0,) * len(shape), pipeline_mode=pl.Buffered(1))


def _rms_norm_rows(x, g):
    ms = jnp.mean(x * x, axis=-1, keepdims=True)
    return x * lax.rsqrt(ms + EPS) * g


def _packed_weight(shape):
    k, n = shape
    return pltpu.VMEM((k // 2, n), jnp.uint32)


def _pack(block):
    return pltpu.bitcast(block.astype(_BF16), jnp.uint32)


def _unpack(words):
    return pltpu.bitcast(words, _BF16)


def _stage_weights(jobs, stage, sem):
    def copy(n):
        src, r0, c0, _ = jobs[n]
        return pltpu.make_async_copy(src.at[pl.ds(r0, STAGE_ROWS), pl.ds(c0, STAGE_COLS)],
                                     stage.at[n % STAGE_SLOTS], sem.at[n % STAGE_SLOTS])

    ahead = STAGE_SLOTS - 1
    for n in range(min(ahead, len(jobs))):
        copy(n).start(priority=n % 2)
    for n, job in enumerate(jobs):
        if n + ahead < len(jobs):
            copy(n + ahead).start(priority=(n + ahead) % 2)
        copy(n).wait()
        job[3](stage[n % STAGE_SLOTS])


def _weight_jobs(src, dst, row_lo=0, col_lo=0, shape=None):
    k, n = shape if shape is not None else src.shape
    jobs = []
    for r in range(0, k, STAGE_ROWS):
        for c in range(0, n, STAGE_COLS):
            def store(block, r=r, c=c):
                dst[r // 2:(r + STAGE_ROWS) // 2, c:c + STAGE_COLS] = _pack(block)
            jobs.append((src, row_lo + r, col_lo + c, store))
    return jobs


def _proj_kernel(x_ref, g1_ref, w_in_hbm, gq_ref, gk_ref, ones_ref, cw_ref, cb_ref,
                 q_ref, k_ref, vt_ref, yc_ref,
                 u_scr, wqk_ref, wvt_ref, wc_ref, stage, sem, *, tiles_per_seq):
    tm = x_ref.shape[0]

    @pl.when(pl.program_id(0) == 0)
    def _():
        def store_vt(block, r):
            wvt_ref[:, r:r + STAGE_ROWS] = _pack(block.T)

        jobs = _weight_jobs(w_in_hbm, wqk_ref, 0, 0, (D_MODEL, 2 * D_MODEL))
        jobs += _weight_jobs(w_in_hbm, wc_ref, 0, 3 * D_MODEL, (D_MODEL, 3 * D_MODEL))
        jobs += [(w_in_hbm, r, 2 * D_MODEL, functools.partial(store_vt, r=r))
                 for r in range(0, D_MODEL, STAGE_ROWS)]
        _stage_weights(jobs, stage, sem)

    h = _rms_norm_rows(x_ref[...], g1_ref[...]).astype(_BF16)
    n_col_tiles = D_MODEL // PROJ_TILE
    slabs = PROJ_TILE // LANES

    def proj(w_ref, j, c, width=PROJ_TILE):
        lo = j * D_MODEL + c * width
        return jnp.dot(h, _unpack(w_ref[:, lo:lo + width]), preferred_element_type=_F32)

    @pl.when(pl.program_id(0) % tiles_per_seq == 0)
    def _():
        u_scr[0:CARRY_ROWS, :] = jnp.zeros((CARRY_ROWS, D_MODEL), _F32)

    for c in range(D_MODEL // CONV_TILE):
        cols = slice(c * CONV_TILE, (c + 1) * CONV_TILE)
        bg = proj(wc_ref, 0, c, CONV_TILE)
        u = proj(wc_ref, 1, c, CONV_TILE) * proj(wc_ref, 2, c, CONV_TILE)
        u_scr[CARRY_ROWS:CARRY_ROWS + tm, cols] = u
        u1 = u_scr[CARRY_ROWS - 1:CARRY_ROWS - 1 + tm, cols]
        u2 = u_scr[CARRY_ROWS - 2:CARRY_ROWS - 2 + tm, cols]
        conv = (cb_ref[:, cols] + cw_ref[0:1, cols] * u2 + cw_ref[1:2, cols] * u1
                + cw_ref[2:3, cols] * u)
        yc_ref[:, cols] = (bg * conv).astype(_BF16)
    u_scr[0:CARRY_ROWS, :] = u_scr[tm:tm + CARRY_ROWS, :]

    def finish_head_norm(p, g_ref, o_ref, c):
        p2 = (p * p).astype(_BF16)
        ss = jnp.concatenate(
            [jnp.dot(p2[:, t * MXU_TILE:(t + 1) * MXU_TILE], ones_ref[...],
                     preferred_element_type=_F32) for t in range(PROJ_TILE // MXU_TILE)], axis=-1)
        g = g_ref[:, c * PROJ_TILE:(c + 1) * PROJ_TILE]
        pn = (p * lax.rsqrt(ss * (1.0 / HEAD_DIM) + EPS) * g).astype(_BF16)
        for sl in range(slabs):
            o_ref[c * slabs + sl] = pn[:, sl * LANES:(sl + 1) * LANES]

    waiting = None
    for j, (g_ref, o_ref) in enumerate(((gq_ref, q_ref), (gk_ref, k_ref))):
        for c in range(n_col_tiles):
            p = proj(wqk_ref, j, c)
            if waiting is not None:
                finish_head_norm(*waiting)
            waiting = (p, g_ref, o_ref, c)

    for c in range(n_col_tiles):
        rows = slice(c * PROJ_TILE // 2, (c + 1) * PROJ_TILE // 2)
        vt = lax.dot_general(_unpack(wvt_ref[rows, :]), h, _NT_DIMS,
                             preferred_element_type=_F32).astype(_BF16)
        if waiting is not None:
            finish_head_norm(*waiting)
            waiting = None
        for sl in range(slabs):
            vt_ref[c * slabs + sl] = vt[sl * LANES:(sl + 1) * LANES, :]


def _bias_tile(rb):
    kw, tq = KEY_TILES * Q_TILE, Q_TILE
    lane = lax.broadcasted_iota(jnp.int32, rb.shape, 1)
    top = rb[:, 2 * MAX_REL:2 * MAX_REL + 1]
    fwd = jnp.where(lane > 2 * MAX_REL, top, rb) * LOG2E
    rolled = pltpu.roll(jnp.broadcast_to(fwd, (kw, BIAS_LANES)), tq, axis=1,
                        stride=1, stride_axis=0)
    kj = lax.broadcasted_iota(jnp.int32, (kw, tq), 0)
    qi = lax.broadcasted_iota(jnp.int32, (kw, tq), 1)
    dchunk = (qi // CHUNK + N_PREV_CHUNKS) - kj // CHUNK
    band = (dchunk >= 0) & (dchunk <= N_PREV_CHUNKS)
    return jnp.where(band, rolled[:, :tq], NEG_INF)


def _attn_kernel(q_ref, *refs):
    k_refs, v_refs = refs[:KEY_REFS], refs[KEY_REFS:2 * KEY_REFS]
    rb_ref, o_ref, bias_ref = refs[2 * KEY_REFS:]
    tq = Q_TILE
    lane = lax.broadcasted_iota(jnp.int32, (tq, LANES), 1)
    ones_rows = jnp.ones((BF16_ROWS, tq), _BF16)

    @pl.when((pl.program_id(0) == 0) & (pl.program_id(1) == 0))
    def _():
        def build(head, carry):
            bias_ref[head] = _bias_tile(rb_ref[head])
            return carry
        lax.fori_loop(0, N_HEADS, build, 0)

    def scores(sub, tiles, head):
        lo, hi = tiles[0] * tq, (tiles[-1] + 1) * tq
        hp, hh = divmod(head, 2)
        sel = (lane < HEAD_DIM) if hh == 0 else (lane >= HEAD_DIM)
        q = q_ref[hp, sub * tq:(sub + 1) * tq, :]
        qh = jnp.where(sel, q, jnp.zeros_like(q))
        keys = jnp.concatenate([k_refs[sub + t][hp] for t in tiles], axis=0)
        s = lax.dot_general(keys, qh, _NT_DIMS, preferred_element_type=_F32)
        cols = []
        for c in range(tq // LANES):
            r0 = max(lo, c * LANES)
            r1 = min(hi, c * LANES + BAND_ROWS)
            cols.append((r0, r1, s[r0 - lo:r1 - lo, c * LANES:(c + 1) * LANES]
                         + bias_ref[head, r0:r1, c * LANES:(c + 1) * LANES]))
        return cols

    def attend(sub, tiles, head, cols):
        lo, hi = tiles[0] * tq, (tiles[-1] + 1) * tq
        hp, hh = divmod(head, 2)
        p_cols = []
        for r0, r1, s in cols:
            m = jnp.max(s, axis=0, keepdims=True)
            p = jnp.exp2(s - m).astype(_BF16)
            pads = [jnp.zeros((n, LANES), _BF16) for n in (r0 - lo, hi - r1)]
            p_cols.append(jnp.concatenate(
                [blk for blk in (pads[0], p, pads[1]) if blk.shape[0]], axis=0))
        p = jnp.concatenate(p_cols, axis=1)
        v_ext = jnp.concatenate(
            [jnp.concatenate([v_refs[sub + t][hp, hh * HEAD_DIM:(hh + 1) * HEAD_DIM, :],
                              ones_rows], axis=0) for t in tiles], axis=1)
        acc = jnp.dot(v_ext, p, preferred_element_type=_F32)
        return acc[:HEAD_DIM] * (1.0 / acc[HEAD_DIM:HEAD_DIM + 1])

    def run(tiles_per_sub):
        work = [(sub, tiles, head) for sub, tiles in enumerate(tiles_per_sub)
                for head in range(N_HEADS)]
        pending = [scores(*w) for w in work[:SCORE_LOOKAHEAD]]
        halves = []
        for n, (sub, tiles, head) in enumerate(work):
            if n + SCORE_LOOKAHEAD < len(work):
                pending.append(scores(*work[n + SCORE_LOOKAHEAD]))
            halves.append(attend(sub, tiles, head, pending.pop(0)))
            if head % 2:
                o_ref[head // 2, sub * tq:(sub + 1) * tq, :] = (
                    jnp.concatenate(halves, axis=0).T.astype(_BF16))
                halves = []

    all_tiles = tuple(range(KEY_TILES))
    first_step = tuple(all_tiles[max(KEY_TILES - 1 - j, 0):] for j in range(Q_TILES_PER_STEP))
    qs = pl.program_id(1)
    pl.when(qs == 0)(functools.partial(run, first_step))
    pl.when(qs > 0)(functools.partial(run, (all_tiles,) * Q_TILES_PER_STEP))


def _mix_mlp_kernel(x_ref, a_ref, yc_ref, g1_ref, bgate_ref, g2_ref,
                    wg_hbm, wap_hbm, wcp_hbm, wout_hbm, wup_hbm, wdown_hbm, o_ref,
                    wg_ref, wap_ref, wcp_ref, wout_ref, wup_ref, wdown_ref, stage, sem):
    @pl.when(pl.program_id(0) == 0)
    def _():
        pairs = ((wg_hbm, wg_ref), (wap_hbm, wap_ref), (wcp_hbm, wcp_ref), (wout_hbm, wout_ref),
                 (wup_hbm, wup_ref), (wdown_hbm, wdown_ref))
        _stage_weights([job for src, dst in pairs for job in _weight_jobs(src, dst)], stage, sem)

    def dot(lhs, w_ref, rows=slice(None), cols=slice(None)):
        return jnp.dot(lhs, _unpack(w_ref[rows, cols]), preferred_element_type=_F32)

    x = x_ref[...]
    h = _rms_norm_rows(x, g1_ref[...]).astype(_BF16)
    gates = jax.nn.sigmoid(dot(h, wg_ref) + bgate_ref[...])
    a = jnp.concatenate([a_ref[hp] for hp in range(HEAD_PAIRS)], axis=-1)
    ya = dot(a, wap_ref)
    yc = dot(yc_ref[...], wcp_ref)
    merged = (gates[:, :D_MODEL] * ya + gates[:, D_MODEL:] * yc).astype(_BF16)
    x1 = x + dot(merged, wout_ref)
    h2 = _rms_norm_rows(x1, g2_ref[...]).astype(_BF16)
    acc = x1
    for f in range(D_FF // FF_TILE):
        up = dot(h2, wup_ref, cols=slice(f * FF_TILE, (f + 1) * FF_TILE))
        act = jnp.square(jnp.maximum(up, 0.0)).astype(_BF16)
        acc = acc + dot(act, wdown_ref, rows=slice(f * FF_TILE // 2, (f + 1) * FF_TILE // 2))
    o_ref[...] = acc


def kernel(x, norm1_g, w_in, q_norm_g, k_norm_g, rel_bias, conv_w, conv_b, w_attn_proj,
           w_conv_proj, w_gate, b_gate, w_out, norm2_g, w_up, w_down):
    b, s, d = x.shape
    assert d == D_MODEL and s % ROW_TILE == 0 and s % PROJ_ROWS == 0
    assert s % (Q_TILE * Q_TILES_PER_STEP) == 0
    t = b * s
    xf = x.reshape(t, d)
    row = lambda v: v.reshape(1, -1).astype(_F32)
    cparams = functools.partial(pltpu.CompilerParams, vmem_limit_bytes=VMEM_LIMIT)

    gq = row(jnp.tile(q_norm_g, N_HEADS) * (HEAD_DIM ** -0.5 * LOG2E))
    gk = row(jnp.tile(k_norm_g, N_HEADS))
    r = jnp.arange(MXU_TILE)
    head_ones = (r[:, None] // HEAD_DIM == r[None, :] // HEAD_DIM).astype(_BF16)
    staging = [pltpu.VMEM((STAGE_SLOTS, STAGE_ROWS, STAGE_COLS), _F32),
               pltpu.SemaphoreType.DMA((STAGE_SLOTS,))]
    heads_shape = jax.ShapeDtypeStruct((HEAD_PAIRS, t, LANES), _BF16)

    def heads_spec(rows):
        return pl.BlockSpec((HEAD_PAIRS, rows, LANES), lambda i: (0, i, 0))

    q, k, vt, yc = pl.pallas_call(
        functools.partial(_proj_kernel, tiles_per_seq=s // PROJ_ROWS),
        grid=(t // PROJ_ROWS,),
        in_specs=[
            pl.BlockSpec((PROJ_ROWS, d), lambda i: (i, 0)),
            _resident((1, d)),
            pl.BlockSpec(memory_space=pl.ANY),
            _resident((1, d)),
            _resident((1, d)),
            _resident((MXU_TILE, MXU_TILE)),
            _resident((CONV_WIDTH, d)),
            _resident((1, d)),
        ],
        out_specs=[heads_spec(PROJ_ROWS), heads_spec(PROJ_ROWS),
                   pl.BlockSpec((HEAD_PAIRS, LANES, PROJ_ROWS), lambda i: (0, 0, i)),
                   pl.BlockSpec((PROJ_ROWS, d), lambda i: (i, 0))],
        out_shape=[heads_shape, heads_shape,
                   jax.ShapeDtypeStruct((HEAD_PAIRS, LANES, t), _BF16),
                   jax.ShapeDtypeStruct((t, d), _BF16)],
        scratch_shapes=[pltpu.VMEM((PROJ_ROWS + CARRY_ROWS, d), _F32),
                        _packed_weight((d, 2 * d)), _packed_weight((d, d)),
                        _packed_weight((d, 3 * d))] + staging,
        compiler_params=cparams(dimension_semantics=("arbitrary",)),
        name="proj",
    )(xf, row(norm1_g), w_in.astype(_F32), gq, gk, head_ones, conv_w.astype(_F32), row(conv_b))

    n_rel = rel_bias.shape[1]
    rb_rows = jnp.pad(rel_bias.astype(_F32), ((0, 0), (0, BIAS_LANES - n_rel)))
    nq = s // Q_TILE
    ns = nq // Q_TILES_PER_STEP
    q_spec = pl.BlockSpec((HEAD_PAIRS, Q_TILES_PER_STEP * Q_TILE, LANES),
                          lambda bi, si: (0, bi * ns + si, 0))

    def key_tile(si, ref_idx):
        return jnp.maximum(si * Q_TILES_PER_STEP - (KEY_TILES - 1) + ref_idx, 0)

    def k_spec(ref_idx):
        return pl.BlockSpec((HEAD_PAIRS, Q_TILE, LANES),
                            lambda bi, si: (0, bi * nq + key_tile(si, ref_idx), 0))

    def vt_spec(ref_idx):
        return pl.BlockSpec((HEAD_PAIRS, LANES, Q_TILE),
                            lambda bi, si: (0, 0, bi * nq + key_tile(si, ref_idx)))

    attn = pl.pallas_call(
        _attn_kernel,
        grid=(b, ns),
        in_specs=[q_spec] + [k_spec(r) for r in range(KEY_REFS)]
                 + [vt_spec(r) for r in range(KEY_REFS)]
                 + [_resident((N_HEADS, 1, BIAS_LANES))],
        out_specs=q_spec,
        out_shape=heads_shape,
        scratch_shapes=[pltpu.VMEM((N_HEADS, KEY_TILES * Q_TILE, Q_TILE), _F32)],
        compiler_params=cparams(dimension_semantics=("arbitrary", "arbitrary")),
        name="attn",
    )(q, *([k] * KEY_REFS), *([vt] * KEY_REFS), rb_rows.reshape(N_HEADS, 1, BIAS_LANES))

    mlp_weights = [w.astype(_F32) for w in (w_gate, w_attn_proj, w_conv_proj, w_out, w_up, w_down)]
    out = pl.pallas_call(
        _mix_mlp_kernel,
        grid=(t // ROW_TILE,),
        in_specs=[
            pl.BlockSpec((ROW_TILE, d), lambda i: (i, 0)),
            heads_spec(ROW_TILE),
            pl.BlockSpec((ROW_TILE, d), lambda i: (i, 0)),
            _resident((1, d)),
            _resident((1, 2 * d)),
            _resident((1, d)),
        ] + [pl.BlockSpec(memory_space=pl.ANY)] * len(mlp_weights),
        out_specs=pl.BlockSpec((ROW_TILE, d), lambda i: (i, 0)),
        out_shape=jax.ShapeDtypeStruct((t, d), _F32),
        scratch_shapes=[_packed_weight(w.shape) for w in mlp_weights] + staging,
        compiler_params=cparams(dimension_semantics=("arbitrary",)),
        name="mix_mlp",
    )(xf, attn, yc, row(norm1_g), row(b_gate), row(norm2_g), *mlp_weights)
    return out.reshape(b, s, d)
```

```python
import functools
import math

import jax
import jax.numpy as jnp
from jax import lax
from jax.experimental import pallas as pl
from jax.experimental.pallas import tpu as pltpu

D_MODEL = 1024
N_HEADS = 16
HEAD_DIM = 64
CHUNK = 64
N_PREV_CHUNKS = 8
MAX_REL = 256
CONV_WIDTH = 3
D_FF = 4 * D_MODEL
EPS = 1e-6
NEG_INF = -1e30
LOG2E = math.log2(math.e)

LANES = 128
BF16_ROWS = 16
HEAD_PAIRS = D_MODEL // LANES
MXU_TILE = 256

PROJ_ROWS = 512
ROW_TILE = 512
PROJ_TILE = 2 * MXU_TILE
CONV_TILE = MXU_TILE
Q_TILE = 256
KEY_TILES = 1 + (N_PREV_CHUNKS * CHUNK) // Q_TILE
Q_TILES_PER_STEP = 2
KEY_REFS = KEY_TILES + Q_TILES_PER_STEP - 1
BAND_ROWS = (N_PREV_CHUNKS + LANES // CHUNK) * CHUNK
BIAS_LANES = (KEY_TILES + 1) * Q_TILE
assert 2 * MAX_REL + 1 <= BIAS_LANES and Q_TILE - 1 <= MAX_REL
SCORE_LOOKAHEAD = 3
VALUE_DELAY = 0
FF_TILE = 1024
MLP_SUB_TILES = 2
CARRY_ROWS = 8
STAGE_ROWS, STAGE_COLS = 256, D_MODEL
STAGE_SLOTS = 4
VMEM_LIMIT = 56 * 1024 * 1024

_BF16 = jnp.bfloat16
_F32 = jnp.float32
_NT_DIMS = (((1,), (1,)), ((), ()))


def _resident(shape):
    return pl.BlockSpec(shape, lambda *_: (0,) * len(shape), pipeline_mode=pl.Buffered(1))


def _rms_norm_rows(x, g):
    ms = jnp.mean(x * x, axis=-1, keepdims=True)
    return x * lax.rsqrt(ms + EPS) * g


def _packed_weight(shape):
    k, n = shape
    return pltpu.VMEM((k // 2, n), jnp.uint32)


def _pack(block):
    return pltpu.bitcast(block.astype(_BF16), jnp.uint32)


def _unpack(words):
    return pltpu.bitcast(words, _BF16)


def _stage_weights(jobs, stage, sem):
    def copy(n):
        src, r0, c0, _ = jobs[n]
        return pltpu.make_async_copy(src.at[pl.ds(r0, STAGE_ROWS), pl.ds(c0, STAGE_COLS)],
                                     stage.at[n % STAGE_SLOTS], sem.at[n % STAGE_SLOTS])

    ahead = STAGE_SLOTS - 1
    for n in range(min(ahead, len(jobs))):
        copy(n).start(priority=n % 2)
    for n, job in enumerate(jobs):
        if n + ahead < len(jobs):
            copy(n + ahead).start(priority=(n + ahead) % 2)
        copy(n).wait()
        job[3](stage[n % STAGE_SLOTS])


def _weight_jobs(src, dst, row_lo=0, col_lo=0, shape=None):
    k, n = shape if shape is not None else src.shape
    jobs = []
    for r in range(0, k, STAGE_ROWS):
        for c in range(0, n, STAGE_COLS):
            def store(block, r=r, c=c):
                dst[r // 2:(r + STAGE_ROWS) // 2, c:c + STAGE_COLS] = _pack(block)
            jobs.append((src, row_lo + r, col_lo + c, store))
    return jobs


def _proj_kernel(x_ref, g1_ref, w_in_hbm, gq_ref, gk_ref, ones_ref, cw_ref, cb_ref,
                 q_ref, k_ref, vt_ref, yc_ref,
                 u_scr, wqk_ref, wvt_ref, wc_ref, stage, sem, *, tiles_per_seq):
    tm = x_ref.shape[0]

    @pl.when(pl.program_id(0) == 0)
    def _():
        def store_vt(block, r):
            wvt_ref[:, r:r + STAGE_ROWS] = _pack(block.T)

        jobs = _weight_jobs(w_in_hbm, wqk_ref, 0, 0, (D_MODEL, 2 * D_MODEL))
        jobs += _weight_jobs(w_in_hbm, wc_ref, 0, 3 * D_MODEL, (D_MODEL, 3 * D_MODEL))
        jobs += [(w_in_hbm, r, 2 * D_MODEL, functools.partial(store_vt, r=r))
                 for r in range(0, D_MODEL, STAGE_ROWS)]
        _stage_weights(jobs, stage, sem)

    h = _rms_norm_rows(x_ref[...], g1_ref[...]).astype(_BF16)
    n_col_tiles = D_MODEL // PROJ_TILE
    slabs = PROJ_TILE // LANES

    def proj(w_ref, j, c, width=PROJ_TILE):
        lo = j * D_MODEL + c * width
        return jnp.dot(h, _unpack(w_ref[:, lo:lo + width]), preferred_element_type=_F32)

    @pl.when(pl.program_id(0) % tiles_per_seq == 0)
    def _():
        u_scr[0:CARRY_ROWS, :] = jnp.zeros((CARRY_ROWS, D_MODEL), _F32)

    for c in range(D_MODEL // CONV_TILE):
        cols = slice(c * CONV_TILE, (c + 1) * CONV_TILE)
        bg = proj(wc_ref, 0, c, CONV_TILE)
        u = proj(wc_ref, 1, c, CONV_TILE) * proj(wc_ref, 2, c, CONV_TILE)
        u_scr[CARRY_ROWS:CARRY_ROWS + tm, cols] = u
        u1 = u_scr[CARRY_ROWS - 1:CARRY_ROWS - 1 + tm, cols]
        u2 = u_scr[CARRY_ROWS - 2:CARRY_ROWS - 2 + tm, cols]
        conv = (cb_ref[:, cols] + cw_ref[0:1, cols] * u2 + cw_ref[1:2, cols] * u1
                + cw_ref[2:3, cols] * u)
        yc_ref[:, cols] = (bg * conv).astype(_BF16)
    u_scr[0:CARRY_ROWS, :] = u_scr[tm:tm + CARRY_ROWS, :]

    def finish_head_norm(p, g_ref, o_ref, c):
        p2 = (p * p).astype(_BF16)
        ss = jnp.concatenate(
            [jnp.dot(p2[:, t * MXU_TILE:(t + 1) * MXU_TILE], ones_ref[...],
                     preferred_element_type=_F32) for t in range(PROJ_TILE // MXU_TILE)], axis=-1)
        g = g_ref[:, c * PROJ_TILE:(c + 1) * PROJ_TILE]
        pn = (p * lax.rsqrt(ss * (1.0 / HEAD_DIM) + EPS) * g).astype(_BF16)
        for sl in range(slabs):
            o_ref[c * slabs + sl] = pn[:, sl * LANES:(sl + 1) * LANES]

    waiting = None
    for j, (g_ref, o_ref) in enumerate(((gq_ref, q_ref), (gk_ref, k_ref))):
        for c in range(n_col_tiles):
            p = proj(wqk_ref, j, c)
            if waiting is not None:
                finish_head_norm(*waiting)
            waiting = (p, g_ref, o_ref, c)

    for c in range(n_col_tiles):
        rows = slice(c * PROJ_TILE // 2, (c + 1) * PROJ_TILE // 2)
        vt = lax.dot_general(_unpack(wvt_ref[rows, :]), h, _NT_DIMS,
                             preferred_element_type=_F32).astype(_BF16)
        if waiting is not None:
            finish_head_norm(*waiting)
            waiting = None
        for sl in range(slabs):
            vt_ref[c * slabs + sl] = vt[sl * LANES:(sl + 1) * LANES, :]


def _bias_tile(rb):
    kw, tq = KEY_TILES * Q_TILE, Q_TILE
    lane = lax.broadcasted_iota(jnp.int32, rb.shape, 1)
    top = rb[:, 2 * MAX_REL:2 * MAX_REL + 1]
    fwd = jnp.where(lane > 2 * MAX_REL, top, rb) * LOG2E
    rolled = pltpu.roll(jnp.broadcast_to(fwd, (kw, BIAS_LANES)), tq, axis=1,
                        stride=1, stride_axis=0)
    kj = lax.broadcasted_iota(jnp.int32, (kw, tq), 0)
    qi = lax.broadcasted_iota(jnp.int32, (kw, tq), 1)
    dchunk = (qi // CHUNK + N_PREV_CHUNKS) - kj // CHUNK
    band = (dchunk >= 0) & (dchunk <= N_PREV_CHUNKS)
    return jnp.where(band, rolled[:, :tq], NEG_INF)


def _attn_kernel(q_ref, *refs):
    k_refs, v_refs = refs[:KEY_REFS], refs[KEY_REFS:2 * KEY_REFS]
    rb_ref, o_ref, bias_ref = refs[2 * KEY_REFS:]
    tq = Q_TILE
    lane = lax.broadcasted_iota(jnp.int32, (tq, LANES), 1)
    ones_rows = jnp.ones((BF16_ROWS, tq), _BF16)

    @pl.when((pl.program_id(0) == 0) & (pl.program_id(1) == 0))
    def _():
        def build(head, carry):
            bias_ref[head] = _bias_tile(rb_ref[head])
            return carry
        lax.fori_loop(0, N_HEADS, build, 0)

    def scores(sub, tiles, head):
        lo, hi = tiles[0] * tq, (tiles[-1] + 1) * tq
        hp, hh = divmod(head, 2)
        sel = (lane < HEAD_DIM) if hh == 0 else (lane >= HEAD_DIM)
        q = q_ref[hp, sub * tq:(sub + 1) * tq, :]
        qh = jnp.where(sel, q, jnp.zeros_like(q))
        keys = jnp.concatenate([k_refs[sub + t][hp] for t in tiles], axis=0)
        s = lax.dot_general(keys, qh, _NT_DIMS, preferred_element_type=_F32)
        cols = []
        for c in range(tq // LANES):
            r0 = max(lo, c * LANES)
            r1 = min(hi, c * LANES + BAND_ROWS)
            cols.append((r0, r1, s[r0 - lo:r1 - lo, c * LANES:(c + 1) * LANES]
                         + bias_ref[head, r0:r1, c * LANES:(c + 1) * LANES]))
        return cols

    def softmax_numerators(tiles, cols):
        lo, hi = tiles[0] * tq, (tiles[-1] + 1) * tq
        p_cols = []
        for r0, r1, s in cols:
            m = jnp.max(s, axis=0, keepdims=True)
            p = jnp.exp2(s - m).astype(_BF16)
            pads = [jnp.zeros((n, LANES), _BF16) for n in (r0 - lo, hi - r1)]
            p_cols.append(jnp.concatenate(
                [blk for blk in (pads[0], p, pads[1]) if blk.shape[0]], axis=0))
        return jnp.concatenate(p_cols, axis=1)

    def weighted_values(sub, tiles, head, p):
        hp, hh = divmod(head, 2)
        v_ext = jnp.concatenate(
            [jnp.concatenate([v_refs[sub + t][hp, hh * HEAD_DIM:(hh + 1) * HEAD_DIM, :],
                              ones_rows], axis=0) for t in tiles], axis=1)
        acc = jnp.dot(v_ext, p, preferred_element_type=_F32)
        return acc[:HEAD_DIM] * (1.0 / acc[HEAD_DIM:HEAD_DIM + 1])

    def run(tiles_per_sub):
        work = [(sub, tiles, head) for sub, tiles in enumerate(tiles_per_sub)
                for head in range(N_HEADS)]
        scored = [scores(*w) for w in work[:SCORE_LOOKAHEAD]]
        numerators, halves = [], []
        for n in range(len(work) + VALUE_DELAY):
            if n + SCORE_LOOKAHEAD < len(work):
                scored.append(scores(*work[n + SCORE_LOOKAHEAD]))
            if n < len(work):
                numerators.append(softmax_numerators(work[n][1], scored.pop(0)))
            if n >= VALUE_DELAY:
                sub, tiles, head = work[n - VALUE_DELAY]
                halves.append(weighted_values(sub, tiles, head, numerators.pop(0)))
                if head % 2:
                    o_ref[head // 2, sub * tq:(sub + 1) * tq, :] = (
                        jnp.concatenate(halves, axis=0).T.astype(_BF16))
                    halves = []

    all_tiles = tuple(range(KEY_TILES))
    first_step = tuple(all_tiles[max(KEY_TILES - 1 - j, 0):] for j in range(Q_TILES_PER_STEP))
    qs = pl.program_id(1)
    pl.when(qs == 0)(functools.partial(run, first_step))
    pl.when(qs > 0)(functools.partial(run, (all_tiles,) * Q_TILES_PER_STEP))


def _mix_mlp_kernel(x_ref, a_ref, yc_ref, g1_ref, bgate_ref, g2_ref,
                    wg_hbm, wap_hbm, wcp_hbm, wout_hbm, wup_hbm, wdown_hbm, o_ref,
                    wg_ref, wap_ref, wcp_ref, wout_ref, wup_ref, wdown_ref, stage, sem):
    @pl.when(pl.program_id(0) == 0)
    def _():
        pairs = ((wg_hbm, wg_ref), (wap_hbm, wap_ref), (wcp_hbm, wcp_ref), (wout_hbm, wout_ref),
                 (wup_hbm, wup_ref), (wdown_hbm, wdown_ref))
        _stage_weights([job for src, dst in pairs for job in _weight_jobs(src, dst)], stage, sem)

    def dot(lhs, w_ref, rows=slice(None), cols=slice(None)):
        return jnp.dot(lhs, _unpack(w_ref[rows, cols]), preferred_element_type=_F32)

    sub_rows = x_ref.shape[0] // MLP_SUB_TILES
    subs = [slice(n * sub_rows, (n + 1) * sub_rows) for n in range(MLP_SUB_TILES)]

    def branches(rows):
        x = x_ref[rows, :]
        h = _rms_norm_rows(x, g1_ref[...]).astype(_BF16)
        gates = jax.nn.sigmoid(dot(h, wg_ref) + bgate_ref[...])
        a = jnp.concatenate([a_ref[hp, rows, :] for hp in range(HEAD_PAIRS)], axis=-1)
        ya = dot(a, wap_ref)
        yc = dot(yc_ref[rows, :], wcp_ref)
        return x, (gates[:, :D_MODEL] * ya + gates[:, D_MODEL:] * yc).astype(_BF16)

    def residual(x, merged):
        return x + dot(merged, wout_ref)

    def mlp(rows, x1):
        h2 = _rms_norm_rows(x1, g2_ref[...]).astype(_BF16)
        acc = x1
        for f in range(D_FF // FF_TILE):
            up = dot(h2, wup_ref, cols=slice(f * FF_TILE, (f + 1) * FF_TILE))
            act = jnp.square(jnp.maximum(up, 0.0)).astype(_BF16)
            acc = acc + dot(act, wdown_ref, rows=slice(f * FF_TILE // 2, (f + 1) * FF_TILE // 2))
        o_ref[rows, :] = acc

    merged = [branches(rows) for rows in subs]
    x1 = [residual(*m) for m in merged]
    for rows, v in zip(subs, x1):
        mlp(rows, v)


def kernel(x, norm1_g, w_in, q_norm_g, k_norm_g, rel_bias, conv_w, conv_b, w_attn_proj,
           w_conv_proj, w_gate, b_gate, w_out, norm2_g, w_up, w_down):
    b, s, d = x.shape
    assert d == D_MODEL and s % ROW_TILE == 0 and s % PROJ_ROWS == 0
    assert s % (Q_TILE * Q_TILES_PER_STEP) == 0
    t = b * s
    xf = x.reshape(t, d)
    row = lambda v: v.reshape(1, -1).astype(_F32)
    cparams = functools.partial(pltpu.CompilerParams, vmem_limit_bytes=VMEM_LIMIT)

    gq = row(jnp.tile(q_norm_g, N_HEADS) * (HEAD_DIM ** -0.5 * LOG2E))
    gk = row(jnp.tile(k_norm_g, N_HEADS))
    r = jnp.arange(MXU_TILE)
    head_ones = (r[:, None] // HEAD_DIM == r[None, :] // HEAD_DIM).astype(_BF16)
    staging = [pltpu.VMEM((STAGE_SLOTS, STAGE_ROWS, STAGE_COLS), _F32),
               pltpu.SemaphoreType.DMA((STAGE_SLOTS,))]
    heads_shape = jax.ShapeDtypeStruct((HEAD_PAIRS, t, LANES), _BF16)

    def heads_spec(rows):
        return pl.BlockSpec((HEAD_PAIRS, rows, LANES), lambda i: (0, i, 0))

    q, k, vt, yc = pl.pallas_call(
        functools.partial(_proj_kernel, tiles_per_seq=s // PROJ_ROWS),
        grid=(t // PROJ_ROWS,),
        in_specs=[
            pl.BlockSpec((PROJ_ROWS, d), lambda i: (i, 0)),
            _resident((1, d)),
            pl.BlockSpec(memory_space=pl.ANY),
            _resident((1, d)),
            _resident((1, d)),
            _resident((MXU_TILE, MXU_TILE)),
            _resident((CONV_WIDTH, d)),
            _resident((1, d)),
        ],
        out_specs=[heads_spec(PROJ_ROWS), heads_spec(PROJ_ROWS),
                   pl.BlockSpec((HEAD_PAIRS, LANES, PROJ_ROWS), lambda i: (0, 0, i)),
                   pl.BlockSpec((PROJ_ROWS, d), lambda i: (i, 0))],
        out_shape=[heads_shape, heads_shape,
                   jax.ShapeDtypeStruct((HEAD_PAIRS, LANES, t), _BF16),
                   jax.ShapeDtypeStruct((t, d), _BF16)],
        scratch_shapes=[pltpu.VMEM((PROJ_ROWS + CARRY_ROWS, d), _F32),
                        _packed_weight((d, 2 * d)), _packed_weight((d, d)),
                        _packed_weight((d, 3 * d))] + staging,
        compiler_params=cparams(dimension_semantics=("arbitrary",)),
        name="proj",
    )(xf, row(norm1_g), w_in.astype(_F32), gq, gk, head_ones, conv_w.astype(_F32), row(conv_b))

    n_rel = rel_bias.shape[1]
    rb_rows = jnp.pad(rel_bias.astype(_F32), ((0, 0), (0, BIAS_LANES - n_rel)))
    nq = s // Q_TILE
    ns = nq // Q_TILES_PER_STEP
    q_spec = pl.BlockSpec((HEAD_PAIRS, Q_TILES_PER_STEP * Q_TILE, LANES),
                          lambda bi, si: (0, bi * ns + si, 0))

    def key_tile(si, ref_idx):
        return jnp.maximum(si * Q_TILES_PER_STEP - (KEY_TILES - 1) + ref_idx, 0)

    def k_spec(ref_idx):
        return pl.BlockSpec((HEAD_PAIRS, Q_TILE, LANES),
                            lambda bi, si: (0, bi * nq + key_tile(si, ref_idx), 0))

    def vt_spec(ref_idx):
        return pl.BlockSpec((HEAD_PAIRS, LANES, Q_TILE),
                            lambda bi, si: (0, 0, bi * nq + key_tile(si, ref_idx)))

    attn = pl.pallas_call(
        _attn_kernel,
        grid=(b, ns),
        in_specs=[q_spec] + [k_spec(r) for r in range(KEY_REFS)]
                 + [vt_spec(r) for r in range(KEY_REFS)]
                 + [_resident((N_HEADS, 1, BIAS_LANES))],
        out_specs=q_spec,
        out_shape=heads_shape,
        scratch_shapes=[pltpu.VMEM((N_HEADS, KEY_TILES * Q_TILE, Q_TILE), _F32)],
        compiler_params=cparams(dimension_semantics=("arbitrary", "arbitrary")),
        name="attn",
    )(q, *([k] * KEY_REFS), *([vt] * KEY_REFS), rb_rows.reshape(N_HEADS, 1, BIAS_LANES))

    mlp_weights = [w.astype(_F32) for w in (w_gate, w_attn_proj, w_conv_proj, w_out, w_up, w_down)]
    out = pl.pallas_call(
        _mix_mlp_kernel,
        grid=(t // ROW_TILE,),
        in_specs=[
            pl.BlockSpec((ROW_TILE, d), lambda i: (i, 0)),
            heads_spec(ROW_TILE),
            pl.BlockSpec((ROW_TILE, d), lambda i: (i, 0)),
            _resident((1, d)),
            _resident((1, 2 * d)),
            _resident((1, d)),
        ] + [pl.BlockSpec(memory_space=pl.ANY)] * len(mlp_weights),
        out_specs=pl.BlockSpec((ROW_TILE, d), lambda i: (i, 0)),
        out_shape=jax.ShapeDtypeStruct((t, d), _F32),
        scratch_shapes=[_packed_weight(w.shape) for w in mlp_weights] + staging,
        compiler_params=cparams(dimension_semantics=("arbitrary",)),
        name="mix_mlp",
    )(xf, attn, yc, row(norm1_g), row(b_gate), row(norm2_g), *mlp_weights)
    return out.reshape(b, s, d)
```

```python
import functools
import math

import jax
import jax.numpy as jnp
from jax import lax
from jax.experimental import pallas as pl
from jax.experimental.pallas import tpu as pltpu

D_MODEL = 1024
N_HEADS = 16
HEAD_DIM = 64
CHUNK = 64
N_PREV_CHUNKS = 8
MAX_REL = 256
CONV_WIDTH = 3
D_FF = 4 * D_MODEL
EPS = 1e-6
NEG_INF = -1e30
LOG2E = math.log2(math.e)

LANES = 128
BF16_ROWS = 16
HEAD_PAIRS = D_MODEL // LANES
MXU_TILE = 256

PROJ_ROWS = 512
ROW_TILE = 512
PROJ_TILE = 2 * MXU_TILE
CONV_TILE = MXU_TILE
Q_TILE = 256
KEY_TILES = 1 + (N_PREV_CHUNKS * CHUNK) // Q_TILE
Q_TILES_PER_STEP = 2
KEY_REFS = KEY_TILES + Q_TILES_PER_STEP - 1
BAND_ROWS = (N_PREV_CHUNKS + LANES // CHUNK) * CHUNK
BIAS_LANES = (KEY_TILES + 1) * Q_TILE
assert 2 * MAX_REL + 1 <= BIAS_LANES and Q_TILE - 1 <= MAX_REL
SCORE_LOOKAHEAD = 3
VALUE_DELAY = 0
FF_TILE = 1024
PROJ_SUB_TILES = 2
MLP_SUB_TILES = 2
CARRY_ROWS = 8
STAGE_ROWS, STAGE_COLS = 256, D_MODEL
STAGE_SLOTS = 4
VMEM_LIMIT = 56 * 1024 * 1024

_BF16 = jnp.bfloat16
_F32 = jnp.float32
_NT_DIMS = (((1,), (1,)), ((), ()))


def _resident(shape):
    return pl.BlockSpec(shape, lambda *_: (0,) * len(shape), pipeline_mode=pl.Buffered(1))


def _rms_norm_rows(x, g):
    ms = jnp.mean(x * x, axis=-1, keepdims=True)
    return x * lax.rsqrt(ms + EPS) * g


def _packed_weight(shape):
    k, n = shape
    return pltpu.VMEM((k // 2, n), jnp.uint32)


def _pack(block):
    return pltpu.bitcast(block.astype(_BF16), jnp.uint32)


def _unpack(words):
    return pltpu.bitcast(words, _BF16)


def _stage_weights(jobs, stage, sem):
    def copy(n):
        src, r0, c0, _ = jobs[n]
        return pltpu.make_async_copy(src.at[pl.ds(r0, STAGE_ROWS), pl.ds(c0, STAGE_COLS)],
                                     stage.at[n % STAGE_SLOTS], sem.at[n % STAGE_SLOTS])

    ahead = STAGE_SLOTS - 1
    for n in range(min(ahead, len(jobs))):
        copy(n).start(priority=n % 2)
    for n, job in enumerate(jobs):
        if n + ahead < len(jobs):
            copy(n + ahead).start(priority=(n + ahead) % 2)
        copy(n).wait()
        job[3](stage[n % STAGE_SLOTS])


def _weight_jobs(src, dst, row_lo=0, col_lo=0, shape=None):
    k, n = shape if shape is not None else src.shape
    jobs = []
    for r in range(0, k, STAGE_ROWS):
        for c in range(0, n, STAGE_COLS):
            def store(block, r=r, c=c):
                dst[r // 2:(r + STAGE_ROWS) // 2, c:c + STAGE_COLS] = _pack(block)
            jobs.append((src, row_lo + r, col_lo + c, store))
    return jobs


def _proj_kernel(x_ref, g1_ref, w_in_hbm, gq_ref, gk_ref, ones_ref, cw_ref, cb_ref,
                 q_ref, k_ref, vt_ref, yc_ref,
                 u_scr, wqk_ref, wvt_ref, wc_ref, stage, sem, *, tiles_per_seq):
    tm = x_ref.shape[0]

    @pl.when(pl.program_id(0) == 0)
    def _():
        def store_vt(block, r):
            wvt_ref[:, r:r + STAGE_ROWS] = _pack(block.T)

        jobs = _weight_jobs(w_in_hbm, wqk_ref, 0, 0, (D_MODEL, 2 * D_MODEL))
        jobs += _weight_jobs(w_in_hbm, wc_ref, 0, 3 * D_MODEL, (D_MODEL, 3 * D_MODEL))
        jobs += [(w_in_hbm, r, 2 * D_MODEL, functools.partial(store_vt, r=r))
                 for r in range(0, D_MODEL, STAGE_ROWS)]
        _stage_weights(jobs, stage, sem)

    n_col_tiles = D_MODEL // PROJ_TILE
    slabs = PROJ_TILE // LANES

    @pl.when(pl.program_id(0) % tiles_per_seq == 0)
    def _():
        u_scr[0:CARRY_ROWS, :] = jnp.zeros((CARRY_ROWS, D_MODEL), _F32)

    def sub_tile(r0, rows):
        tok = slice(r0, r0 + rows)
        h = _rms_norm_rows(x_ref[tok, :], g1_ref[...]).astype(_BF16)

        def proj(w_ref, j, c, width=PROJ_TILE):
            lo = j * D_MODEL + c * width
            return jnp.dot(h, _unpack(w_ref[:, lo:lo + width]), preferred_element_type=_F32)

        for c in range(D_MODEL // CONV_TILE):
            cols = slice(c * CONV_TILE, (c + 1) * CONV_TILE)
            bg = proj(wc_ref, 0, c, CONV_TILE)
            u = proj(wc_ref, 1, c, CONV_TILE) * proj(wc_ref, 2, c, CONV_TILE)
            base = CARRY_ROWS + r0
            u_scr[base:base + rows, cols] = u
            u1 = u_scr[base - 1:base - 1 + rows, cols]
            u2 = u_scr[base - 2:base - 2 + rows, cols]
            conv = (cb_ref[:, cols] + cw_ref[0:1, cols] * u2 + cw_ref[1:2, cols] * u1
                    + cw_ref[2:3, cols] * u)
            yc_ref[tok, cols] = (bg * conv).astype(_BF16)

        def finish_head_norm(p, g_ref, o_ref, c):
            p2 = (p * p).astype(_BF16)
            ss = jnp.concatenate(
                [jnp.dot(p2[:, t * MXU_TILE:(t + 1) * MXU_TILE], ones_ref[...],
                         preferred_element_type=_F32)
                 for t in range(PROJ_TILE // MXU_TILE)], axis=-1)
            g = g_ref[:, c * PROJ_TILE:(c + 1) * PROJ_TILE]
            pn = (p * lax.rsqrt(ss * (1.0 / HEAD_DIM) + EPS) * g).astype(_BF16)
            for sl in range(slabs):
                o_ref[c * slabs + sl, tok, :] = pn[:, sl * LANES:(sl + 1) * LANES]

        waiting = None
        for j, (g_ref, o_ref) in enumerate(((gq_ref, q_ref), (gk_ref, k_ref))):
            for c in range(n_col_tiles):
                p = proj(wqk_ref, j, c)
                if waiting is not None:
                    finish_head_norm(*waiting)
                waiting = (p, g_ref, o_ref, c)

        for c in range(n_col_tiles):
            w_rows = slice(c * PROJ_TILE // 2, (c + 1) * PROJ_TILE // 2)
            vt = lax.dot_general(_unpack(wvt_ref[w_rows, :]), h, _NT_DIMS,
                                 preferred_element_type=_F32).astype(_BF16)
            if waiting is not None:
                finish_head_norm(*waiting)
                waiting = None
            for sl in range(slabs):
                vt_ref[c * slabs + sl, :, tok] = vt[sl * LANES:(sl + 1) * LANES, :]

    rows = tm // PROJ_SUB_TILES
    for n in range(PROJ_SUB_TILES):
        sub_tile(n * rows, rows)
    u_scr[0:CARRY_ROWS, :] = u_scr[tm:tm + CARRY_ROWS, :]


def _bias_tile(rb):
    kw, tq = KEY_TILES * Q_TILE, Q_TILE
    lane = lax.broadcasted_iota(jnp.int32, rb.shape, 1)
    top = rb[:, 2 * MAX_REL:2 * MAX_REL + 1]
    fwd = jnp.where(lane > 2 * MAX_REL, top, rb) * LOG2E
    rolled = pltpu.roll(jnp.broadcast_to(fwd, (kw, BIAS_LANES)), tq, axis=1,
                        stride=1, stride_axis=0)
    kj = lax.broadcasted_iota(jnp.int32, (kw, tq), 0)
    qi = lax.broadcasted_iota(jnp.int32, (kw, tq), 1)
    dchunk = (qi // CHUNK + N_PREV_CHUNKS) - kj // CHUNK
    band = (dchunk >= 0) & (dchunk <= N_PREV_CHUNKS)
    return jnp.where(band, rolled[:, :tq], NEG_INF)


def _attn_kernel(q_ref, *refs):
    k_refs, v_refs = refs[:KEY_REFS], refs[KEY_REFS:2 * KEY_REFS]
    rb_ref, o_ref, bias_ref = refs[2 * KEY_REFS:]
    tq = Q_TILE
    lane = lax.broadcasted_iota(jnp.int32, (tq, LANES), 1)
    ones_rows = jnp.ones((BF16_ROWS, tq), _BF16)

    @pl.when((pl.program_id(0) == 0) & (pl.program_id(1) == 0))
    def _():
        def build(head, carry):
            bias_ref[head] = _bias_tile(rb_ref[head])
            return carry
        lax.fori_loop(0, N_HEADS, build, 0)

    def scores(sub, tiles, head):
        lo, hi = tiles[0] * tq, (tiles[-1] + 1) * tq
        hp, hh = divmod(head, 2)
        sel = (lane < HEAD_DIM) if hh == 0 else (lane >= HEAD_DIM)
        q = q_ref[hp, sub * tq:(sub + 1) * tq, :]
        qh = jnp.where(sel, q, jnp.zeros_like(q))
        keys = jnp.concatenate([k_refs[sub + t][hp] for t in tiles], axis=0)
        s = lax.dot_general(keys, qh, _NT_DIMS, preferred_element_type=_F32)
        cols = []
        for c in range(tq // LANES):
            r0 = max(lo, c * LANES)
            r1 = min(hi, c * LANES + BAND_ROWS)
            cols.append((r0, r1, s[r0 - lo:r1 - lo, c * LANES:(c + 1) * LANES]
                         + bias_ref[head, r0:r1, c * LANES:(c + 1) * LANES]))
        return cols

    def softmax_numerators(tiles, cols):
        lo, hi = tiles[0] * tq, (tiles[-1] + 1) * tq
        p_cols = []
        for r0, r1, s in cols:
            m = jnp.max(s, axis=0, keepdims=True)
            p = jnp.exp2(s - m).astype(_BF16)
            pads = [jnp.zeros((n, LANES), _BF16) for n in (r0 - lo, hi - r1)]
            p_cols.append(jnp.concatenate(
                [blk for blk in (pads[0], p, pads[1]) if blk.shape[0]], axis=0))
        return jnp.concatenate(p_cols, axis=1)

    def weighted_values(sub, tiles, head, p):
        hp, hh = divmod(head, 2)
        v_ext = jnp.concatenate(
            [jnp.concatenate([v_refs[sub + t][hp, hh * HEAD_DIM:(hh + 1) * HEAD_DIM, :],
                              ones_rows], axis=0) for t in tiles], axis=1)
        acc = jnp.dot(v_ext, p, preferred_element_type=_F32)
        return acc[:HEAD_DIM] * (1.0 / acc[HEAD_DIM:HEAD_DIM + 1])

    def run(tiles_per_sub):
        work = [(sub, tiles, head) for sub, tiles in enumerate(tiles_per_sub)
                for head in range(N_HEADS)]
        scored = [scores(*w) for w in work[:SCORE_LOOKAHEAD]]
        numerators, halves = [], []
        for n in range(len(work) + VALUE_DELAY):
            if n + SCORE_LOOKAHEAD < len(work):
                scored.append(scores(*work[n + SCORE_LOOKAHEAD]))
            if n < len(work):
                numerators.append(softmax_numerators(work[n][1], scored.pop(0)))
            if n >= VALUE_DELAY:
                sub, tiles, head = work[n - VALUE_DELAY]
                halves.append(weighted_values(sub, tiles, head, numerators.pop(0)))
                if head % 2:
                    o_ref[head // 2, sub * tq:(sub + 1) * tq, :] = (
                        jnp.concatenate(halves, axis=0).T.astype(_BF16))
                    halves = []

    all_tiles = tuple(range(KEY_TILES))
    first_step = tuple(all_tiles[max(KEY_TILES - 1 - j, 0):] for j in range(Q_TILES_PER_STEP))
    qs = pl.program_id(1)
    pl.when(qs == 0)(functools.partial(run, first_step))
    pl.when(qs > 0)(functools.partial(run, (all_tiles,) * Q_TILES_PER_STEP))


def _mix_mlp_kernel(x_ref, a_ref, yc_ref, g1_ref, bgate_ref, g2_ref,
                    wg_hbm, wap_hbm, wcp_hbm, wout_hbm, wup_hbm, wdown_hbm, o_ref,
                    wg_ref, wap_ref, wcp_ref, wout_ref, wup_ref, wdown_ref, stage, sem):
    @pl.when(pl.program_id(0) == 0)
    def _():
        pairs = ((wg_hbm, wg_ref), (wap_hbm, wap_ref), (wcp_hbm, wcp_ref), (wout_hbm, wout_ref),
                 (wup_hbm, wup_ref), (wdown_hbm, wdown_ref))
        _stage_weights([job for src, dst in pairs for job in _weight_jobs(src, dst)], stage, sem)

    def dot(lhs, w_ref, rows=slice(None), cols=slice(None)):
        return jnp.dot(lhs, _unpack(w_ref[rows, cols]), preferred_element_type=_F32)

    sub_rows = x_ref.shape[0] // MLP_SUB_TILES
    subs = [slice(n * sub_rows, (n + 1) * sub_rows) for n in range(MLP_SUB_TILES)]

    def branches(rows):
        x = x_ref[rows, :]
        h = _rms_norm_rows(x, g1_ref[...]).astype(_BF16)
        gates = jax.nn.sigmoid(dot(h, wg_ref) + bgate_ref[...])
        a = jnp.concatenate([a_ref[hp, rows, :] for hp in range(HEAD_PAIRS)], axis=-1)
        ya = dot(a, wap_ref)
        yc = dot(yc_ref[rows, :], wcp_ref)
        return x, (gates[:, :D_MODEL] * ya + gates[:, D_MODEL:] * yc).astype(_BF16)

    def residual(x, merged):
        return x + dot(merged, wout_ref)

    def mlp(rows, x1):
        h2 = _rms_norm_rows(x1, g2_ref[...]).astype(_BF16)
        acc = x1
        for f in range(D_FF // FF_TILE):
            up = dot(h2, wup_ref, cols=slice(f * FF_TILE, (f + 1) * FF_TILE))
            act = jnp.square(jnp.maximum(up, 0.0)).astype(_BF16)
            acc = acc + dot(act, wdown_ref, rows=slice(f * FF_TILE // 2, (f + 1) * FF_TILE // 2))
        o_ref[rows, :] = acc

    merged = [branches(rows) for rows in subs]
    x1 = [residual(*m) for m in merged]
    for rows, v in zip(subs, x1):
        mlp(rows, v)


def kernel(x, norm1_g, w_in, q_norm_g, k_norm_g, rel_bias, conv_w, conv_b, w_attn_proj,
           w_conv_proj, w_gate, b_gate, w_out, norm2_g, w_up, w_down):
    b, s, d = x.shape
    assert d == D_MODEL and s % ROW_TILE == 0 and s % PROJ_ROWS == 0
    assert s % (Q_TILE * Q_TILES_PER_STEP) == 0
    t = b * s
    xf = x.reshape(t, d)
    row = lambda v: v.reshape(1, -1).astype(_F32)
    cparams = functools.partial(pltpu.CompilerParams, vmem_limit_bytes=VMEM_LIMIT)

    gq = row(jnp.tile(q_norm_g, N_HEADS) * (HEAD_DIM ** -0.5 * LOG2E))
    gk = row(jnp.tile(k_norm_g, N_HEADS))
    r = jnp.arange(MXU_TILE)
    head_ones = (r[:, None] // HEAD_DIM == r[None, :] // HEAD_DIM).astype(_BF16)
    staging = [pltpu.VMEM((STAGE_SLOTS, STAGE_ROWS, STAGE_COLS), _F32),
               pltpu.SemaphoreType.DMA((STAGE_SLOTS,))]
    heads_shape = jax.ShapeDtypeStruct((HEAD_PAIRS, t, LANES), _BF16)

    def heads_spec(rows):
        return pl.BlockSpec((HEAD_PAIRS, rows, LANES), lambda i: (0, i, 0))

    q, k, vt, yc = pl.pallas_call(
        functools.partial(_proj_kernel, tiles_per_seq=s // PROJ_ROWS),
        grid=(t // PROJ_ROWS,),
        in_specs=[
            pl.BlockSpec((PROJ_ROWS, d), lambda i: (i, 0)),
            _resident((1, d)),
            pl.BlockSpec(memory_space=pl.ANY),
            _resident((1, d)),
            _resident((1, d)),
            _resident((MXU_TILE, MXU_TILE)),
            _resident((CONV_WIDTH, d)),
            _resident((1, d)),
        ],
        out_specs=[heads_spec(PROJ_ROWS), heads_spec(PROJ_ROWS),
                   pl.BlockSpec((HEAD_PAIRS, LANES, PROJ_ROWS), lambda i: (0, 0, i)),
                   pl.BlockSpec((PROJ_ROWS, d), lambda i: (i, 0))],
        out_shape=[heads_shape, heads_shape,
                   jax.ShapeDtypeStruct((HEAD_PAIRS, LANES, t), _BF16),
                   jax.ShapeDtypeStruct((t, d), _BF16)],
        scratch_shapes=[pltpu.VMEM((PROJ_ROWS + CARRY_ROWS, d), _F32),
                        _packed_weight((d, 2 * d)), _packed_weight((d, d)),
                        _packed_weight((d, 3 * d))] + staging,
        compiler_params=cparams(dimension_semantics=("arbitrary",)),
        name="proj",
    )(xf, row(norm1_g), w_in.astype(_F32), gq, gk, head_ones, conv_w.astype(_F32), row(conv_b))

    n_rel = rel_bias.shape[1]
    rb_rows = jnp.pad(rel_bias.astype(_F32), ((0, 0), (0, BIAS_LANES - n_rel)))
    nq = s // Q_TILE
    ns = nq // Q_TILES_PER_STEP
    q_spec = pl.BlockSpec((HEAD_PAIRS, Q_TILES_PER_STEP * Q_TILE, LANES),
                          lambda bi, si: (0, bi * ns + si, 0))

    def key_tile(si, ref_idx):
        return jnp.maximum(si * Q_TILES_PER_STEP - (KEY_TILES - 1) + ref_idx, 0)

    def k_spec(ref_idx):
        return pl.BlockSpec((HEAD_PAIRS, Q_TILE, LANES),
                            lambda bi, si: (0, bi * nq + key_tile(si, ref_idx), 0))

    def vt_spec(ref_idx):
        return pl.BlockSpec((HEAD_PAIRS, LANES, Q_TILE),
                            lambda bi, si: (0, 0, bi * nq + key_tile(si, ref_idx)))

    attn = pl.pallas_call(
        _attn_kernel,
        grid=(b, ns),
        in_specs=[q_spec] + [k_spec(r) for r in range(KEY_REFS)]
                 + [vt_spec(r) for r in range(KEY_REFS)]
                 + [_resident((N_HEADS, 1, BIAS_LANES))],
        out_specs=q_spec,
        out_shape=heads_shape,
        scratch_shapes=[pltpu.VMEM((N_HEADS, KEY_TILES * Q_TILE, Q_TILE), _F32)],
        compiler_params=cparams(dimension_semantics=("arbitrary", "arbitrary")),
        name="attn",
    )(q, *([k] * KEY_REFS), *([vt] * KEY_REFS), rb_rows.reshape(N_HEADS, 1, BIAS_LANES))

    mlp_weights = [w.astype(_F32) for w in (w_gate, w_attn_proj, w_conv_proj, w_out, w_up, w_down)]
    out = pl.pallas_call(
        _mix_mlp_kernel,
        grid=(t // ROW_TILE,),
        in_specs=[
            pl.BlockSpec((ROW_TILE, d), lambda i: (i, 0)),
            heads_spec(ROW_TILE),
            pl.BlockSpec((ROW_TILE, d), lambda i: (i, 0)),
            _resident((1, d)),
            _resident((1, 2 * d)),
            _resident((1, d)),
        ] + [pl.BlockSpec(memory_space=pl.ANY)] * len(mlp_weights),
        out_specs=pl.BlockSpec((ROW_TILE, d), lambda i: (i, 0)),
        out_shape=jax.ShapeDtypeStruct((t, d), _F32),
        scratch_shapes=[_packed_weight(w.shape) for w in mlp_weights] + staging,
        compiler_params=cparams(dimension_semantics=("arbitrary",)),
        name="mix_mlp",
    )(xf, attn, yc, row(norm1_g), row(b_gate), row(norm2_g), *mlp_weights)
    return out.reshape(b, s, d)
```

```python
import functools
import math

import jax
import jax.numpy as jnp
from jax import lax
from jax.experimental import pallas as pl
from jax.experimental.pallas import tpu as pltpu

D_MODEL = 1024
N_HEADS = 16
HEAD_DIM = 64
CHUNK = 64
N_PREV_CHUNKS = 8
MAX_REL = 256
CONV_WIDTH = 3
D_FF = 4 * D_MODEL
EPS = 1e-6
NEG_INF = -1e30
LOG2E = math.log2(math.e)

LANES = 128
BF16_ROWS = 16
HEAD_PAIRS = D_MODEL // LANES
MXU_TILE = 256

PROJ_ROWS = 512
ROW_TILE = 512
PROJ_TILE = 2 * MXU_TILE
CONV_TILE = MXU_TILE
Q_TILE = 256
KEY_TILES = 1 + (N_PREV_CHUNKS * CHUNK) // Q_TILE
Q_TILES_PER_STEP = 2
KEY_REFS = KEY_TILES + Q_TILES_PER_STEP - 1
BAND_ROWS = (N_PREV_CHUNKS + LANES // CHUNK) * CHUNK
BIAS_LANES = (KEY_TILES + 1) * Q_TILE
assert 2 * MAX_REL + 1 <= BIAS_LANES and Q_TILE - 1 <= MAX_REL
SCORE_LOOKAHEAD = 3
VALUE_DELAY = 0
HEADS_PER_MXU_TILE = MXU_TILE // HEAD_DIM
CONV_PROJ_EVERY = N_HEADS // (D_MODEL // PROJ_TILE)
FF_TILE = 1024
PROJ_SUB_TILES = 2
MLP_SUB_TILES = 2
CARRY_ROWS = 8
STAGE_ROWS, STAGE_COLS = 256, D_MODEL
STAGE_SLOTS = 4
VMEM_LIMIT = 56 * 1024 * 1024

_BF16 = jnp.bfloat16
_F32 = jnp.float32
_NT_DIMS = (((1,), (1,)), ((), ()))


def _resident(shape):
    return pl.BlockSpec(shape, lambda *_: (0,) * len(shape), pipeline_mode=pl.Buffered(1))


def _rms_norm_rows(x, g):
    ms = jnp.mean(x * x, axis=-1, keepdims=True)
    return x * lax.rsqrt(ms + EPS) * g


def _packed_weight(shape):
    k, n = shape
    return pltpu.VMEM((k // 2, n), jnp.uint32)


def _pack(block):
    return pltpu.bitcast(block.astype(_BF16), jnp.uint32)


def _unpack(words):
    return pltpu.bitcast(words, _BF16)


def _stage_weights(jobs, stage, sem):
    def copy(n):
        src, r0, c0, _ = jobs[n]
        return pltpu.make_async_copy(src.at[pl.ds(r0, STAGE_ROWS), pl.ds(c0, STAGE_COLS)],
                                     stage.at[n % STAGE_SLOTS], sem.at[n % STAGE_SLOTS])

    ahead = STAGE_SLOTS - 1
    for n in range(min(ahead, len(jobs))):
        copy(n).start(priority=n % 2)
    for n, job in enumerate(jobs):
        if n + ahead < len(jobs):
            copy(n + ahead).start(priority=(n + ahead) % 2)
        copy(n).wait()
        job[3](stage[n % STAGE_SLOTS])


def _weight_jobs(src, dst, row_lo=0, col_lo=0, shape=None):
    k, n = shape if shape is not None else src.shape
    jobs = []
    for r in range(0, k, STAGE_ROWS):
        for c in range(0, n, STAGE_COLS):
            def store(block, r=r, c=c):
                dst[r // 2:(r + STAGE_ROWS) // 2, c:c + STAGE_COLS] = _pack(block)
            jobs.append((src, row_lo + r, col_lo + c, store))
    return jobs


def _proj_kernel(x_ref, g1_ref, w_in_hbm, gq_ref, gk_ref, ones_ref, cw_ref, cb_ref,
                 q_ref, k_ref, vt_ref, yc_ref,
                 u_scr, wqk_ref, wvt_ref, wc_ref, stage, sem, *, tiles_per_seq):
    tm = x_ref.shape[0]

    @pl.when(pl.program_id(0) == 0)
    def _():
        def store_vt(block, r):
            wvt_ref[:, r:r + STAGE_ROWS] = _pack(block.T)

        jobs = _weight_jobs(w_in_hbm, wqk_ref, 0, 0, (D_MODEL, 2 * D_MODEL))
        jobs += _weight_jobs(w_in_hbm, wc_ref, 0, 3 * D_MODEL, (D_MODEL, 3 * D_MODEL))
        jobs += [(w_in_hbm, r, 2 * D_MODEL, functools.partial(store_vt, r=r))
                 for r in range(0, D_MODEL, STAGE_ROWS)]
        _stage_weights(jobs, stage, sem)

    n_col_tiles = D_MODEL // PROJ_TILE
    slabs = PROJ_TILE // LANES

    @pl.when(pl.program_id(0) % tiles_per_seq == 0)
    def _():
        u_scr[0:CARRY_ROWS, :] = jnp.zeros((CARRY_ROWS, D_MODEL), _F32)

    def sub_tile(r0, rows):
        tok = slice(r0, r0 + rows)
        h = _rms_norm_rows(x_ref[tok, :], g1_ref[...]).astype(_BF16)

        def proj(w_ref, j, c, width=PROJ_TILE):
            lo = j * D_MODEL + c * width
            return jnp.dot(h, _unpack(w_ref[:, lo:lo + width]), preferred_element_type=_F32)

        for c in range(D_MODEL // CONV_TILE):
            cols = slice(c * CONV_TILE, (c + 1) * CONV_TILE)
            bg = proj(wc_ref, 0, c, CONV_TILE)
            u = proj(wc_ref, 1, c, CONV_TILE) * proj(wc_ref, 2, c, CONV_TILE)
            base = CARRY_ROWS + r0
            u_scr[base:base + rows, cols] = u
            u1 = u_scr[base - 1:base - 1 + rows, cols]
            u2 = u_scr[base - 2:base - 2 + rows, cols]
            conv = (cb_ref[:, cols] + cw_ref[0:1, cols] * u2 + cw_ref[1:2, cols] * u1
                    + cw_ref[2:3, cols] * u)
            yc_ref[tok, cols] = (bg * conv).astype(_BF16)

        def finish_head_norm(p, g_ref, o_ref, c):
            p2 = (p * p).astype(_BF16)
            ss = jnp.concatenate(
                [jnp.dot(p2[:, t * MXU_TILE:(t + 1) * MXU_TILE], ones_ref[...],
                         preferred_element_type=_F32)
                 for t in range(PROJ_TILE // MXU_TILE)], axis=-1)
            g = g_ref[:, c * PROJ_TILE:(c + 1) * PROJ_TILE]
            pn = (p * lax.rsqrt(ss * (1.0 / HEAD_DIM) + EPS) * g).astype(_BF16)
            for sl in range(slabs):
                o_ref[c * slabs + sl, tok, :] = pn[:, sl * LANES:(sl + 1) * LANES]

        waiting = None
        for j, (g_ref, o_ref) in enumerate(((gq_ref, q_ref), (gk_ref, k_ref))):
            for c in range(n_col_tiles):
                p = proj(wqk_ref, j, c)
                if waiting is not None:
                    finish_head_norm(*waiting)
                waiting = (p, g_ref, o_ref, c)

        for c in range(n_col_tiles):
            w_rows = slice(c * PROJ_TILE // 2, (c + 1) * PROJ_TILE // 2)
            vt = lax.dot_general(_unpack(wvt_ref[w_rows, :]), h, _NT_DIMS,
                                 preferred_element_type=_F32).astype(_BF16)
            if waiting is not None:
                finish_head_norm(*waiting)
                waiting = None
            for sl in range(slabs):
                vt_ref[c * slabs + sl, :, tok] = vt[sl * LANES:(sl + 1) * LANES, :]

    rows = tm // PROJ_SUB_TILES
    for n in range(PROJ_SUB_TILES):
        sub_tile(n * rows, rows)
    u_scr[0:CARRY_ROWS, :] = u_scr[tm:tm + CARRY_ROWS, :]


def _bias_tile(rb):
    kw, tq = KEY_TILES * Q_TILE, Q_TILE
    lane = lax.broadcasted_iota(jnp.int32, rb.shape, 1)
    top = rb[:, 2 * MAX_REL:2 * MAX_REL + 1]
    fwd = jnp.where(lane > 2 * MAX_REL, top, rb) * LOG2E
    rolled = pltpu.roll(jnp.broadcast_to(fwd, (kw, BIAS_LANES)), tq, axis=1,
                        stride=1, stride_axis=0)
    kj = lax.broadcasted_iota(jnp.int32, (kw, tq), 0)
    qi = lax.broadcasted_iota(jnp.int32, (kw, tq), 1)
    dchunk = (qi // CHUNK + N_PREV_CHUNKS) - kj // CHUNK
    band = (dchunk >= 0) & (dchunk <= N_PREV_CHUNKS)
    return jnp.where(band, rolled[:, :tq], NEG_INF)


def _attn_kernel(q_ref, *refs):
    k_refs, v_refs = refs[:KEY_REFS], refs[KEY_REFS:2 * KEY_REFS]
    (rb_ref, yc_ref, wap_hbm, wcp_hbm, ya_ref, ycp_ref,
     bias_ref, wap_ref, wcp_ref, stage, sem) = refs[2 * KEY_REFS:]
    tq = Q_TILE
    lane = lax.broadcasted_iota(jnp.int32, (tq, LANES), 1)
    ones_rows = jnp.ones((BF16_ROWS, tq), _BF16)

    @pl.when((pl.program_id(0) == 0) & (pl.program_id(1) == 0))
    def _():
        _stage_weights(_weight_jobs(wap_hbm, wap_ref) + _weight_jobs(wcp_hbm, wcp_ref),
                       stage, sem)

        def build(head, carry):
            bias_ref[head] = _bias_tile(rb_ref[head])
            return carry
        lax.fori_loop(0, N_HEADS, build, 0)

    def scores(sub, tiles, head):
        lo, hi = tiles[0] * tq, (tiles[-1] + 1) * tq
        hp, hh = divmod(head, 2)
        sel = (lane < HEAD_DIM) if hh == 0 else (lane >= HEAD_DIM)
        q = q_ref[hp, sub * tq:(sub + 1) * tq, :]
        qh = jnp.where(sel, q, jnp.zeros_like(q))
        keys = jnp.concatenate([k_refs[sub + t][hp] for t in tiles], axis=0)
        s = lax.dot_general(keys, qh, _NT_DIMS, preferred_element_type=_F32)
        cols = []
        for c in range(tq // LANES):
            r0 = max(lo, c * LANES)
            r1 = min(hi, c * LANES + BAND_ROWS)
            cols.append((r0, r1, s[r0 - lo:r1 - lo, c * LANES:(c + 1) * LANES]
                         + bias_ref[head, r0:r1, c * LANES:(c + 1) * LANES]))
        return cols

    def softmax_numerators(tiles, cols):
        lo, hi = tiles[0] * tq, (tiles[-1] + 1) * tq
        p_cols = []
        for r0, r1, s in cols:
            m = jnp.max(s, axis=0, keepdims=True)
            p = jnp.exp2(s - m).astype(_BF16)
            pads = [jnp.zeros((n, LANES), _BF16) for n in (r0 - lo, hi - r1)]
            p_cols.append(jnp.concatenate(
                [blk for blk in (pads[0], p, pads[1]) if blk.shape[0]], axis=0))
        return jnp.concatenate(p_cols, axis=1)

    def weighted_values(sub, tiles, head, p):
        hp, hh = divmod(head, 2)
        v_ext = jnp.concatenate(
            [jnp.concatenate([v_refs[sub + t][hp, hh * HEAD_DIM:(hh + 1) * HEAD_DIM, :],
                              ones_rows], axis=0) for t in tiles], axis=1)
        acc = jnp.dot(v_ext, p, preferred_element_type=_F32)
        return acc[:HEAD_DIM] * (1.0 / acc[HEAD_DIM:HEAD_DIM + 1])

    def run(tiles_per_sub):
        work = [(sub, tiles, head) for sub, tiles in enumerate(tiles_per_sub)
                for head in range(N_HEADS)]
        scored = [scores(*w) for w in work[:SCORE_LOOKAHEAD]]
        numerators, halves, slabs, ya = [], [], [], None
        for n in range(len(work) + VALUE_DELAY):
            if n + SCORE_LOOKAHEAD < len(work):
                scored.append(scores(*work[n + SCORE_LOOKAHEAD]))
            if n < len(work):
                sub, _, head = work[n]
                if head % CONV_PROJ_EVERY == 0:
                    conv_branch_projection(sub, head // CONV_PROJ_EVERY)
                numerators.append(softmax_numerators(work[n][1], scored.pop(0)))
            if n >= VALUE_DELAY:
                sub, tiles, head = work[n - VALUE_DELAY]
                halves.append(weighted_values(sub, tiles, head, numerators.pop(0)))
                if head % 2:
                    slabs.append(jnp.concatenate(halves, axis=0).T.astype(_BF16))
                    halves = []
                if len(slabs) == MXU_TILE // LANES:
                    c = head // HEADS_PER_MXU_TILE
                    part = jnp.dot(jnp.concatenate(slabs, axis=1),
                                   _unpack(wap_ref[c * MXU_TILE // 2:(c + 1) * MXU_TILE // 2, :]),
                                   preferred_element_type=_F32)
                    ya = part if ya is None else ya + part
                    slabs = []
                if head == N_HEADS - 1:
                    ya_ref[sub * tq:(sub + 1) * tq, :] = ya
                    ya = None

    def conv_branch_projection(sub, c):
        cols = slice(c * PROJ_TILE, (c + 1) * PROJ_TILE)
        rows = slice(sub * tq, (sub + 1) * tq)
        ycp_ref[rows, cols] = jnp.dot(yc_ref[rows, :], _unpack(wcp_ref[:, cols]),
                                      preferred_element_type=_F32)

    all_tiles = tuple(range(KEY_TILES))
    first_step = tuple(all_tiles[max(KEY_TILES - 1 - j, 0):] for j in range(Q_TILES_PER_STEP))
    qs = pl.program_id(1)
    pl.when(qs == 0)(functools.partial(run, first_step))
    pl.when(qs > 0)(functools.partial(run, (all_tiles,) * Q_TILES_PER_STEP))


def _mix_mlp_kernel(x_ref, ya_ref, ycp_ref, g1_ref, bgate_ref, g2_ref,
                    wg_hbm, wout_hbm, wup_hbm, wdown_hbm, o_ref,
                    wg_ref, wout_ref, wup_ref, wdown_ref, stage, sem):
    @pl.when(pl.program_id(0) == 0)
    def _():
        pairs = ((wg_hbm, wg_ref), (wout_hbm, wout_ref), (wup_hbm, wup_ref),
                 (wdown_hbm, wdown_ref))
        _stage_weights([job for src, dst in pairs for job in _weight_jobs(src, dst)], stage, sem)

    def dot(lhs, w_ref, rows=slice(None), cols=slice(None)):
        return jnp.dot(lhs, _unpack(w_ref[rows, cols]), preferred_element_type=_F32)

    sub_rows = x_ref.shape[0] // MLP_SUB_TILES
    subs = [slice(n * sub_rows, (n + 1) * sub_rows) for n in range(MLP_SUB_TILES)]

    def branches(rows):
        x = x_ref[rows, :]
        h = _rms_norm_rows(x, g1_ref[...]).astype(_BF16)
        gates = jax.nn.sigmoid(dot(h, wg_ref) + bgate_ref[...])
        merged = gates[:, :D_MODEL] * ya_ref[rows, :] + gates[:, D_MODEL:] * ycp_ref[rows, :]
        return x, merged.astype(_BF16)

    def residual(x, merged):
        return x + dot(merged, wout_ref)

    def mlp(rows, x1):
        h2 = _rms_norm_rows(x1, g2_ref[...]).astype(_BF16)
        acc = x1
        for f in range(D_FF // FF_TILE):
            up = dot(h2, wup_ref, cols=slice(f * FF_TILE, (f + 1) * FF_TILE))
            act = jnp.square(jnp.maximum(up, 0.0)).astype(_BF16)
            acc = acc + dot(act, wdown_ref, rows=slice(f * FF_TILE // 2, (f + 1) * FF_TILE // 2))
        o_ref[rows, :] = acc

    merged = [branches(rows) for rows in subs]
    x1 = [residual(*m) for m in merged]
    for rows, v in zip(subs, x1):
        mlp(rows, v)


def kernel(x, norm1_g, w_in, q_norm_g, k_norm_g, rel_bias, conv_w, conv_b, w_attn_proj,
           w_conv_proj, w_gate, b_gate, w_out, norm2_g, w_up, w_down):
    b, s, d = x.shape
    assert d == D_MODEL and s % ROW_TILE == 0 and s % PROJ_ROWS == 0
    assert s % (Q_TILE * Q_TILES_PER_STEP) == 0
    t = b * s
    xf = x.reshape(t, d)
    row = lambda v: v.reshape(1, -1).astype(_F32)
    cparams = functools.partial(pltpu.CompilerParams, vmem_limit_bytes=VMEM_LIMIT)

    gq = row(jnp.tile(q_norm_g, N_HEADS) * (HEAD_DIM ** -0.5 * LOG2E))
    gk = row(jnp.tile(k_norm_g, N_HEADS))
    r = jnp.arange(MXU_TILE)
    head_ones = (r[:, None] // HEAD_DIM == r[None, :] // HEAD_DIM).astype(_BF16)
    staging = [pltpu.VMEM((STAGE_SLOTS, STAGE_ROWS, STAGE_COLS), _F32),
               pltpu.SemaphoreType.DMA((STAGE_SLOTS,))]
    heads_shape = jax.ShapeDtypeStruct((HEAD_PAIRS, t, LANES), _BF16)

    def heads_spec(rows):
        return pl.BlockSpec((HEAD_PAIRS, rows, LANES), lambda i: (0, i, 0))

    q, k, vt, yc = pl.pallas_call(
        functools.partial(_proj_kernel, tiles_per_seq=s // PROJ_ROWS),
        grid=(t // PROJ_ROWS,),
        in_specs=[
            pl.BlockSpec((PROJ_ROWS, d), lambda i: (i, 0)),
            _resident((1, d)),
            pl.BlockSpec(memory_space=pl.ANY),
            _resident((1, d)),
            _resident((1, d)),
            _resident((MXU_TILE, MXU_TILE)),
            _resident((CONV_WIDTH, d)),
            _resident((1, d)),
        ],
        out_specs=[heads_spec(PROJ_ROWS), heads_spec(PROJ_ROWS),
                   pl.BlockSpec((HEAD_PAIRS, LANES, PROJ_ROWS), lambda i: (0, 0, i)),
                   pl.BlockSpec((PROJ_ROWS, d), lambda i: (i, 0))],
        out_shape=[heads_shape, heads_shape,
                   jax.ShapeDtypeStruct((HEAD_PAIRS, LANES, t), _BF16),
                   jax.ShapeDtypeStruct((t, d), _BF16)],
        scratch_shapes=[pltpu.VMEM((PROJ_ROWS + CARRY_ROWS, d), _F32),
                        _packed_weight((d, 2 * d)), _packed_weight((d, d)),
                        _packed_weight((d, 3 * d))] + staging,
        compiler_params=cparams(dimension_semantics=("arbitrary",)),
        name="proj",
    )(xf, row(norm1_g), w_in.astype(_F32), gq, gk, head_ones, conv_w.astype(_F32), row(conv_b))

    n_rel = rel_bias.shape[1]
    rb_rows = jnp.pad(rel_bias.astype(_F32), ((0, 0), (0, BIAS_LANES - n_rel)))
    nq = s // Q_TILE
    ns = nq // Q_TILES_PER_STEP
    q_spec = pl.BlockSpec((HEAD_PAIRS, Q_TILES_PER_STEP * Q_TILE, LANES),
                          lambda bi, si: (0, bi * ns + si, 0))

    def key_tile(si, ref_idx):
        return jnp.maximum(si * Q_TILES_PER_STEP - (KEY_TILES - 1) + ref_idx, 0)

    def k_spec(ref_idx):
        return pl.BlockSpec((HEAD_PAIRS, Q_TILE, LANES),
                            lambda bi, si: (0, bi * nq + key_tile(si, ref_idx), 0))

    def vt_spec(ref_idx):
        return pl.BlockSpec((HEAD_PAIRS, LANES, Q_TILE),
                            lambda bi, si: (0, 0, bi * nq + key_tile(si, ref_idx)))

    step_rows = Q_TILES_PER_STEP * Q_TILE
    rows_spec = pl.BlockSpec((step_rows, d), lambda bi, si: (bi * ns + si, 0))
    branch_shape = jax.ShapeDtypeStruct((t, d), _F32)
    ya, ycp = pl.pallas_call(
        _attn_kernel,
        grid=(b, ns),
        in_specs=[q_spec] + [k_spec(r) for r in range(KEY_REFS)]
                 + [vt_spec(r) for r in range(KEY_REFS)]
                 + [_resident((N_HEADS, 1, BIAS_LANES)), rows_spec,
                    pl.BlockSpec(memory_space=pl.ANY), pl.BlockSpec(memory_space=pl.ANY)],
        out_specs=[rows_spec, rows_spec],
        out_shape=[branch_shape, branch_shape],
        scratch_shapes=[pltpu.VMEM((N_HEADS, KEY_TILES * Q_TILE, Q_TILE), _F32),
                        _packed_weight((d, d)), _packed_weight((d, d))] + staging,
        compiler_params=cparams(dimension_semantics=("arbitrary", "arbitrary")),
        name="attn",
    )(q, *([k] * KEY_REFS), *([vt] * KEY_REFS), rb_rows.reshape(N_HEADS, 1, BIAS_LANES), yc,
      w_attn_proj.astype(_F32), w_conv_proj.astype(_F32))

    mlp_weights = [w.astype(_F32) for w in (w_gate, w_out, w_up, w_down)]
    out = pl.pallas_call(
        _mix_mlp_kernel,
        grid=(t // ROW_TILE,),
        in_specs=[
            pl.BlockSpec((ROW_TILE, d), lambda i: (i, 0)),
            pl.BlockSpec((ROW_TILE, d), lambda i: (i, 0)),
            pl.BlockSpec((ROW_TILE, d), lambda i: (i, 0)),
            _resident((1, d)),
            _resident((1, 2 * d)),
            _resident((1, d)),
        ] + [pl.BlockSpec(memory_space=pl.ANY)] * len(mlp_weights),
        out_specs=pl.BlockSpec((ROW_TILE, d), lambda i: (i, 0)),
        out_shape=jax.ShapeDtypeStruct((t, d), _F32),
        scratch_shapes=[_packed_weight(w.shape) for w in mlp_weights] + staging,
        compiler_params=cparams(dimension_semantics=("arbitrary",)),
        name="mix_mlp",
    )(xf, ya, ycp, row(norm1_g), row(b_gate), row(norm2_g), *mlp_weights)
    return out.reshape(b, s, d)
```

```python
import functools
import math

import jax
import jax.numpy as jnp
from jax import lax
from jax.experimental import pallas as pl
from jax.experimental.pallas import tpu as pltpu

D_MODEL = 1024
N_HEADS = 16
HEAD_DIM = 64
CHUNK = 64
N_PREV_CHUNKS = 8
MAX_REL = 256
CONV_WIDTH = 3
D_FF = 4 * D_MODEL
EPS = 1e-6
NEG_INF = -1e30
LOG2E = math.log2(math.e)

LANES = 128
BF16_ROWS = 16
HEAD_PAIRS = D_MODEL // LANES
MXU_TILE = 256

PROJ_ROWS = 512
ROW_TILE = 512
PROJ_TILE = 2 * MXU_TILE
CONV_TILE = MXU_TILE
Q_TILE = 256
KEY_TILES = 1 + (N_PREV_CHUNKS * CHUNK) // Q_TILE
Q_TILES_PER_STEP = 2
KEY_REFS = KEY_TILES + Q_TILES_PER_STEP - 1
BAND_ROWS = (N_PREV_CHUNKS + LANES // CHUNK) * CHUNK
BIAS_LANES = (KEY_TILES + 1) * Q_TILE
assert 2 * MAX_REL + 1 <= BIAS_LANES and Q_TILE - 1 <= MAX_REL
SCORE_LOOKAHEAD = 3
VALUE_DELAY = 0
FF_TILE = 1024
PROJ_SUB_TILES = 2
MLP_SUB_TILES = 2
CARRY_ROWS = 8
STAGE_ROWS, STAGE_COLS = 256, D_MODEL
STAGE_SLOTS = 4
VMEM_LIMIT = 56 * 1024 * 1024

_BF16 = jnp.bfloat16
_F32 = jnp.float32
_NT_DIMS = (((1,), (1,)), ((), ()))


def _resident(shape):
    return pl.BlockSpec(shape, lambda *_: (0,) * len(shape), pipeline_mode=pl.Buffered(1))


def _rms_norm_rows(x, g):
    ms = jnp.mean(x * x, axis=-1, keepdims=True)
    return x * lax.rsqrt(ms + EPS) * g


def _packed_weight(shape):
    k, n = shape
    return pltpu.VMEM((k // 2, n), jnp.uint32)


def _pack(block):
    return pltpu.bitcast(block.astype(_BF16), jnp.uint32)


def _unpack(words):
    return pltpu.bitcast(words, _BF16)


def _stage_weights(jobs, stage, sem):
    def copy(n):
        src, r0, c0, _ = jobs[n]
        return pltpu.make_async_copy(src.at[pl.ds(r0, STAGE_ROWS), pl.ds(c0, STAGE_COLS)],
                                     stage.at[n % STAGE_SLOTS], sem.at[n % STAGE_SLOTS])

    ahead = STAGE_SLOTS - 1
    for n in range(min(ahead, len(jobs))):
        copy(n).start(priority=n % 2)
    for n, job in enumerate(jobs):
        if n + ahead < len(jobs):
            copy(n + ahead).start(priority=(n + ahead) % 2)
        copy(n).wait()
        job[3](stage[n % STAGE_SLOTS])


def _weight_jobs(src, dst, row_lo=0, col_lo=0, shape=None):
    k, n = shape if shape is not None else src.shape
    jobs = []
    for r in range(0, k, STAGE_ROWS):
        for c in range(0, n, STAGE_COLS):
            def store(block, r=r, c=c):
                dst[r // 2:(r + STAGE_ROWS) // 2, c:c + STAGE_COLS] = _pack(block)
            jobs.append((src, row_lo + r, col_lo + c, store))
    return jobs


def _proj_kernel(x_ref, g1_ref, w_in_hbm, gq_ref, gk_ref, cw_ref, cb_ref,
                 q_ref, k_ref, vt_ref, yc_ref,
                 u_scr, wqk_ref, wvt_ref, wc_ref, stage, sem, *, tiles_per_seq):
    tm = x_ref.shape[0]

    @pl.when(pl.program_id(0) == 0)
    def _():
        def store_vt(block, r):
            wvt_ref[:, r:r + STAGE_ROWS] = _pack(block.T)

        jobs = _weight_jobs(w_in_hbm, wqk_ref, 0, 0, (D_MODEL, 2 * D_MODEL))
        jobs += _weight_jobs(w_in_hbm, wc_ref, 0, 3 * D_MODEL, (D_MODEL, 3 * D_MODEL))
        jobs += [(w_in_hbm, r, 2 * D_MODEL, functools.partial(store_vt, r=r))
                 for r in range(0, D_MODEL, STAGE_ROWS)]
        _stage_weights(jobs, stage, sem)

    n_col_tiles = D_MODEL // PROJ_TILE
    slabs = PROJ_TILE // LANES

    @pl.when(pl.program_id(0) % tiles_per_seq == 0)
    def _():
        u_scr[0:CARRY_ROWS, :] = jnp.zeros((CARRY_ROWS, D_MODEL), _F32)

    def sub_tile(r0, rows):
        tok = slice(r0, r0 + rows)
        h = _rms_norm_rows(x_ref[tok, :], g1_ref[...]).astype(_BF16)
        low_half = lax.broadcasted_iota(jnp.int32, (rows, LANES), 1) < HEAD_DIM

        def proj(w_ref, j, c, width=PROJ_TILE):
            lo = j * D_MODEL + c * width
            return jnp.dot(h, _unpack(w_ref[:, lo:lo + width]), preferred_element_type=_F32)

        for c in range(D_MODEL // CONV_TILE):
            cols = slice(c * CONV_TILE, (c + 1) * CONV_TILE)
            bg = proj(wc_ref, 0, c, CONV_TILE)
            u = proj(wc_ref, 1, c, CONV_TILE) * proj(wc_ref, 2, c, CONV_TILE)
            base = CARRY_ROWS + r0
            u_scr[base:base + rows, cols] = u
            u1 = u_scr[base - 1:base - 1 + rows, cols]
            u2 = u_scr[base - 2:base - 2 + rows, cols]
            conv = (cb_ref[:, cols] + cw_ref[0:1, cols] * u2 + cw_ref[1:2, cols] * u1
                    + cw_ref[2:3, cols] * u)
            yc_ref[tok, cols] = (bg * conv).astype(_BF16)

        def finish_head_norm(p, g_ref, o_ref, c):
            g = g_ref[:, c * PROJ_TILE:(c + 1) * PROJ_TILE]
            for sl in range(slabs):
                ps = p[:, sl * LANES:(sl + 1) * LANES]
                p2 = ps * ps
                ss_lo = jnp.sum(jnp.where(low_half, p2, 0.0), axis=-1, keepdims=True)
                ss_hi = jnp.sum(jnp.where(low_half, 0.0, p2), axis=-1, keepdims=True)
                ss = jnp.where(low_half, ss_lo, ss_hi)
                pn = ps * lax.rsqrt(ss * (1.0 / HEAD_DIM) + EPS) * g[:, sl * LANES:(sl + 1) * LANES]
                o_ref[c * slabs + sl, tok, :] = pn.astype(_BF16)

        waiting = None
        for j, (g_ref, o_ref) in enumerate(((gq_ref, q_ref), (gk_ref, k_ref))):
            for c in range(n_col_tiles):
                p = proj(wqk_ref, j, c)
                if waiting is not None:
                    finish_head_norm(*waiting)
                waiting = (p, g_ref, o_ref, c)

        for c in range(n_col_tiles):
            w_rows = slice(c * PROJ_TILE // 2, (c + 1) * PROJ_TILE // 2)
            vt = lax.dot_general(_unpack(wvt_ref[w_rows, :]), h, _NT_DIMS,
                                 preferred_element_type=_F32).astype(_BF16)
            if waiting is not None:
                finish_head_norm(*waiting)
                waiting = None
            for sl in range(slabs):
                vt_ref[c * slabs + sl, :, tok] = vt[sl * LANES:(sl + 1) * LANES, :]

    rows = tm // PROJ_SUB_TILES
    for n in range(PROJ_SUB_TILES):
        sub_tile(n * rows, rows)
    u_scr[0:CARRY_ROWS, :] = u_scr[tm:tm + CARRY_ROWS, :]


def _bias_tile(rb):
    kw, tq = KEY_TILES * Q_TILE, Q_TILE
    lane = lax.broadcasted_iota(jnp.int32, rb.shape, 1)
    top = rb[:, 2 * MAX_REL:2 * MAX_REL + 1]
    fwd = jnp.where(lane > 2 * MAX_REL, top, rb) * LOG2E
    rolled = pltpu.roll(jnp.broadcast_to(fwd, (kw, BIAS_LANES)), tq, axis=1,
                        stride=1, stride_axis=0)
    kj = lax.broadcasted_iota(jnp.int32, (kw, tq), 0)
    qi = lax.broadcasted_iota(jnp.int32, (kw, tq), 1)
    dchunk = (qi // CHUNK + N_PREV_CHUNKS) - kj // CHUNK
    band = (dchunk >= 0) & (dchunk <= N_PREV_CHUNKS)
    return jnp.where(band, rolled[:, :tq], NEG_INF)


def _attn_kernel(q_ref, *refs):
    k_refs, v_refs = refs[:KEY_REFS], refs[KEY_REFS:2 * KEY_REFS]
    rb_ref, o_ref, bias_ref = refs[2 * KEY_REFS:]
    tq = Q_TILE
    lane = lax.broadcasted_iota(jnp.int32, (tq, LANES), 1)
    ones_rows = jnp.ones((BF16_ROWS, tq), _BF16)

    @pl.when((pl.program_id(0) == 0) & (pl.program_id(1) == 0))
    def _():
        def build(head, carry):
            bias_ref[head] = _bias_tile(rb_ref[head])
            return carry
        lax.fori_loop(0, N_HEADS, build, 0)

    def scores(sub, tiles, head):
        lo, hi = tiles[0] * tq, (tiles[-1] + 1) * tq
        hp, hh = divmod(head, 2)
        sel = (lane < HEAD_DIM) if hh == 0 else (lane >= HEAD_DIM)
        q = q_ref[hp, sub * tq:(sub + 1) * tq, :]
        qh = jnp.where(sel, q, jnp.zeros_like(q))
        keys = jnp.concatenate([k_refs[sub + t][hp] for t in tiles], axis=0)
        s = lax.dot_general(keys, qh, _NT_DIMS, preferred_element_type=_F32)
        cols = []
        for c in range(tq // LANES):
            r0 = max(lo, c * LANES)
            r1 = min(hi, c * LANES + BAND_ROWS)
            cols.append((r0, r1, s[r0 - lo:r1 - lo, c * LANES:(c + 1) * LANES]
                         + bias_ref[head, r0:r1, c * LANES:(c + 1) * LANES]))
        return cols

    def softmax_numerators(tiles, cols):
        lo, hi = tiles[0] * tq, (tiles[-1] + 1) * tq
        p_cols = []
        for r0, r1, s in cols:
            m = jnp.max(s, axis=0, keepdims=True)
            p = jnp.exp2(s - m).astype(_BF16)
            pads = [jnp.zeros((n, LANES), _BF16) for n in (r0 - lo, hi - r1)]
            p_cols.append(jnp.concatenate(
                [blk for blk in (pads[0], p, pads[1]) if blk.shape[0]], axis=0))
        return jnp.concatenate(p_cols, axis=1)

    def weighted_values(sub, tiles, head, p):
        hp, hh = divmod(head, 2)
        v_ext = jnp.concatenate(
            [jnp.concatenate([v_refs[sub + t][hp, hh * HEAD_DIM:(hh + 1) * HEAD_DIM, :],
                              ones_rows], axis=0) for t in tiles], axis=1)
        acc = jnp.dot(v_ext, p, preferred_element_type=_F32)
        return acc[:HEAD_DIM] * (1.0 / acc[HEAD_DIM:HEAD_DIM + 1])

    def run(tiles_per_sub):
        work = [(sub, tiles, head) for sub, tiles in enumerate(tiles_per_sub)
                for head in range(N_HEADS)]
        scored = [scores(*w) for w in work[:SCORE_LOOKAHEAD]]
        numerators, halves = [], []
        for n in range(len(work) + VALUE_DELAY):
            if n + SCORE_LOOKAHEAD < len(work):
                scored.append(scores(*work[n + SCORE_LOOKAHEAD]))
            if n < len(work):
                numerators.append(softmax_numerators(work[n][1], scored.pop(0)))
            if n >= VALUE_DELAY:
                sub, tiles, head = work[n - VALUE_DELAY]
                halves.append(weighted_values(sub, tiles, head, numerators.pop(0)))
                if head % 2:
                    o_ref[head // 2, sub * tq:(sub + 1) * tq, :] = (
                        jnp.concatenate(halves, axis=0).T.astype(_BF16))
                    halves = []

    all_tiles = tuple(range(KEY_TILES))
    first_step = tuple(all_tiles[max(KEY_TILES - 1 - j, 0):] for j in range(Q_TILES_PER_STEP))
    qs = pl.program_id(1)
    pl.when(qs == 0)(functools.partial(run, first_step))
    pl.when(qs > 0)(functools.partial(run, (all_tiles,) * Q_TILES_PER_STEP))


def _mix_mlp_kernel(x_ref, a_ref, yc_ref, g1_ref, bgate_ref, g2_ref,
                    wg_hbm, wap_hbm, wcp_hbm, wout_hbm, wup_hbm, wdown_hbm, o_ref,
                    wg_ref, wap_ref, wcp_ref, wout_ref, wup_ref, wdown_ref, stage, sem):
    @pl.when(pl.program_id(0) == 0)
    def _():
        pairs = ((wg_hbm, wg_ref), (wap_hbm, wap_ref), (wcp_hbm, wcp_ref), (wout_hbm, wout_ref),
                 (wup_hbm, wup_ref), (wdown_hbm, wdown_ref))
        _stage_weights([job for src, dst in pairs for job in _weight_jobs(src, dst)], stage, sem)

    def dot(lhs, w_ref, rows=slice(None), cols=slice(None)):
        return jnp.dot(lhs, _unpack(w_ref[rows, cols]), preferred_element_type=_F32)

    sub_rows = x_ref.shape[0] // MLP_SUB_TILES
    subs = [slice(n * sub_rows, (n + 1) * sub_rows) for n in range(MLP_SUB_TILES)]

    def branches(rows):
        x = x_ref[rows, :]
        h = _rms_norm_rows(x, g1_ref[...]).astype(_BF16)
        gates = jax.nn.sigmoid(dot(h, wg_ref) + bgate_ref[...])
        a = jnp.concatenate([a_ref[hp, rows, :] for hp in range(HEAD_PAIRS)], axis=-1)
        ya = dot(a, wap_ref)
        yc = dot(yc_ref[rows, :], wcp_ref)
        return x, (gates[:, :D_MODEL] * ya + gates[:, D_MODEL:] * yc).astype(_BF16)

    def residual(x, merged):
        return x + dot(merged, wout_ref)

    def mlp(rows, x1):
        h2 = _rms_norm_rows(x1, g2_ref[...]).astype(_BF16)
        acc = x1
        for f in range(D_FF // FF_TILE):
            up = dot(h2, wup_ref, cols=slice(f * FF_TILE, (f + 1) * FF_TILE))
            act = jnp.square(jnp.maximum(up, 0.0)).astype(_BF16)
            acc = acc + dot(act, wdown_ref, rows=slice(f * FF_TILE // 2, (f + 1) * FF_TILE // 2))
        o_ref[rows, :] = acc

    merged = [branches(rows) for rows in subs]
    x1 = [residual(*m) for m in merged]
    for rows, v in zip(subs, x1):
        mlp(rows, v)


def kernel(x, norm1_g, w_in, q_norm_g, k_norm_g, rel_bias, conv_w, conv_b, w_attn_proj,
           w_conv_proj, w_gate, b_gate, w_out, norm2_g, w_up, w_down):
    b, s, d = x.shape
    assert d == D_MODEL and s % ROW_TILE == 0 and s % PROJ_ROWS == 0
    assert s % (Q_TILE * Q_TILES_PER_STEP) == 0
    t = b * s
    xf = x.reshape(t, d)
    row = lambda v: v.reshape(1, -1).astype(_F32)
    cparams = functools.partial(pltpu.CompilerParams, vmem_limit_bytes=VMEM_LIMIT)

    gq = row(jnp.tile(q_norm_g, N_HEADS) * (HEAD_DIM ** -0.5 * LOG2E))
    gk = row(jnp.tile(k_norm_g, N_HEADS))
    staging = [pltpu.VMEM((STAGE_SLOTS, STAGE_ROWS, STAGE_COLS), _F32),
               pltpu.SemaphoreType.DMA((STAGE_SLOTS,))]
    heads_shape = jax.ShapeDtypeStruct((HEAD_PAIRS, t, LANES), _BF16)

    def heads_spec(rows):
        return pl.BlockSpec((HEAD_PAIRS, rows, LANES), lambda i: (0, i, 0))

    q, k, vt, yc = pl.pallas_call(
        functools.partial(_proj_kernel, tiles_per_seq=s // PROJ_ROWS),
        grid=(t // PROJ_ROWS,),
        in_specs=[
            pl.BlockSpec((PROJ_ROWS, d), lambda i: (i, 0)),
            _resident((1, d)),
            pl.BlockSpec(memory_space=pl.ANY),
            _resident((1, d)),
            _resident((1, d)),
            _resident((CONV_WIDTH, d)),
            _resident((1, d)),
        ],
        out_specs=[heads_spec(PROJ_ROWS), heads_spec(PROJ_ROWS),
                   pl.BlockSpec((HEAD_PAIRS, LANES, PROJ_ROWS), lambda i: (0, 0, i)),
                   pl.BlockSpec((PROJ_ROWS, d), lambda i: (i, 0))],
        out_shape=[heads_shape, heads_shape,
                   jax.ShapeDtypeStruct((HEAD_PAIRS, LANES, t), _BF16),
                   jax.ShapeDtypeStruct((t, d), _BF16)],
        scratch_shapes=[pltpu.VMEM((PROJ_ROWS + CARRY_ROWS, d), _F32),
                        _packed_weight((d, 2 * d)), _packed_weight((d, d)),
                        _packed_weight((d, 3 * d))] + staging,
        compiler_params=cparams(dimension_semantics=("arbitrary",)),
        name="proj",
    )(xf, row(norm1_g), w_in.astype(_F32), gq, gk, conv_w.astype(_F32), row(conv_b))

    n_rel = rel_bias.shape[1]
    rb_rows = jnp.pad(rel_bias.astype(_F32), ((0, 0), (0, BIAS_LANES - n_rel)))
    nq = s // Q_TILE
    ns = nq // Q_TILES_PER_STEP
    q_spec = pl.BlockSpec((HEAD_PAIRS, Q_TILES_PER_STEP * Q_TILE, LANES),
                          lambda bi, si: (0, bi * ns + si, 0))

    def key_tile(si, ref_idx):
        return jnp.maximum(si * Q_TILES_PER_STEP - (KEY_TILES - 1) + ref_idx, 0)

    def k_spec(ref_idx):
        return pl.BlockSpec((HEAD_PAIRS, Q_TILE, LANES),
                            lambda bi, si: (0, bi * nq + key_tile(si, ref_idx), 0))

    def vt_spec(ref_idx):
        return pl.BlockSpec((HEAD_PAIRS, LANES, Q_TILE),
                            lambda bi, si: (0, 0, bi * nq + key_tile(si, ref_idx)))

    attn = pl.pallas_call(
        _attn_kernel,
        grid=(b, ns),
        in_specs=[q_spec] + [k_spec(r) for r in range(KEY_REFS)]
                 + [vt_spec(r) for r in range(KEY_REFS)]
                 + [_resident((N_HEADS, 1, BIAS_LANES))],
        out_specs=q_spec,
        out_shape=heads_shape,
        scratch_shapes=[pltpu.VMEM((N_HEADS, KEY_TILES * Q_TILE, Q_TILE), _F32)],
        compiler_params=cparams(dimension_semantics=("arbitrary", "arbitrary")),
        name="attn",
    )(q, *([k] * KEY_REFS), *([vt] * KEY_REFS), rb_rows.reshape(N_HEADS, 1, BIAS_LANES))

    mlp_weights = [w.astype(_F32) for w in (w_gate, w_attn_proj, w_conv_proj, w_out, w_up, w_down)]
    out = pl.pallas_call(
        _mix_mlp_kernel,
        grid=(t // ROW_TILE,),
        in_specs=[
            pl.BlockSpec((ROW_TILE, d), lambda i: (i, 0)),
            heads_spec(ROW_TILE),
            pl.BlockSpec((ROW_TILE, d), lambda i: (i, 0)),
            _resident((1, d)),
            _resident((1, 2 * d)),
            _resident((1, d)),
        ] + [pl.BlockSpec(memory_space=pl.ANY)] * len(mlp_weights),
        out_specs=pl.BlockSpec((ROW_TILE, d), lambda i: (i, 0)),
        out_shape=jax.ShapeDtypeStruct((t, d), _F32),
        scratch_shapes=[_packed_weight(w.shape) for w in mlp_weights] + staging,
        compiler_params=cparams(dimension_semantics=("arbitrary",)),
        name="mix_mlp",
    )(xf, attn, yc, row(norm1_g), row(b_gate), row(norm2_g), *mlp_weights)
    return out.reshape(b, s, d)
```

```python
import functools
import math

import jax
import jax.numpy as jnp
from jax import lax
from jax.experimental import pallas as pl
from jax.experimental.pallas import tpu as pltpu

D_MODEL = 1024
N_HEADS = 16
HEAD_DIM = 64
CHUNK = 64
N_PREV_CHUNKS = 8
MAX_REL = 256
CONV_WIDTH = 3
D_FF = 4 * D_MODEL
EPS = 1e-6
NEG_INF = -1e30
LOG2E = math.log2(math.e)

LANES = 128
BF16_ROWS = 16
HEAD_PAIRS = D_MODEL // LANES
MXU_TILE = 256

PROJ_ROWS = 512
ROW_TILE = 512
PROJ_TILE = 2 * MXU_TILE
CONV_TILE = MXU_TILE
Q_TILE = 256
KEY_TILES = 1 + (N_PREV_CHUNKS * CHUNK) // Q_TILE
Q_TILES_PER_STEP = 4
KEY_REFS = KEY_TILES + Q_TILES_PER_STEP - 1
BAND_ROWS = (N_PREV_CHUNKS + LANES // CHUNK) * CHUNK
BIAS_LANES = (KEY_TILES + 1) * Q_TILE
assert 2 * MAX_REL + 1 <= BIAS_LANES and Q_TILE - 1 <= MAX_REL
SCORE_LOOKAHEAD = 3
VALUE_DELAY = 0
FF_TILE = 1024
PROJ_SUB_TILES = 2
MLP_SUB_TILES = 2
CARRY_ROWS = 8
STAGE_ROWS, STAGE_COLS = 256, D_MODEL
STAGE_SLOTS = 4
VMEM_LIMIT = 56 * 1024 * 1024

_BF16 = jnp.bfloat16
_F32 = jnp.float32
_NT_DIMS = (((1,), (1,)), ((), ()))


def _resident(shape):
    return pl.BlockSpec(shape, lambda *_: (0,) * len(shape), pipeline_mode=pl.Buffered(1))


def _rms_norm_rows(x, g):
    ms = jnp.mean(x * x, axis=-1, keepdims=True)
    return x * lax.rsqrt(ms + EPS) * g


def _packed_weight(shape):
    k, n = shape
    return pltpu.VMEM((k // 2, n), jnp.uint32)


def _pack(block):
    return pltpu.bitcast(block.astype(_BF16), jnp.uint32)


def _unpack(words):
    return pltpu.bitcast(words, _BF16)


def _stage_weights(jobs, stage, sem):
    def copy(n):
        src, r0, c0, _ = jobs[n]
        return pltpu.make_async_copy(src.at[pl.ds(r0, STAGE_ROWS), pl.ds(c0, STAGE_COLS)],
                                     stage.at[n % STAGE_SLOTS], sem.at[n % STAGE_SLOTS])

    ahead = STAGE_SLOTS - 1
    for n in range(min(ahead, len(jobs))):
        copy(n).start(priority=n % 2)
    for n, job in enumerate(jobs):
        if n + ahead < len(jobs):
            copy(n + ahead).start(priority=(n + ahead) % 2)
        copy(n).wait()
        job[3](stage[n % STAGE_SLOTS])


def _weight_jobs(src, dst, row_lo=0, col_lo=0, shape=None):
    k, n = shape if shape is not None else src.shape
    jobs = []
    for r in range(0, k, STAGE_ROWS):
        for c in range(0, n, STAGE_COLS):
            def store(block, r=r, c=c):
                dst[r // 2:(r + STAGE_ROWS) // 2, c:c + STAGE_COLS] = _pack(block)
            jobs.append((src, row_lo + r, col_lo + c, store))
    return jobs


def _proj_kernel(x_ref, g1_ref, w_in_hbm, gq_ref, gk_ref, cw_ref, cb_ref,
                 q_ref, k_ref, vt_ref, yc_ref,
                 u_scr, wqk_ref, wvt_ref, wc_ref, stage, sem, *, tiles_per_seq):
    tm = x_ref.shape[0]

    @pl.when(pl.program_id(0) == 0)
    def _():
        def store_vt(block, r):
            wvt_ref[:, r:r + STAGE_ROWS] = _pack(block.T)

        jobs = _weight_jobs(w_in_hbm, wqk_ref, 0, 0, (D_MODEL, 2 * D_MODEL))
        jobs += _weight_jobs(w_in_hbm, wc_ref, 0, 3 * D_MODEL, (D_MODEL, 3 * D_MODEL))
        jobs += [(w_in_hbm, r, 2 * D_MODEL, functools.partial(store_vt, r=r))
                 for r in range(0, D_MODEL, STAGE_ROWS)]
        _stage_weights(jobs, stage, sem)

    n_col_tiles = D_MODEL // PROJ_TILE
    slabs = PROJ_TILE // LANES

    @pl.when(pl.program_id(0) % tiles_per_seq == 0)
    def _():
        u_scr[0:CARRY_ROWS, :] = jnp.zeros((CARRY_ROWS, D_MODEL), _F32)

    def sub_tile(r0, rows):
        tok = slice(r0, r0 + rows)
        h = _rms_norm_rows(x_ref[tok, :], g1_ref[...]).astype(_BF16)
        low_half = lax.broadcasted_iota(jnp.int32, (rows, LANES), 1) < HEAD_DIM

        def proj(w_ref, j, c, width=PROJ_TILE):
            lo = j * D_MODEL + c * width
            return jnp.dot(h, _unpack(w_ref[:, lo:lo + width]), preferred_element_type=_F32)

        for c in range(D_MODEL // CONV_TILE):
            cols = slice(c * CONV_TILE, (c + 1) * CONV_TILE)
            bg = proj(wc_ref, 0, c, CONV_TILE)
            u = proj(wc_ref, 1, c, CONV_TILE) * proj(wc_ref, 2, c, CONV_TILE)
            base = CARRY_ROWS + r0
            u_scr[base:base + rows, cols] = u
            u1 = u_scr[base - 1:base - 1 + rows, cols]
            u2 = u_scr[base - 2:base - 2 + rows, cols]
            conv = (cb_ref[:, cols] + cw_ref[0:1, cols] * u2 + cw_ref[1:2, cols] * u1
                    + cw_ref[2:3, cols] * u)
            yc_ref[tok, cols] = (bg * conv).astype(_BF16)

        def finish_head_norm(p, g_ref, o_ref, c):
            g = g_ref[:, c * PROJ_TILE:(c + 1) * PROJ_TILE]
            for sl in range(slabs):
                ps = p[:, sl * LANES:(sl + 1) * LANES]
                p2 = ps * ps
                ss_lo = jnp.sum(jnp.where(low_half, p2, 0.0), axis=-1, keepdims=True)
                ss_hi = jnp.sum(jnp.where(low_half, 0.0, p2), axis=-1, keepdims=True)
                ss = jnp.where(low_half, ss_lo, ss_hi)
                pn = ps * lax.rsqrt(ss * (1.0 / HEAD_DIM) + EPS) * g[:, sl * LANES:(sl + 1) * LANES]
                o_ref[c * slabs + sl, tok, :] = pn.astype(_BF16)

        waiting = None
        for j, (g_ref, o_ref) in enumerate(((gq_ref, q_ref), (gk_ref, k_ref))):
            for c in range(n_col_tiles):
                p = proj(wqk_ref, j, c)
                if waiting is not None:
                    finish_head_norm(*waiting)
                waiting = (p, g_ref, o_ref, c)

        for c in range(n_col_tiles):
            w_rows = slice(c * PROJ_TILE // 2, (c + 1) * PROJ_TILE // 2)
            vt = lax.dot_general(_unpack(wvt_ref[w_rows, :]), h, _NT_DIMS,
                                 preferred_element_type=_F32).astype(_BF16)
            if waiting is not None:
                finish_head_norm(*waiting)
                waiting = None
            for sl in range(slabs):
                vt_ref[c * slabs + sl, :, tok] = vt[sl * LANES:(sl + 1) * LANES, :]

    rows = tm // PROJ_SUB_TILES
    for n in range(PROJ_SUB_TILES):
        sub_tile(n * rows, rows)
    u_scr[0:CARRY_ROWS, :] = u_scr[tm:tm + CARRY_ROWS, :]


def _bias_tile(rb):
    kw, tq = KEY_TILES * Q_TILE, Q_TILE
    lane = lax.broadcasted_iota(jnp.int32, rb.shape, 1)
    top = rb[:, 2 * MAX_REL:2 * MAX_REL + 1]
    fwd = jnp.where(lane > 2 * MAX_REL, top, rb) * LOG2E
    rolled = pltpu.roll(jnp.broadcast_to(fwd, (kw, BIAS_LANES)), tq, axis=1,
                        stride=1, stride_axis=0)
    kj = lax.broadcasted_iota(jnp.int32, (kw, tq), 0)
    qi = lax.broadcasted_iota(jnp.int32, (kw, tq), 1)
    dchunk = (qi // CHUNK + N_PREV_CHUNKS) - kj // CHUNK
    band = (dchunk >= 0) & (dchunk <= N_PREV_CHUNKS)
    return jnp.where(band, rolled[:, :tq], NEG_INF)


def _attn_kernel(q_ref, *refs):
    k_refs, v_refs = refs[:KEY_REFS], refs[KEY_REFS:2 * KEY_REFS]
    rb_ref, o_ref, bias_ref = refs[2 * KEY_REFS:]
    tq = Q_TILE
    lane = lax.broadcasted_iota(jnp.int32, (tq, LANES), 1)
    ones_rows = jnp.ones((BF16_ROWS, tq), _BF16)

    @pl.when((pl.program_id(0) == 0) & (pl.program_id(1) == 0))
    def _():
        def build(head, carry):
            bias_ref[head] = _bias_tile(rb_ref[head])
            return carry
        lax.fori_loop(0, N_HEADS, build, 0)

    def scores(sub, tiles, head):
        lo, hi = tiles[0] * tq, (tiles[-1] + 1) * tq
        hp, hh = divmod(head, 2)
        sel = (lane < HEAD_DIM) if hh == 0 else (lane >= HEAD_DIM)
        q = q_ref[hp, sub * tq:(sub + 1) * tq, :]
        qh = jnp.where(sel, q, jnp.zeros_like(q))
        keys = jnp.concatenate([k_refs[sub + t][hp] for t in tiles], axis=0)
        s = lax.dot_general(keys, qh, _NT_DIMS, preferred_element_type=_F32)
        cols = []
        for c in range(tq // LANES):
            r0 = max(lo, c * LANES)
            r1 = min(hi, c * LANES + BAND_ROWS)
            cols.append((r0, r1, s[r0 - lo:r1 - lo, c * LANES:(c + 1) * LANES]
                         + bias_ref[head, r0:r1, c * LANES:(c + 1) * LANES]))
        return cols

    def softmax_numerators(tiles, cols):
        lo, hi = tiles[0] * tq, (tiles[-1] + 1) * tq
        p_cols = []
        for r0, r1, s in cols:
            m = jnp.max(s, axis=0, keepdims=True)
            p = jnp.exp2(s - m).astype(_BF16)
            pads = [jnp.zeros((n, LANES), _BF16) for n in (r0 - lo, hi - r1)]
            p_cols.append(jnp.concatenate(
                [blk for blk in (pads[0], p, pads[1]) if blk.shape[0]], axis=0))
        return jnp.concatenate(p_cols, axis=1)

    def weighted_values(sub, tiles, head, p):
        hp, hh = divmod(head, 2)
        v_ext = jnp.concatenate(
            [jnp.concatenate([v_refs[sub + t][hp, hh * HEAD_DIM:(hh + 1) * HEAD_DIM, :],
                              ones_rows], axis=0) for t in tiles], axis=1)
        acc = jnp.dot(v_ext, p, preferred_element_type=_F32)
        return acc[:HEAD_DIM] * (1.0 / acc[HEAD_DIM:HEAD_DIM + 1])

    def run(tiles_per_sub):
        work = [(sub, tiles, head) for sub, tiles in enumerate(tiles_per_sub)
                for head in range(N_HEADS)]
        scored = [scores(*w) for w in work[:SCORE_LOOKAHEAD]]
        numerators, halves = [], []
        for n in range(len(work) + VALUE_DELAY):
            if n + SCORE_LOOKAHEAD < len(work):
                scored.append(scores(*work[n + SCORE_LOOKAHEAD]))
            if n < len(work):
                numerators.append(softmax_numerators(work[n][1], scored.pop(0)))
            if n >= VALUE_DELAY:
                sub, tiles, head = work[n - VALUE_DELAY]
                halves.append(weighted_values(sub, tiles, head, numerators.pop(0)))
                if head % 2:
                    o_ref[head // 2, sub * tq:(sub + 1) * tq, :] = (
                        jnp.concatenate(halves, axis=0).T.astype(_BF16))
                    halves = []

    all_tiles = tuple(range(KEY_TILES))
    first_step = tuple(all_tiles[max(KEY_TILES - 1 - j, 0):] for j in range(Q_TILES_PER_STEP))
    qs = pl.program_id(1)
    pl.when(qs == 0)(functools.partial(run, first_step))
    pl.when(qs > 0)(functools.partial(run, (all_tiles,) * Q_TILES_PER_STEP))


def _mix_mlp_kernel(x_ref, a_ref, yc_ref, g1_ref, bgate_ref, g2_ref,
                    wg_hbm, wap_hbm, wcp_hbm, wout_hbm, wup_hbm, wdown_hbm, o_ref,
                    wg_ref, wap_ref, wcp_ref, wout_ref, wup_ref, wdown_ref, stage, sem):
    @pl.when(pl.program_id(0) == 0)
    def _():
        pairs = ((wg_hbm, wg_ref), (wap_hbm, wap_ref), (wcp_hbm, wcp_ref), (wout_hbm, wout_ref),
                 (wup_hbm, wup_ref), (wdown_hbm, wdown_ref))
        _stage_weights([job for src, dst in pairs for job in _weight_jobs(src, dst)], stage, sem)

    def dot(lhs, w_ref, rows=slice(None), cols=slice(None)):
        return jnp.dot(lhs, _unpack(w_ref[rows, cols]), preferred_element_type=_F32)

    sub_rows = x_ref.shape[0] // MLP_SUB_TILES
    subs = [slice(n * sub_rows, (n + 1) * sub_rows) for n in range(MLP_SUB_TILES)]

    def branches(rows):
        x = x_ref[rows, :]
        h = _rms_norm_rows(x, g1_ref[...]).astype(_BF16)
        gates = jax.nn.sigmoid(dot(h, wg_ref) + bgate_ref[...])
        a = jnp.concatenate([a_ref[hp, rows, :] for hp in range(HEAD_PAIRS)], axis=-1)
        ya = dot(a, wap_ref)
        yc = dot(yc_ref[rows, :], wcp_ref)
        return x, (gates[:, :D_MODEL] * ya + gates[:, D_MODEL:] * yc).astype(_BF16)

    def residual(x, merged):
        return x + dot(merged, wout_ref)

    def mlp(rows, x1):
        h2 = _rms_norm_rows(x1, g2_ref[...]).astype(_BF16)
        acc = x1
        for f in range(D_FF // FF_TILE):
            up = dot(h2, wup_ref, cols=slice(f * FF_TILE, (f + 1) * FF_TILE))
            act = jnp.square(jnp.maximum(up, 0.0)).astype(_BF16)
            acc = acc + dot(act, wdown_ref, rows=slice(f * FF_TILE // 2, (f + 1) * FF_TILE // 2))
        o_ref[rows, :] = acc

    merged = [branches(rows) for rows in subs]
    x1 = [residual(*m) for m in merged]
    for rows, v in zip(subs, x1):
        mlp(rows, v)


def kernel(x, norm1_g, w_in, q_norm_g, k_norm_g, rel_bias, conv_w, conv_b, w_attn_proj,
           w_conv_proj, w_gate, b_gate, w_out, norm2_g, w_up, w_down):
    b, s, d = x.shape
    assert d == D_MODEL and s % ROW_TILE == 0 and s % PROJ_ROWS == 0
    assert s % (Q_TILE * Q_TILES_PER_STEP) == 0
    t = b * s
    xf = x.reshape(t, d)
    row = lambda v: v.reshape(1, -1).astype(_F32)
    cparams = functools.partial(pltpu.CompilerParams, vmem_limit_bytes=VMEM_LIMIT)

    gq = row(jnp.tile(q_norm_g, N_HEADS) * (HEAD_DIM ** -0.5 * LOG2E))
    gk = row(jnp.tile(k_norm_g, N_HEADS))
    staging = [pltpu.VMEM((STAGE_SLOTS, STAGE_ROWS, STAGE_COLS), _F32),
               pltpu.SemaphoreType.DMA((STAGE_SLOTS,))]
    heads_shape = jax.ShapeDtypeStruct((HEAD_PAIRS, t, LANES), _BF16)

    def heads_spec(rows):
        return pl.BlockSpec((HEAD_PAIRS, rows, LANES), lambda i: (0, i, 0))

    q, k, vt, yc = pl.pallas_call(
        functools.partial(_proj_kernel, tiles_per_seq=s // PROJ_ROWS),
        grid=(t // PROJ_ROWS,),
        in_specs=[
            pl.BlockSpec((PROJ_ROWS, d), lambda i: (i, 0)),
            _resident((1, d)),
            pl.BlockSpec(memory_space=pl.ANY),
            _resident((1, d)),
            _resident((1, d)),
            _resident((CONV_WIDTH, d)),
            _resident((1, d)),
        ],
        out_specs=[heads_spec(PROJ_ROWS), heads_spec(PROJ_ROWS),
                   pl.BlockSpec((HEAD_PAIRS, LANES, PROJ_ROWS), lambda i: (0, 0, i)),
                   pl.BlockSpec((PROJ_ROWS, d), lambda i: (i, 0))],
        out_shape=[heads_shape, heads_shape,
                   jax.ShapeDtypeStruct((HEAD_PAIRS, LANES, t), _BF16),
                   jax.ShapeDtypeStruct((t, d), _BF16)],
        scratch_shapes=[pltpu.VMEM((PROJ_ROWS + CARRY_ROWS, d), _F32),
                        _packed_weight((d, 2 * d)), _packed_weight((d, d)),
                        _packed_weight((d, 3 * d))] + staging,
        compiler_params=cparams(dimension_semantics=("arbitrary",)),
        name="proj",
    )(xf, row(norm1_g), w_in.astype(_F32), gq, gk, conv_w.astype(_F32), row(conv_b))

    n_rel = rel_bias.shape[1]
    rb_rows = jnp.pad(rel_bias.astype(_F32), ((0, 0), (0, BIAS_LANES - n_rel)))
    nq = s // Q_TILE
    ns = nq // Q_TILES_PER_STEP
    q_spec = pl.BlockSpec((HEAD_PAIRS, Q_TILES_PER_STEP * Q_TILE, LANES),
                          lambda bi, si: (0, bi * ns + si, 0))

    def key_tile(si, ref_idx):
        return jnp.maximum(si * Q_TILES_PER_STEP - (KEY_TILES - 1) + ref_idx, 0)

    def k_spec(ref_idx):
        return pl.BlockSpec((HEAD_PAIRS, Q_TILE, LANES),
                            lambda bi, si: (0, bi * nq + key_tile(si, ref_idx), 0))

    def vt_spec(ref_idx):
        return pl.BlockSpec((HEAD_PAIRS, LANES, Q_TILE),
                            lambda bi, si: (0, 0, bi * nq + key_tile(si, ref_idx)))

    attn = pl.pallas_call(
        _attn_kernel,
        grid=(b, ns),
        in_specs=[q_spec] + [k_spec(r) for r in range(KEY_REFS)]
                 + [vt_spec(r) for r in range(KEY_REFS)]
                 + [_resident((N_HEADS, 1, BIAS_LANES))],
        out_specs=q_spec,
        out_shape=heads_shape,
        scratch_shapes=[pltpu.VMEM((N_HEADS, KEY_TILES * Q_TILE, Q_TILE), _F32)],
        compiler_params=cparams(dimension_semantics=("arbitrary", "arbitrary")),
        name="attn",
    )(q, *([k] * KEY_REFS), *([vt] * KEY_REFS), rb_rows.reshape(N_HEADS, 1, BIAS_LANES))

    mlp_weights = [w.astype(_F32) for w in (w_gate, w_attn_proj, w_conv_proj, w_out, w_up, w_down)]
    out = pl.pallas_call(
        _mix_mlp_kernel,
        grid=(t // ROW_TILE,),
        in_specs=[
            pl.BlockSpec((ROW_TILE, d), lambda i: (i, 0)),
            heads_spec(ROW_TILE),
            pl.BlockSpec((ROW_TILE, d), lambda i: (i, 0)),
            _resident((1, d)),
            _resident((1, 2 * d)),
            _resident((1, d)),
        ] + [pl.BlockSpec(memory_space=pl.ANY)] * len(mlp_weights),
        out_specs=pl.BlockSpec((ROW_TILE, d), lambda i: (i, 0)),
        out_shape=jax.ShapeDtypeStruct((t, d), _F32),
        scratch_shapes=[_packed_weight(w.shape) for w in mlp_weights] + staging,
        compiler_params=cparams(dimension_semantics=("arbitrary",)),
        name="mix_mlp",
    )(xf, attn, yc, row(norm1_g), row(b_gate), row(norm2_g), *mlp_weights)
    return out.reshape(b, s, d)
```

```python
import functools
import math

import jax
import jax.numpy as jnp
from jax import lax
from jax.experimental import pallas as pl
from jax.experimental.pallas import tpu as pltpu

D_MODEL = 1024
N_HEADS = 16
HEAD_DIM = 64
CHUNK = 64
N_PREV_CHUNKS = 8
MAX_REL = 256
CONV_WIDTH = 3
D_FF = 4 * D_MODEL
EPS = 1e-6
NEG_INF = -1e30
LOG2E = math.log2(math.e)

LANES = 128
BF16_ROWS = 16
HEAD_PAIRS = D_MODEL // LANES
MXU_TILE = 256

PROJ_ROWS = 512
ROW_TILE = 512
PROJ_TILE = 2 * MXU_TILE
CONV_TILE = MXU_TILE
Q_TILE = 256
KEY_TILES = 1 + (N_PREV_CHUNKS * CHUNK) // Q_TILE
Q_TILES_PER_STEP = 2
KEY_REFS = KEY_TILES + Q_TILES_PER_STEP - 1
BAND_ROWS = (N_PREV_CHUNKS + LANES // CHUNK) * CHUNK
BIAS_LANES = (KEY_TILES + 1) * Q_TILE
assert 2 * MAX_REL + 1 <= BIAS_LANES and Q_TILE - 1 <= MAX_REL
SCORE_LOOKAHEAD = 3
VALUE_DELAY = 0
FF_TILE = 1024
PROJ_SUB_TILES = 2
MLP_SUB_TILES = 2
CARRY_ROWS = 8
STAGE_ROWS, STAGE_COLS = 256, D_MODEL
STAGE_SLOTS = 4
VMEM_LIMIT = 56 * 1024 * 1024

_BF16 = jnp.bfloat16
_F32 = jnp.float32
_NT_DIMS = (((1,), (1,)), ((), ()))


def _resident(shape):
    return pl.BlockSpec(shape, lambda *_: (0,) * len(shape), pipeline_mode=pl.Buffered(1))


def _rms_norm_rows(x, g):
    ms = jnp.mean(x * x, axis=-1, keepdims=True)
    return x * lax.rsqrt(ms + EPS) * g


def _packed_weight(shape):
    k, n = shape
    return pltpu.VMEM((k // 2, n), jnp.uint32)


def _pack(block):
    return pltpu.bitcast(block.astype(_BF16), jnp.uint32)


def _unpack(words):
    return pltpu.bitcast(words, _BF16)


def _stage_weights(jobs, stage, sem):
    def copy(n):
        src, r0, c0, _ = jobs[n]
        return pltpu.make_async_copy(src.at[pl.ds(r0, STAGE_ROWS), pl.ds(c0, STAGE_COLS)],
                                     stage.at[n % STAGE_SLOTS], sem.at[n % STAGE_SLOTS])

    ahead = STAGE_SLOTS - 1
    for n in range(min(ahead, len(jobs))):
        copy(n).start(priority=n % 2)
    for n, job in enumerate(jobs):
        if n + ahead < len(jobs):
            copy(n + ahead).start(priority=(n + ahead) % 2)
        copy(n).wait()
        job[3](stage[n % STAGE_SLOTS])


def _weight_jobs(src, dst, row_lo=0, col_lo=0, shape=None):
    k, n = shape if shape is not None else src.shape
    jobs = []
    for r in range(0, k, STAGE_ROWS):
        for c in range(0, n, STAGE_COLS):
            def store(block, r=r, c=c):
                dst[r // 2:(r + STAGE_ROWS) // 2, c:c + STAGE_COLS] = _pack(block)
            jobs.append((src, row_lo + r, col_lo + c, store))
    return jobs


def _proj_kernel(x_ref, g1_ref, w_in_hbm, gq_ref, gk_ref, cw_ref, cb_ref,
                 q_ref, k_ref, vt_ref, yc_ref,
                 u_scr, wqk_ref, wvt_ref, wc_ref, stage, sem, *, tiles_per_seq):
    tm = x_ref.shape[0]

    @pl.when(pl.program_id(0) == 0)
    def _():
        def store_vt(block, r):
            wvt_ref[:, r:r + STAGE_ROWS] = _pack(block.T)

        jobs = _weight_jobs(w_in_hbm, wqk_ref, 0, 0, (D_MODEL, 2 * D_MODEL))
        jobs += _weight_jobs(w_in_hbm, wc_ref, 0, 3 * D_MODEL, (D_MODEL, 3 * D_MODEL))
        jobs += [(w_in_hbm, r, 2 * D_MODEL, functools.partial(store_vt, r=r))
                 for r in range(0, D_MODEL, STAGE_ROWS)]
        _stage_weights(jobs, stage, sem)

    n_col_tiles = D_MODEL // PROJ_TILE
    slabs = PROJ_TILE // LANES

    @pl.when(pl.program_id(0) % tiles_per_seq == 0)
    def _():
        u_scr[0:CARRY_ROWS, :] = jnp.zeros((CARRY_ROWS, D_MODEL), _F32)

    def sub_tile(r0, rows):
        tok = slice(r0, r0 + rows)
        h = _rms_norm_rows(x_ref[tok, :], g1_ref[...]).astype(_BF16)
        low_half = lax.broadcasted_iota(jnp.int32, (rows, LANES), 1) < HEAD_DIM

        def proj(w_ref, j, c, width=PROJ_TILE):
            lo = j * D_MODEL + c * width
            return jnp.dot(h, _unpack(w_ref[:, lo:lo + width]), preferred_element_type=_F32)

        for c in range(D_MODEL // CONV_TILE):
            cols = slice(c * CONV_TILE, (c + 1) * CONV_TILE)
            bg = proj(wc_ref, 0, c, CONV_TILE)
            u = proj(wc_ref, 1, c, CONV_TILE) * proj(wc_ref, 2, c, CONV_TILE)
            base = CARRY_ROWS + r0
            u_scr[base:base + rows, cols] = u
            u1 = u_scr[base - 1:base - 1 + rows, cols]
            u2 = u_scr[base - 2:base - 2 + rows, cols]
            conv = (cb_ref[:, cols] + cw_ref[0:1, cols] * u2 + cw_ref[1:2, cols] * u1
                    + cw_ref[2:3, cols] * u)
            yc_ref[tok, cols] = (bg * conv).astype(_BF16)

        def finish_head_norm(p, g_ref, o_ref, c):
            g = g_ref[:, c * PROJ_TILE:(c + 1) * PROJ_TILE]
            for sl in range(slabs):
                ps = p[:, sl * LANES:(sl + 1) * LANES]
                p2 = ps * ps
                ss_lo = jnp.sum(jnp.where(low_half, p2, 0.0), axis=-1, keepdims=True)
                ss_hi = jnp.sum(jnp.where(low_half, 0.0, p2), axis=-1, keepdims=True)
                ss = jnp.where(low_half, ss_lo, ss_hi)
                pn = ps * lax.rsqrt(ss * (1.0 / HEAD_DIM) + EPS) * g[:, sl * LANES:(sl + 1) * LANES]
                o_ref[c * slabs + sl, tok, :] = pn.astype(_BF16)

        waiting = None
        for j, (g_ref, o_ref) in enumerate(((gq_ref, q_ref), (gk_ref, k_ref))):
            for c in range(n_col_tiles):
                p = proj(wqk_ref, j, c)
                if waiting is not None:
                    finish_head_norm(*waiting)
                waiting = (p, g_ref, o_ref, c)

        for c in range(n_col_tiles):
            w_rows = slice(c * PROJ_TILE // 2, (c + 1) * PROJ_TILE // 2)
            vt = lax.dot_general(_unpack(wvt_ref[w_rows, :]), h, _NT_DIMS,
                                 preferred_element_type=_F32).astype(_BF16)
            if waiting is not None:
                finish_head_norm(*waiting)
                waiting = None
            for sl in range(slabs):
                for kt in range(rows // Q_TILE):
                    vt_ref[r0 // Q_TILE + kt, c * slabs + sl] = (
                        vt[sl * LANES:(sl + 1) * LANES, kt * Q_TILE:(kt + 1) * Q_TILE])

    rows = tm // PROJ_SUB_TILES
    for n in range(PROJ_SUB_TILES):
        sub_tile(n * rows, rows)
    u_scr[0:CARRY_ROWS, :] = u_scr[tm:tm + CARRY_ROWS, :]


def _bias_tile(rb):
    kw, tq = KEY_TILES * Q_TILE, Q_TILE
    lane = lax.broadcasted_iota(jnp.int32, rb.shape, 1)
    top = rb[:, 2 * MAX_REL:2 * MAX_REL + 1]
    fwd = jnp.where(lane > 2 * MAX_REL, top, rb) * LOG2E
    rolled = pltpu.roll(jnp.broadcast_to(fwd, (kw, BIAS_LANES)), tq, axis=1,
                        stride=1, stride_axis=0)
    kj = lax.broadcasted_iota(jnp.int32, (kw, tq), 0)
    qi = lax.broadcasted_iota(jnp.int32, (kw, tq), 1)
    dchunk = (qi // CHUNK + N_PREV_CHUNKS) - kj // CHUNK
    band = (dchunk >= 0) & (dchunk <= N_PREV_CHUNKS)
    return jnp.where(band, rolled[:, :tq], NEG_INF)


def _attn_kernel(q_ref, *refs):
    k_refs, v_refs = refs[:KEY_REFS], refs[KEY_REFS:2 * KEY_REFS]
    rb_ref, o_ref, bias_ref = refs[2 * KEY_REFS:]
    tq = Q_TILE
    lane = lax.broadcasted_iota(jnp.int32, (tq, LANES), 1)
    ones_rows = jnp.ones((BF16_ROWS, tq), _BF16)

    @pl.when((pl.program_id(0) == 0) & (pl.program_id(1) == 0))
    def _():
        def build(head, carry):
            bias_ref[head] = _bias_tile(rb_ref[head])
            return carry
        lax.fori_loop(0, N_HEADS, build, 0)

    def scores(sub, tiles, head):
        lo, hi = tiles[0] * tq, (tiles[-1] + 1) * tq
        hp, hh = divmod(head, 2)
        sel = (lane < HEAD_DIM) if hh == 0 else (lane >= HEAD_DIM)
        q = q_ref[hp, sub * tq:(sub + 1) * tq, :]
        qh = jnp.where(sel, q, jnp.zeros_like(q))
        keys = jnp.concatenate([k_refs[sub + t][hp] for t in tiles], axis=0)
        s = lax.dot_general(keys, qh, _NT_DIMS, preferred_element_type=_F32)
        cols = []
        for c in range(tq // LANES):
            r0 = max(lo, c * LANES)
            r1 = min(hi, c * LANES + BAND_ROWS)
            cols.append((r0, r1, s[r0 - lo:r1 - lo, c * LANES:(c + 1) * LANES]
                         + bias_ref[head, r0:r1, c * LANES:(c + 1) * LANES]))
        return cols

    def softmax_numerators(tiles, cols):
        lo, hi = tiles[0] * tq, (tiles[-1] + 1) * tq
        p_cols = []
        for r0, r1, s in cols:
            m = jnp.max(s, axis=0, keepdims=True)
            p = jnp.exp2(s - m).astype(_BF16)
            pads = [jnp.zeros((n, LANES), _BF16) for n in (r0 - lo, hi - r1)]
            p_cols.append(jnp.concatenate(
                [blk for blk in (pads[0], p, pads[1]) if blk.shape[0]], axis=0))
        return jnp.concatenate(p_cols, axis=1)

    def weighted_values(sub, tiles, head, p):
        hp, hh = divmod(head, 2)
        v_ext = jnp.concatenate(
            [jnp.concatenate([v_refs[sub + t][0, hp, hh * HEAD_DIM:(hh + 1) * HEAD_DIM, :],
                              ones_rows], axis=0) for t in tiles], axis=1)
        acc = jnp.dot(v_ext, p, preferred_element_type=_F32)
        return acc[:HEAD_DIM] * (1.0 / acc[HEAD_DIM:HEAD_DIM + 1])

    def run(tiles_per_sub):
        work = [(sub, tiles, head) for sub, tiles in enumerate(tiles_per_sub)
                for head in range(N_HEADS)]
        scored = [scores(*w) for w in work[:SCORE_LOOKAHEAD]]
        numerators, halves = [], []
        for n in range(len(work) + VALUE_DELAY):
            if n + SCORE_LOOKAHEAD < len(work):
                scored.append(scores(*work[n + SCORE_LOOKAHEAD]))
            if n < len(work):
                numerators.append(softmax_numerators(work[n][1], scored.pop(0)))
            if n >= VALUE_DELAY:
                sub, tiles, head = work[n - VALUE_DELAY]
                halves.append(weighted_values(sub, tiles, head, numerators.pop(0)))
                if head % 2:
                    o_ref[head // 2, sub * tq:(sub + 1) * tq, :] = (
                        jnp.concatenate(halves, axis=0).T.astype(_BF16))
                    halves = []

    all_tiles = tuple(range(KEY_TILES))
    first_step = tuple(all_tiles[max(KEY_TILES - 1 - j, 0):] for j in range(Q_TILES_PER_STEP))
    qs = pl.program_id(1)
    pl.when(qs == 0)(functools.partial(run, first_step))
    pl.when(qs > 0)(functools.partial(run, (all_tiles,) * Q_TILES_PER_STEP))


def _mix_mlp_kernel(x_ref, a_ref, yc_ref, g1_ref, bgate_ref, g2_ref,
                    wg_hbm, wap_hbm, wcp_hbm, wout_hbm, wup_hbm, wdown_hbm, o_ref,
                    wg_ref, wap_ref, wcp_ref, wout_ref, wup_ref, wdown_ref, stage, sem):
    @pl.when(pl.program_id(0) == 0)
    def _():
        pairs = ((wg_hbm, wg_ref), (wap_hbm, wap_ref), (wcp_hbm, wcp_ref), (wout_hbm, wout_ref),
                 (wup_hbm, wup_ref), (wdown_hbm, wdown_ref))
        _stage_weights([job for src, dst in pairs for job in _weight_jobs(src, dst)], stage, sem)

    def dot(lhs, w_ref, rows=slice(None), cols=slice(None)):
        return jnp.dot(lhs, _unpack(w_ref[rows, cols]), preferred_element_type=_F32)

    sub_rows = x_ref.shape[0] // MLP_SUB_TILES
    subs = [slice(n * sub_rows, (n + 1) * sub_rows) for n in range(MLP_SUB_TILES)]

    def branches(rows):
        x = x_ref[rows, :]
        h = _rms_norm_rows(x, g1_ref[...]).astype(_BF16)
        gates = jax.nn.sigmoid(dot(h, wg_ref) + bgate_ref[...])
        a = jnp.concatenate([a_ref[hp, rows, :] for hp in range(HEAD_PAIRS)], axis=-1)
        ya = dot(a, wap_ref)
        yc = dot(yc_ref[rows, :], wcp_ref)
        return x, (gates[:, :D_MODEL] * ya + gates[:, D_MODEL:] * yc).astype(_BF16)

    def residual(x, merged):
        return x + dot(merged, wout_ref)

    def mlp(rows, x1):
        h2 = _rms_norm_rows(x1, g2_ref[...]).astype(_BF16)
        acc = x1
        for f in range(D_FF // FF_TILE):
            up = dot(h2, wup_ref, cols=slice(f * FF_TILE, (f + 1) * FF_TILE))
            act = jnp.square(jnp.maximum(up, 0.0)).astype(_BF16)
            acc = acc + dot(act, wdown_ref, rows=slice(f * FF_TILE // 2, (f + 1) * FF_TILE // 2))
        o_ref[rows, :] = acc

    merged = [branches(rows) for rows in subs]
    x1 = [residual(*m) for m in merged]
    for rows, v in zip(subs, x1):
        mlp(rows, v)


def kernel(x, norm1_g, w_in, q_norm_g, k_norm_g, rel_bias, conv_w, conv_b, w_attn_proj,
           w_conv_proj, w_gate, b_gate, w_out, norm2_g, w_up, w_down):
    b, s, d = x.shape
    assert d == D_MODEL and s % ROW_TILE == 0 and s % PROJ_ROWS == 0
    assert s % (Q_TILE * Q_TILES_PER_STEP) == 0 and (PROJ_ROWS // PROJ_SUB_TILES) % Q_TILE == 0
    t = b * s
    xf = x.reshape(t, d)
    row = lambda v: v.reshape(1, -1).astype(_F32)
    cparams = functools.partial(pltpu.CompilerParams, vmem_limit_bytes=VMEM_LIMIT)

    gq = row(jnp.tile(q_norm_g, N_HEADS) * (HEAD_DIM ** -0.5 * LOG2E))
    gk = row(jnp.tile(k_norm_g, N_HEADS))
    staging = [pltpu.VMEM((STAGE_SLOTS, STAGE_ROWS, STAGE_COLS), _F32),
               pltpu.SemaphoreType.DMA((STAGE_SLOTS,))]
    heads_shape = jax.ShapeDtypeStruct((HEAD_PAIRS, t, LANES), _BF16)

    def heads_spec(rows):
        return pl.BlockSpec((HEAD_PAIRS, rows, LANES), lambda i: (0, i, 0))

    q, k, vt, yc = pl.pallas_call(
        functools.partial(_proj_kernel, tiles_per_seq=s // PROJ_ROWS),
        grid=(t // PROJ_ROWS,),
        in_specs=[
            pl.BlockSpec((PROJ_ROWS, d), lambda i: (i, 0)),
            _resident((1, d)),
            pl.BlockSpec(memory_space=pl.ANY),
            _resident((1, d)),
            _resident((1, d)),
            _resident((CONV_WIDTH, d)),
            _resident((1, d)),
        ],
        out_specs=[heads_spec(PROJ_ROWS), heads_spec(PROJ_ROWS),
                   pl.BlockSpec((PROJ_ROWS // Q_TILE, HEAD_PAIRS, LANES, Q_TILE),
                                lambda i: (i, 0, 0, 0)),
                   pl.BlockSpec((PROJ_ROWS, d), lambda i: (i, 0))],
        out_shape=[heads_shape, heads_shape,
                   jax.ShapeDtypeStruct((t // Q_TILE, HEAD_PAIRS, LANES, Q_TILE), _BF16),
                   jax.ShapeDtypeStruct((t, d), _BF16)],
        scratch_shapes=[pltpu.VMEM((PROJ_ROWS + CARRY_ROWS, d), _F32),
                        _packed_weight((d, 2 * d)), _packed_weight((d, d)),
                        _packed_weight((d, 3 * d))] + staging,
        compiler_params=cparams(dimension_semantics=("arbitrary",)),
        name="proj",
    )(xf, row(norm1_g), w_in.astype(_F32), gq, gk, conv_w.astype(_F32), row(conv_b))

    n_rel = rel_bias.shape[1]
    rb_rows = jnp.pad(rel_bias.astype(_F32), ((0, 0), (0, BIAS_LANES - n_rel)))
    nq = s // Q_TILE
    ns = nq // Q_TILES_PER_STEP
    q_spec = pl.BlockSpec((HEAD_PAIRS, Q_TILES_PER_STEP * Q_TILE, LANES),
                          lambda bi, si: (0, bi * ns + si, 0))

    def key_tile(si, ref_idx):
        return jnp.maximum(si * Q_TILES_PER_STEP - (KEY_TILES - 1) + ref_idx, 0)

    def k_spec(ref_idx):
        return pl.BlockSpec((HEAD_PAIRS, Q_TILE, LANES),
                            lambda bi, si: (0, bi * nq + key_tile(si, ref_idx), 0))

    def vt_spec(ref_idx):
        return pl.BlockSpec((1, HEAD_PAIRS, LANES, Q_TILE),
                            lambda bi, si: (bi * nq + key_tile(si, ref_idx), 0, 0, 0))

    attn = pl.pallas_call(
        _attn_kernel,
        grid=(b, ns),
        in_specs=[q_spec] + [k_spec(r) for r in range(KEY_REFS)]
                 + [vt_spec(r) for r in range(KEY_REFS)]
                 + [_resident((N_HEADS, 1, BIAS_LANES))],
        out_specs=q_spec,
        out_shape=heads_shape,
        scratch_shapes=[pltpu.VMEM((N_HEADS, KEY_TILES * Q_TILE, Q_TILE), _F32)],
        compiler_params=cparams(dimension_semantics=("arbitrary", "arbitrary")),
        name="attn",
    )(q, *([k] * KEY_REFS), *([vt] * KEY_REFS), rb_rows.reshape(N_HEADS, 1, BIAS_LANES))

    mlp_weights = [w.astype(_F32) for w in (w_gate, w_attn_proj, w_conv_proj, w_out, w_up, w_down)]
    out = pl.pallas_call(
        _mix_mlp_kernel,
        grid=(t // ROW_TILE,),
        in_specs=[
            pl.BlockSpec((ROW_TILE, d), lambda i: (i, 0)),
            heads_spec(ROW_TILE),
            pl.BlockSpec((ROW_TILE, d), lambda i: (i, 0)),
            _resident((1, d)),
            _resident((1, 2 * d)),
            _resident((1, d)),
        ] + [pl.BlockSpec(memory_space=pl.ANY)] * len(mlp_weights),
        out_specs=pl.BlockSpec((ROW_TILE, d), lambda i: (i, 0)),
        out_shape=jax.ShapeDtypeStruct((t, d), _F32),
        scratch_shapes=[_packed_weight(w.shape) for w in mlp_weights] + staging,
        compiler_params=cparams(dimension_semantics=("arbitrary",)),
        name="mix_mlp",
    )(xf, attn, yc, row(norm1_g), row(b_gate), row(norm2_g), *mlp_weights)
    return out.reshape(b, s, d)
```

```python
import functools
import math

import jax
import jax.numpy as jnp
from jax import lax
from jax.experimental import pallas as pl
from jax.experimental.pallas import tpu as pltpu

D_MODEL = 1024
N_HEADS = 16
HEAD_DIM = 64
CHUNK = 64
N_PREV_CHUNKS = 8
MAX_REL = 256
CONV_WIDTH = 3
D_FF = 4 * D_MODEL
EPS = 1e-6
NEG_INF = -1e30
LOG2E = math.log2(math.e)

LANES = 128
BF16_ROWS = 16
HEAD_PAIRS = D_MODEL // LANES
MXU_TILE = 256

PROJ_ROWS = 512
ROW_TILE = 512
PROJ_TILE = 2 * MXU_TILE
CONV_TILE = MXU_TILE
Q_TILE = 256
KEY_TILES = 1 + (N_PREV_CHUNKS * CHUNK) // Q_TILE
Q_TILES_PER_STEP = 2
KEY_REFS = KEY_TILES + Q_TILES_PER_STEP - 1
BAND_ROWS = (N_PREV_CHUNKS + LANES // CHUNK) * CHUNK
BIAS_LANES = (KEY_TILES + 1) * Q_TILE
assert 2 * MAX_REL + 1 <= BIAS_LANES and Q_TILE - 1 <= MAX_REL
SCORE_LOOKAHEAD = 3
VALUE_DELAY = 0
FF_TILE = 1024
PROJ_SUB_TILES = 2
MLP_SUB_TILES = 2
CARRY_ROWS = 8
STAGE_ROWS, STAGE_COLS = 256, D_MODEL
STAGE_SLOTS = 4
VMEM_LIMIT = 56 * 1024 * 1024

_BF16 = jnp.bfloat16
_F32 = jnp.float32
_NT_DIMS = (((1,), (1,)), ((), ()))


def _resident(shape):
    return pl.BlockSpec(shape, lambda *_: (0,) * len(shape), pipeline_mode=pl.Buffered(1))


def _rms_norm_rows(x, g):
    ms = jnp.mean(x * x, axis=-1, keepdims=True)
    return x * lax.rsqrt(ms + EPS) * g


def _packed_weight(shape):
    k, n = shape
    return pltpu.VMEM((k // 2, n), jnp.uint32)


def _pack(block):
    return pltpu.bitcast(block.astype(_BF16), jnp.uint32)


def _unpack(words):
    return pltpu.bitcast(words, _BF16)


def _stage_weights(jobs, stage, sem):
    def copy(n):
        src, r0, c0, _ = jobs[n]
        return pltpu.make_async_copy(src.at[pl.ds(r0, STAGE_ROWS), pl.ds(c0, STAGE_COLS)],
                                     stage.at[n % STAGE_SLOTS], sem.at[n % STAGE_SLOTS])

    ahead = STAGE_SLOTS - 1
    for n in range(min(ahead, len(jobs))):
        copy(n).start(priority=n % 2)
    for n, job in enumerate(jobs):
        if n + ahead < len(jobs):
            copy(n + ahead).start(priority=(n + ahead) % 2)
        copy(n).wait()
        job[3](stage[n % STAGE_SLOTS])


def _weight_jobs(src, dst, row_lo=0, col_lo=0, shape=None):
    k, n = shape if shape is not None else src.shape
    jobs = []
    for r in range(0, k, STAGE_ROWS):
        for c in range(0, n, STAGE_COLS):
            def store(block, r=r, c=c):
                dst[r // 2:(r + STAGE_ROWS) // 2, c:c + STAGE_COLS] = _pack(block)
            jobs.append((src, row_lo + r, col_lo + c, store))
    return jobs


def _proj_kernel(x_ref, g1_ref, w_in_hbm, gq_ref, gk_ref, cw_ref, cb_ref,
                 q_ref, k_ref, vt_ref, yc_ref,
                 u_scr, wqk_ref, wvt_ref, wc_ref, stage, sem, *, tiles_per_seq):
    tm = x_ref.shape[0]

    @pl.when(pl.program_id(0) == 0)
    def _():
        def store_vt(block, r):
            wvt_ref[:, r:r + STAGE_ROWS] = _pack(block.T)

        jobs = _weight_jobs(w_in_hbm, wqk_ref, 0, 0, (D_MODEL, 2 * D_MODEL))
        jobs += _weight_jobs(w_in_hbm, wc_ref, 0, 3 * D_MODEL, (D_MODEL, 3 * D_MODEL))
        jobs += [(w_in_hbm, r, 2 * D_MODEL, functools.partial(store_vt, r=r))
                 for r in range(0, D_MODEL, STAGE_ROWS)]
        _stage_weights(jobs, stage, sem)

    n_col_tiles = D_MODEL // PROJ_TILE
    slabs = PROJ_TILE // LANES

    @pl.when(pl.program_id(0) % tiles_per_seq == 0)
    def _():
        u_scr[0:CARRY_ROWS, :] = jnp.zeros((CARRY_ROWS, D_MODEL), _F32)

    def sub_tile(r0, rows):
        tok = slice(r0, r0 + rows)
        h = _rms_norm_rows(x_ref[tok, :], g1_ref[...]).astype(_BF16)
        low_half = lax.broadcasted_iota(jnp.int32, (rows, LANES), 1) < HEAD_DIM

        def proj(w_ref, j, c, width=PROJ_TILE):
            lo = j * D_MODEL + c * width
            return jnp.dot(h, _unpack(w_ref[:, lo:lo + width]), preferred_element_type=_F32)

        for c in range(D_MODEL // CONV_TILE):
            cols = slice(c * CONV_TILE, (c + 1) * CONV_TILE)
            bg = proj(wc_ref, 0, c, CONV_TILE)
            u = proj(wc_ref, 1, c, CONV_TILE) * proj(wc_ref, 2, c, CONV_TILE)
            base = CARRY_ROWS + r0
            u_scr[base:base + rows, cols] = u
            u1 = u_scr[base - 1:base - 1 + rows, cols]
            u2 = u_scr[base - 2:base - 2 + rows, cols]
            conv = (cb_ref[:, cols] + cw_ref[0:1, cols] * u2 + cw_ref[1:2, cols] * u1
                    + cw_ref[2:3, cols] * u)
            yc_ref[tok, cols] = (bg * conv).astype(_BF16)

        def finish_head_norm(p, g_ref, o_ref, c):
            g = g_ref[:, c * PROJ_TILE:(c + 1) * PROJ_TILE]
            for sl in range(slabs):
                ps = p[:, sl * LANES:(sl + 1) * LANES]
                p2 = ps * ps
                ss_lo = jnp.sum(jnp.where(low_half, p2, 0.0), axis=-1, keepdims=True)
                ss_hi = jnp.sum(jnp.where(low_half, 0.0, p2), axis=-1, keepdims=True)
                ss = jnp.where(low_half, ss_lo, ss_hi)
                pn = ps * lax.rsqrt(ss * (1.0 / HEAD_DIM) + EPS) * g[:, sl * LANES:(sl + 1) * LANES]
                o_ref[c * slabs + sl, tok, :] = pn.astype(_BF16)

        waiting = None
        for j, (g_ref, o_ref) in enumerate(((gq_ref, q_ref), (gk_ref, k_ref))):
            for c in range(n_col_tiles):
                p = proj(wqk_ref, j, c)
                if waiting is not None:
                    finish_head_norm(*waiting)
                waiting = (p, g_ref, o_ref, c)

        for c in range(n_col_tiles):
            w_rows = slice(c * PROJ_TILE // 2, (c + 1) * PROJ_TILE // 2)
            vt = lax.dot_general(_unpack(wvt_ref[w_rows, :]), h, _NT_DIMS,
                                 preferred_element_type=_F32).astype(_BF16)
            if waiting is not None:
                finish_head_norm(*waiting)
                waiting = None
            for sl in range(slabs):
                vt_ref[c * slabs + sl, :, tok] = vt[sl * LANES:(sl + 1) * LANES, :]

    rows = tm // PROJ_SUB_TILES
    for n in range(PROJ_SUB_TILES):
        sub_tile(n * rows, rows)
    u_scr[0:CARRY_ROWS, :] = u_scr[tm:tm + CARRY_ROWS, :]


def _bias_tile(rb):
    kw, tq = KEY_TILES * Q_TILE, Q_TILE
    lane = lax.broadcasted_iota(jnp.int32, rb.shape, 1)
    top = rb[:, 2 * MAX_REL:2 * MAX_REL + 1]
    fwd = jnp.where(lane > 2 * MAX_REL, top, rb) * LOG2E
    rolled = pltpu.roll(jnp.broadcast_to(fwd, (BAND_ROWS, BIAS_LANES)), tq, axis=1,
                        stride=1, stride_axis=0)
    kj = lax.broadcasted_iota(jnp.int32, (BAND_ROWS, LANES), 0)
    qi = lax.broadcasted_iota(jnp.int32, (BAND_ROWS, LANES), 1)
    dchunk = (qi // CHUNK + N_PREV_CHUNKS) - kj // CHUNK
    band = (dchunk >= 0) & (dchunk <= N_PREV_CHUNKS)
    return jnp.where(band, rolled[:, :LANES], NEG_INF)


def _attn_kernel(q_ref, *refs):
    k_refs, v_refs = refs[:KEY_REFS], refs[KEY_REFS:2 * KEY_REFS]
    rb_ref, o_ref, bias_ref = refs[2 * KEY_REFS:]
    tq = Q_TILE
    lane = lax.broadcasted_iota(jnp.int32, (tq, LANES), 1)
    ones_rows = jnp.ones((BF16_ROWS, tq), _BF16)

    @pl.when((pl.program_id(0) == 0) & (pl.program_id(1) == 0))
    def _():
        def build(head, carry):
            bias_ref[head] = _bias_tile(rb_ref[head])
            return carry
        lax.fori_loop(0, N_HEADS, build, 0)

    def scores(sub, tiles, head):
        lo, hi = tiles[0] * tq, (tiles[-1] + 1) * tq
        hp, hh = divmod(head, 2)
        sel = (lane < HEAD_DIM) if hh == 0 else (lane >= HEAD_DIM)
        q = q_ref[hp, sub * tq:(sub + 1) * tq, :]
        qh = jnp.where(sel, q, jnp.zeros_like(q))
        keys = jnp.concatenate([k_refs[sub + t][hp] for t in tiles], axis=0)
        s = lax.dot_general(keys, qh, _NT_DIMS, preferred_element_type=_F32)
        cols = []
        for c in range(tq // LANES):
            r0 = max(lo, c * LANES)
            r1 = min(hi, c * LANES + BAND_ROWS)
            cols.append((r0, r1, s[r0 - lo:r1 - lo, c * LANES:(c + 1) * LANES]
                         + bias_ref[head, r0 - c * LANES:r1 - c * LANES, :]))
        return cols

    def softmax_numerators(tiles, cols):
        lo, hi = tiles[0] * tq, (tiles[-1] + 1) * tq
        p_cols = []
        for r0, r1, s in cols:
            m = jnp.max(s, axis=0, keepdims=True)
            p = jnp.exp2(s - m).astype(_BF16)
            pads = [jnp.zeros((n, LANES), _BF16) for n in (r0 - lo, hi - r1)]
            p_cols.append(jnp.concatenate(
                [blk for blk in (pads[0], p, pads[1]) if blk.shape[0]], axis=0))
        return jnp.concatenate(p_cols, axis=1)

    def weighted_values(sub, tiles, head, p):
        hp, hh = divmod(head, 2)
        v_ext = jnp.concatenate(
            [jnp.concatenate([v_refs[sub + t][hp, hh * HEAD_DIM:(hh + 1) * HEAD_DIM, :],
                              ones_rows], axis=0) for t in tiles], axis=1)
        acc = jnp.dot(v_ext, p, preferred_element_type=_F32)
        return acc[:HEAD_DIM] * (1.0 / acc[HEAD_DIM:HEAD_DIM + 1])

    def run(tiles_per_sub):
        work = [(sub, tiles, head) for sub, tiles in enumerate(tiles_per_sub)
                for head in range(N_HEADS)]
        scored = [scores(*w) for w in work[:SCORE_LOOKAHEAD]]
        numerators, halves = [], []
        for n in range(len(work) + VALUE_DELAY):
            if n + SCORE_LOOKAHEAD < len(work):
                scored.append(scores(*work[n + SCORE_LOOKAHEAD]))
            if n < len(work):
                numerators.append(softmax_numerators(work[n][1], scored.pop(0)))
            if n >= VALUE_DELAY:
                sub, tiles, head = work[n - VALUE_DELAY]
                halves.append(weighted_values(sub, tiles, head, numerators.pop(0)))
                if head % 2:
                    o_ref[head // 2, sub * tq:(sub + 1) * tq, :] = (
                        jnp.concatenate(halves, axis=0).T.astype(_BF16))
                    halves = []

    all_tiles = tuple(range(KEY_TILES))
    first_step = tuple(all_tiles[max(KEY_TILES - 1 - j, 0):] for j in range(Q_TILES_PER_STEP))
    qs = pl.program_id(1)
    pl.when(qs == 0)(functools.partial(run, first_step))
    pl.when(qs > 0)(functools.partial(run, (all_tiles,) * Q_TILES_PER_STEP))


def _mix_mlp_kernel(x_ref, a_ref, yc_ref, g1_ref, bgate_ref, g2_ref,
                    wg_hbm, wap_hbm, wcp_hbm, wout_hbm, wup_hbm, wdown_hbm, o_ref,
                    wg_ref, wap_ref, wcp_ref, wout_ref, wup_ref, wdown_ref, stage, sem):
    @pl.when(pl.program_id(0) == 0)
    def _():
        pairs = ((wg_hbm, wg_ref), (wap_hbm, wap_ref), (wcp_hbm, wcp_ref), (wout_hbm, wout_ref),
                 (wup_hbm, wup_ref), (wdown_hbm, wdown_ref))
        _stage_weights([job for src, dst in pairs for job in _weight_jobs(src, dst)], stage, sem)

    def dot(lhs, w_ref, rows=slice(None), cols=slice(None)):
        return jnp.dot(lhs, _unpack(w_ref[rows, cols]), preferred_element_type=_F32)

    sub_rows = x_ref.shape[0] // MLP_SUB_TILES
    subs = [slice(n * sub_rows, (n + 1) * sub_rows) for n in range(MLP_SUB_TILES)]

    def branches(rows):
        x = x_ref[rows, :]
        h = _rms_norm_rows(x, g1_ref[...]).astype(_BF16)
        gates = jax.nn.sigmoid(dot(h, wg_ref) + bgate_ref[...])
        a = jnp.concatenate([a_ref[hp, rows, :] for hp in range(HEAD_PAIRS)], axis=-1)
        ya = dot(a, wap_ref)
        yc = dot(yc_ref[rows, :], wcp_ref)
        return x, (gates[:, :D_MODEL] * ya + gates[:, D_MODEL:] * yc).astype(_BF16)

    def residual(x, merged):
        return x + dot(merged, wout_ref)

    def mlp(rows, x1):
        h2 = _rms_norm_rows(x1, g2_ref[...]).astype(_BF16)
        acc = x1
        for f in range(D_FF // FF_TILE):
            up = dot(h2, wup_ref, cols=slice(f * FF_TILE, (f + 1) * FF_TILE))
            act = jnp.square(jnp.maximum(up, 0.0)).astype(_BF16)
            acc = acc + dot(act, wdown_ref, rows=slice(f * FF_TILE // 2, (f + 1) * FF_TILE // 2))
        o_ref[rows, :] = acc

    merged = [branches(rows) for rows in subs]
    x1 = [residual(*m) for m in merged]
    for rows, v in zip(subs, x1):
        mlp(rows, v)


def kernel(x, norm1_g, w_in, q_norm_g, k_norm_g, rel_bias, conv_w, conv_b, w_attn_proj,
           w_conv_proj, w_gate, b_gate, w_out, norm2_g, w_up, w_down):
    b, s, d = x.shape
    assert d == D_MODEL and s % ROW_TILE == 0 and s % PROJ_ROWS == 0
    assert s % (Q_TILE * Q_TILES_PER_STEP) == 0
    t = b * s
    xf = x.reshape(t, d)
    row = lambda v: v.reshape(1, -1).astype(_F32)
    cparams = functools.partial(pltpu.CompilerParams, vmem_limit_bytes=VMEM_LIMIT)

    gq = row(jnp.tile(q_norm_g, N_HEADS) * (HEAD_DIM ** -0.5 * LOG2E))
    gk = row(jnp.tile(k_norm_g, N_HEADS))
    staging = [pltpu.VMEM((STAGE_SLOTS, STAGE_ROWS, STAGE_COLS), _F32),
               pltpu.SemaphoreType.DMA((STAGE_SLOTS,))]
    heads_shape = jax.ShapeDtypeStruct((HEAD_PAIRS, t, LANES), _BF16)

    def heads_spec(rows):
        return pl.BlockSpec((HEAD_PAIRS, rows, LANES), lambda i: (0, i, 0))

    q, k, vt, yc = pl.pallas_call(
        functools.partial(_proj_kernel, tiles_per_seq=s // PROJ_ROWS),
        grid=(t // PROJ_ROWS,),
        in_specs=[
            pl.BlockSpec((PROJ_ROWS, d), lambda i: (i, 0)),
            _resident((1, d)),
            pl.BlockSpec(memory_space=pl.ANY),
            _resident((1, d)),
            _resident((1, d)),
            _resident((CONV_WIDTH, d)),
            _resident((1, d)),
        ],
        out_specs=[heads_spec(PROJ_ROWS), heads_spec(PROJ_ROWS),
                   pl.BlockSpec((HEAD_PAIRS, LANES, PROJ_ROWS), lambda i: (0, 0, i)),
                   pl.BlockSpec((PROJ_ROWS, d), lambda i: (i, 0))],
        out_shape=[heads_shape, heads_shape,
                   jax.ShapeDtypeStruct((HEAD_PAIRS, LANES, t), _BF16),
                   jax.ShapeDtypeStruct((t, d), _BF16)],
        scratch_shapes=[pltpu.VMEM((PROJ_ROWS + CARRY_ROWS, d), _F32),
                        _packed_weight((d, 2 * d)), _packed_weight((d, d)),
                        _packed_weight((d, 3 * d))] + staging,
        compiler_params=cparams(dimension_semantics=("arbitrary",)),
        name="proj",
    )(xf, row(norm1_g), w_in.astype(_F32), gq, gk, conv_w.astype(_F32), row(conv_b))

    n_rel = rel_bias.shape[1]
    rb_rows = jnp.pad(rel_bias.astype(_F32), ((0, 0), (0, BIAS_LANES - n_rel)))
    nq = s // Q_TILE
    ns = nq // Q_TILES_PER_STEP
    q_spec = pl.BlockSpec((HEAD_PAIRS, Q_TILES_PER_STEP * Q_TILE, LANES),
                          lambda bi, si: (0, bi * ns + si, 0))

    def key_tile(si, ref_idx):
        return jnp.maximum(si * Q_TILES_PER_STEP - (KEY_TILES - 1) + ref_idx, 0)

    def k_spec(ref_idx):
        return pl.BlockSpec((HEAD_PAIRS, Q_TILE, LANES),
                            lambda bi, si: (0, bi * nq + key_tile(si, ref_idx), 0))

    def vt_spec(ref_idx):
        return pl.BlockSpec((HEAD_PAIRS, LANES, Q_TILE),
                            lambda bi, si: (0, 0, bi * nq + key_tile(si, ref_idx)))

    attn = pl.pallas_call(
        _attn_kernel,
        grid=(b, ns),
        in_specs=[q_spec] + [k_spec(r) for r in range(KEY_REFS)]
                 + [vt_spec(r) for r in range(KEY_REFS)]
                 + [_resident((N_HEADS, 1, BIAS_LANES))],
        out_specs=q_spec,
        out_shape=heads_shape,
        scratch_shapes=[pltpu.VMEM((N_HEADS, BAND_ROWS, LANES), _F32)],
        compiler_params=cparams(dimension_semantics=("arbitrary", "arbitrary")),
        name="attn",
    )(q, *([k] * KEY_REFS), *([vt] * KEY_REFS), rb_rows.reshape(N_HEADS, 1, BIAS_LANES))

    mlp_weights = [w.astype(_F32) for w in (w_gate, w_attn_proj, w_conv_proj, w_out, w_up, w_down)]
    out = pl.pallas_call(
        _mix_mlp_kernel,
        grid=(t // ROW_TILE,),
        in_specs=[
            pl.BlockSpec((ROW_TILE, d), lambda i: (i, 0)),
            heads_spec(ROW_TILE),
            pl.BlockSpec((ROW_TILE, d), lambda i: (i, 0)),
            _resident((1, d)),
            _resident((1, 2 * d)),
            _resident((1, d)),
        ] + [pl.BlockSpec(memory_space=pl.ANY)] * len(mlp_weights),
        out_specs=pl.BlockSpec((ROW_TILE, d), lambda i: (i, 0)),
        out_shape=jax.ShapeDtypeStruct((t, d), _F32),
        scratch_shapes=[_packed_weight(w.shape) for w in mlp_weights] + staging,
        compiler_params=cparams(dimension_semantics=("arbitrary",)),
        name="mix_mlp",
    )(xf, attn, yc, row(norm1_g), row(b_gate), row(norm2_g), *mlp_weights)
    return out.reshape(b, s, d)
```

```python
import functools
import math

import jax
import jax.numpy as jnp
from jax import lax
from jax.experimental import pallas as pl
from jax.experimental.pallas import tpu as pltpu

D_MODEL = 1024
N_HEADS = 16
HEAD_DIM = 64
CHUNK = 64
N_PREV_CHUNKS = 8
MAX_REL = 256
CONV_WIDTH = 3
D_FF = 4 * D_MODEL
EPS = 1e-6
NEG_INF = -1e30
LOG2E = math.log2(math.e)

LANES = 128
BF16_ROWS = 16
HEAD_PAIRS = D_MODEL // LANES
MXU_TILE = 256

PROJ_ROWS = 512
ROW_TILE = 512
PROJ_TILE = 2 * MXU_TILE
CONV_TILE = MXU_TILE
Q_TILE = 256
KEY_TILES = 1 + (N_PREV_CHUNKS * CHUNK) // Q_TILE
Q_TILES_PER_STEP = 2
KEY_REFS = KEY_TILES + Q_TILES_PER_STEP - 1
BAND_ROWS = (N_PREV_CHUNKS + LANES // CHUNK) * CHUNK
BIAS_LANES = (KEY_TILES + 1) * Q_TILE
assert 2 * MAX_REL + 1 <= BIAS_LANES and Q_TILE - 1 <= MAX_REL
SCORE_LOOKAHEAD = 3
KEY_BLOCK = LANES
FF_TILE = 1024
PROJ_SUB_TILES = 2
MLP_SUB_TILES = 2
CARRY_ROWS = 8
STAGE_ROWS, STAGE_COLS = 256, D_MODEL
STAGE_SLOTS = 4
VMEM_LIMIT = 56 * 1024 * 1024

_BF16 = jnp.bfloat16
_F32 = jnp.float32
_NT_DIMS = (((1,), (1,)), ((), ()))


def _resident(shape):
    return pl.BlockSpec(shape, lambda *_: (0,) * len(shape), pipeline_mode=pl.Buffered(1))


def _rms_norm_rows(x, g):
    ms = jnp.mean(x * x, axis=-1, keepdims=True)
    return x * lax.rsqrt(ms + EPS) * g


def _packed_weight(shape):
    k, n = shape
    return pltpu.VMEM((k // 2, n), jnp.uint32)


def _pack(block):
    return pltpu.bitcast(block.astype(_BF16), jnp.uint32)


def _unpack(words):
    return pltpu.bitcast(words, _BF16)


def _stage_weights(jobs, stage, sem):
    def copy(n):
        src, r0, c0, _ = jobs[n]
        return pltpu.make_async_copy(src.at[pl.ds(r0, STAGE_ROWS), pl.ds(c0, STAGE_COLS)],
                                     stage.at[n % STAGE_SLOTS], sem.at[n % STAGE_SLOTS])

    ahead = STAGE_SLOTS - 1
    for n in range(min(ahead, len(jobs))):
        copy(n).start(priority=n % 2)
    for n, job in enumerate(jobs):
        if n + ahead < len(jobs):
            copy(n + ahead).start(priority=(n + ahead) % 2)
        copy(n).wait()
        job[3](stage[n % STAGE_SLOTS])


def _weight_jobs(src, dst, row_lo=0, col_lo=0, shape=None):
    k, n = shape if shape is not None else src.shape
    jobs = []
    for r in range(0, k, STAGE_ROWS):
        for c in range(0, n, STAGE_COLS):
            def store(block, r=r, c=c):
                dst[r // 2:(r + STAGE_ROWS) // 2, c:c + STAGE_COLS] = _pack(block)
            jobs.append((src, row_lo + r, col_lo + c, store))
    return jobs


def _proj_kernel(x_ref, g1_ref, w_in_hbm, gq_ref, gk_ref, cw_ref, cb_ref,
                 q_ref, k_ref, vt_ref, yc_ref,
                 u_scr, wqk_ref, wvt_ref, wc_ref, stage, sem, *, tiles_per_seq):
    tm = x_ref.shape[0]

    @pl.when(pl.program_id(0) == 0)
    def _():
        def store_vt(block, r):
            wvt_ref[:, r:r + STAGE_ROWS] = _pack(block.T)

        jobs = _weight_jobs(w_in_hbm, wqk_ref, 0, 0, (D_MODEL, 2 * D_MODEL))
        jobs += _weight_jobs(w_in_hbm, wc_ref, 0, 3 * D_MODEL, (D_MODEL, 3 * D_MODEL))
        jobs += [(w_in_hbm, r, 2 * D_MODEL, functools.partial(store_vt, r=r))
                 for r in range(0, D_MODEL, STAGE_ROWS)]
        _stage_weights(jobs, stage, sem)

    n_col_tiles = D_MODEL // PROJ_TILE
    slabs = PROJ_TILE // LANES

    @pl.when(pl.program_id(0) % tiles_per_seq == 0)
    def _():
        u_scr[0:CARRY_ROWS, :] = jnp.zeros((CARRY_ROWS, D_MODEL), _F32)

    def sub_tile(r0, rows):
        tok = slice(r0, r0 + rows)
        h = _rms_norm_rows(x_ref[tok, :], g1_ref[...]).astype(_BF16)
        low_half = lax.broadcasted_iota(jnp.int32, (rows, LANES), 1) < HEAD_DIM

        def proj(w_ref, j, c, width=PROJ_TILE):
            lo = j * D_MODEL + c * width
            return jnp.dot(h, _unpack(w_ref[:, lo:lo + width]), preferred_element_type=_F32)

        for c in range(D_MODEL // CONV_TILE):
            cols = slice(c * CONV_TILE, (c + 1) * CONV_TILE)
            bg = proj(wc_ref, 0, c, CONV_TILE)
            u = proj(wc_ref, 1, c, CONV_TILE) * proj(wc_ref, 2, c, CONV_TILE)
            base = CARRY_ROWS + r0
            u_scr[base:base + rows, cols] = u
            u1 = u_scr[base - 1:base - 1 + rows, cols]
            u2 = u_scr[base - 2:base - 2 + rows, cols]
            conv = (cb_ref[:, cols] + cw_ref[0:1, cols] * u2 + cw_ref[1:2, cols] * u1
                    + cw_ref[2:3, cols] * u)
            yc_ref[tok, cols] = (bg * conv).astype(_BF16)

        def finish_head_norm(p, g_ref, o_ref, c):
            g = g_ref[:, c * PROJ_TILE:(c + 1) * PROJ_TILE]
            for sl in range(slabs):
                ps = p[:, sl * LANES:(sl + 1) * LANES]
                p2 = ps * ps
                ss_lo = jnp.sum(jnp.where(low_half, p2, 0.0), axis=-1, keepdims=True)
                ss_hi = jnp.sum(jnp.where(low_half, 0.0, p2), axis=-1, keepdims=True)
                ss = jnp.where(low_half, ss_lo, ss_hi)
                pn = ps * lax.rsqrt(ss * (1.0 / HEAD_DIM) + EPS) * g[:, sl * LANES:(sl + 1) * LANES]
                o_ref[c * slabs + sl, tok, :] = pn.astype(_BF16)

        waiting = None
        for j, (g_ref, o_ref) in enumerate(((gq_ref, q_ref), (gk_ref, k_ref))):
            for c in range(n_col_tiles):
                p = proj(wqk_ref, j, c)
                if waiting is not None:
                    finish_head_norm(*waiting)
                waiting = (p, g_ref, o_ref, c)

        for c in range(n_col_tiles):
            w_rows = slice(c * PROJ_TILE // 2, (c + 1) * PROJ_TILE // 2)
            vt = lax.dot_general(_unpack(wvt_ref[w_rows, :]), h, _NT_DIMS,
                                 preferred_element_type=_F32).astype(_BF16)
            if waiting is not None:
                finish_head_norm(*waiting)
                waiting = None
            for sl in range(slabs):
                vt_ref[c * slabs + sl, :, tok] = vt[sl * LANES:(sl + 1) * LANES, :]

    rows = tm // PROJ_SUB_TILES
    for n in range(PROJ_SUB_TILES):
        sub_tile(n * rows, rows)
    u_scr[0:CARRY_ROWS, :] = u_scr[tm:tm + CARRY_ROWS, :]


def _bias_tile(rb):
    kw, tq = KEY_TILES * Q_TILE, Q_TILE
    lane = lax.broadcasted_iota(jnp.int32, rb.shape, 1)
    top = rb[:, 2 * MAX_REL:2 * MAX_REL + 1]
    fwd = jnp.where(lane > 2 * MAX_REL, top, rb) * LOG2E
    rolled = pltpu.roll(jnp.broadcast_to(fwd, (kw, BIAS_LANES)), tq, axis=1,
                        stride=1, stride_axis=0)
    kj = lax.broadcasted_iota(jnp.int32, (kw, tq), 0)
    qi = lax.broadcasted_iota(jnp.int32, (kw, tq), 1)
    dchunk = (qi // CHUNK + N_PREV_CHUNKS) - kj // CHUNK
    band = (dchunk >= 0) & (dchunk <= N_PREV_CHUNKS)
    return jnp.where(band, rolled[:, :tq], NEG_INF)


def _attn_kernel(q_ref, *refs):
    k_refs, v_refs = refs[:KEY_REFS], refs[KEY_REFS:2 * KEY_REFS]
    rb_ref, o_ref, bias_ref = refs[2 * KEY_REFS:]
    tq = Q_TILE
    lane = lax.broadcasted_iota(jnp.int32, (tq, LANES), 1)
    ones_rows = jnp.ones((BF16_ROWS, tq), _BF16)

    @pl.when((pl.program_id(0) == 0) & (pl.program_id(1) == 0))
    def _():
        def build(head, carry):
            bias_ref[head] = _bias_tile(rb_ref[head])
            return carry
        lax.fori_loop(0, N_HEADS, build, 0)

    def scores(sub, tiles, head):
        lo, hi = tiles[0] * tq, (tiles[-1] + 1) * tq
        hp, hh = divmod(head, 2)
        sel = (lane < HEAD_DIM) if hh == 0 else (lane >= HEAD_DIM)
        q = q_ref[hp, sub * tq:(sub + 1) * tq, :]
        qh = jnp.where(sel, q, jnp.zeros_like(q))
        keys = jnp.concatenate([k_refs[sub + t][hp] for t in tiles], axis=0)
        return lax.dot_general(keys, qh, _NT_DIMS, preferred_element_type=_F32)

    def attend(sub, tiles, head, s):
        lo, hi = tiles[0] * tq, (tiles[-1] + 1) * tq
        hp, hh = divmod(head, 2)
        n_cols = tq // LANES
        m = [None] * n_cols
        acc = None
        for b0 in range(lo, hi, KEY_BLOCK):
            p_parts, alpha_parts = [], []
            for c in range(n_cols):
                if not (max(lo, c * LANES) <= b0 < min(hi, c * LANES + BAND_ROWS)):
                    p_parts.append(jnp.zeros((KEY_BLOCK, LANES), _BF16))
                    alpha_parts.append(jnp.ones((1, LANES), _F32))
                    continue
                sb = (s[b0 - lo:b0 - lo + KEY_BLOCK, c * LANES:(c + 1) * LANES]
                      + bias_ref[head, b0:b0 + KEY_BLOCK, c * LANES:(c + 1) * LANES])
                mb = jnp.max(sb, axis=0, keepdims=True)
                if m[c] is None:
                    m_new = mb
                    alpha_parts.append(jnp.ones((1, LANES), _F32))
                else:
                    m_new = jnp.maximum(m[c], mb)
                    alpha_parts.append(jnp.exp2(m[c] - m_new))
                m[c] = m_new
                p_parts.append(jnp.exp2(sb - m_new).astype(_BF16))
            t, off = divmod(b0, tq)
            v_ext = jnp.concatenate(
                [v_refs[sub + t][hp, hh * HEAD_DIM:(hh + 1) * HEAD_DIM, off:off + KEY_BLOCK],
                 ones_rows[:, :KEY_BLOCK]], axis=0)
            part = jnp.dot(v_ext, jnp.concatenate(p_parts, axis=1),
                           preferred_element_type=_F32)
            acc = part if acc is None else acc * jnp.concatenate(alpha_parts, axis=1) + part
        return acc[:HEAD_DIM] * (1.0 / acc[HEAD_DIM:HEAD_DIM + 1])

    def run(tiles_per_sub):
        work = [(sub, tiles, head) for sub, tiles in enumerate(tiles_per_sub)
                for head in range(N_HEADS)]
        scored = [scores(*w) for w in work[:SCORE_LOOKAHEAD]]
        halves = []
        for n, (sub, tiles, head) in enumerate(work):
            if n + SCORE_LOOKAHEAD < len(work):
                scored.append(scores(*work[n + SCORE_LOOKAHEAD]))
            halves.append(attend(sub, tiles, head, scored.pop(0)))
            if head % 2:
                o_ref[head // 2, sub * tq:(sub + 1) * tq, :] = (
                    jnp.concatenate(halves, axis=0).T.astype(_BF16))
                halves = []

    all_tiles = tuple(range(KEY_TILES))
    first_step = tuple(all_tiles[max(KEY_TILES - 1 - j, 0):] for j in range(Q_TILES_PER_STEP))
    qs = pl.program_id(1)
    pl.when(qs == 0)(functools.partial(run, first_step))
    pl.when(qs > 0)(functools.partial(run, (all_tiles,) * Q_TILES_PER_STEP))


def _mix_mlp_kernel(x_ref, a_ref, yc_ref, g1_ref, bgate_ref, g2_ref,
                    wg_hbm, wap_hbm, wcp_hbm, wout_hbm, wup_hbm, wdown_hbm, o_ref,
                    wg_ref, wap_ref, wcp_ref, wout_ref, wup_ref, wdown_ref, stage, sem):
    @pl.when(pl.program_id(0) == 0)
    def _():
        pairs = ((wg_hbm, wg_ref), (wap_hbm, wap_ref), (wcp_hbm, wcp_ref), (wout_hbm, wout_ref),
                 (wup_hbm, wup_ref), (wdown_hbm, wdown_ref))
        _stage_weights([job for src, dst in pairs for job in _weight_jobs(src, dst)], stage, sem)

    def dot(lhs, w_ref, rows=slice(None), cols=slice(None)):
        return jnp.dot(lhs, _unpack(w_ref[rows, cols]), preferred_element_type=_F32)

    sub_rows = x_ref.shape[0] // MLP_SUB_TILES
    subs = [slice(n * sub_rows, (n + 1) * sub_rows) for n in range(MLP_SUB_TILES)]

    def branches(rows):
        x = x_ref[rows, :]
        h = _rms_norm_rows(x, g1_ref[...]).astype(_BF16)
        gates = jax.nn.sigmoid(dot(h, wg_ref) + bgate_ref[...])
        a = jnp.concatenate([a_ref[hp, rows, :] for hp in range(HEAD_PAIRS)], axis=-1)
        ya = dot(a, wap_ref)
        yc = dot(yc_ref[rows, :], wcp_ref)
        return x, (gates[:, :D_MODEL] * ya + gates[:, D_MODEL:] * yc).astype(_BF16)

    def residual(x, merged):
        return x + dot(merged, wout_ref)

    def mlp(rows, x1):
        h2 = _rms_norm_rows(x1, g2_ref[...]).astype(_BF16)
        acc = x1
        for f in range(D_FF // FF_TILE):
            up = dot(h2, wup_ref, cols=slice(f * FF_TILE, (f + 1) * FF_TILE))
            act = jnp.square(jnp.maximum(up, 0.0)).astype(_BF16)
            acc = acc + dot(act, wdown_ref, rows=slice(f * FF_TILE // 2, (f + 1) * FF_TILE // 2))
        o_ref[rows, :] = acc

    merged = [branches(rows) for rows in subs]
    x1 = [residual(*m) for m in merged]
    for rows, v in zip(subs, x1):
        mlp(rows, v)


def kernel(x, norm1_g, w_in, q_norm_g, k_norm_g, rel_bias, conv_w, conv_b, w_attn_proj,
           w_conv_proj, w_gate, b_gate, w_out, norm2_g, w_up, w_down):
    b, s, d = x.shape
    assert d == D_MODEL and s % ROW_TILE == 0 and s % PROJ_ROWS == 0
    assert s % (Q_TILE * Q_TILES_PER_STEP) == 0
    t = b * s
    xf = x.reshape(t, d)
    row = lambda v: v.reshape(1, -1).astype(_F32)
    cparams = functools.partial(pltpu.CompilerParams, vmem_limit_bytes=VMEM_LIMIT)

    gq = row(jnp.tile(q_norm_g, N_HEADS) * (HEAD_DIM ** -0.5 * LOG2E))
    gk = row(jnp.tile(k_norm_g, N_HEADS))
    staging = [pltpu.VMEM((STAGE_SLOTS, STAGE_ROWS, STAGE_COLS), _F32),
               pltpu.SemaphoreType.DMA((STAGE_SLOTS,))]
    heads_shape = jax.ShapeDtypeStruct((HEAD_PAIRS, t, LANES), _BF16)

    def heads_spec(rows):
        return pl.BlockSpec((HEAD_PAIRS, rows, LANES), lambda i: (0, i, 0))

    q, k, vt, yc = pl.pallas_call(
        functools.partial(_proj_kernel, tiles_per_seq=s // PROJ_ROWS),
        grid=(t // PROJ_ROWS,),
        in_specs=[
            pl.BlockSpec((PROJ_ROWS, d), lambda i: (i, 0)),
            _resident((1, d)),
            pl.BlockSpec(memory_space=pl.ANY),
            _resident((1, d)),
            _resident((1, d)),
            _resident((CONV_WIDTH, d)),
            _resident((1, d)),
        ],
        out_specs=[heads_spec(PROJ_ROWS), heads_spec(PROJ_ROWS),
                   pl.BlockSpec((HEAD_PAIRS, LANES, PROJ_ROWS), lambda i: (0, 0, i)),
                   pl.BlockSpec((PROJ_ROWS, d), lambda i: (i, 0))],
        out_shape=[heads_shape, heads_shape,
                   jax.ShapeDtypeStruct((HEAD_PAIRS, LANES, t), _BF16),
                   jax.ShapeDtypeStruct((t, d), _BF16)],
        scratch_shapes=[pltpu.VMEM((PROJ_ROWS + CARRY_ROWS, d), _F32),
                        _packed_weight((d, 2 * d)), _packed_weight((d, d)),
                        _packed_weight((d, 3 * d))] + staging,
        compiler_params=cparams(dimension_semantics=("arbitrary",)),
        name="proj",
    )(xf, row(norm1_g), w_in.astype(_F32), gq, gk, conv_w.astype(_F32), row(conv_b))

    n_rel = rel_bias.shape[1]
    rb_rows = jnp.pad(rel_bias.astype(_F32), ((0, 0), (0, BIAS_LANES - n_rel)))
    nq = s // Q_TILE
    ns = nq // Q_TILES_PER_STEP
    q_spec = pl.BlockSpec((HEAD_PAIRS, Q_TILES_PER_STEP * Q_TILE, LANES),
                          lambda bi, si: (0, bi * ns + si, 0))

    def key_tile(si, ref_idx):
        return jnp.maximum(si * Q_TILES_PER_STEP - (KEY_TILES - 1) + ref_idx, 0)

    def k_spec(ref_idx):
        return pl.BlockSpec((HEAD_PAIRS, Q_TILE, LANES),
                            lambda bi, si: (0, bi * nq + key_tile(si, ref_idx), 0))

    def vt_spec(ref_idx):
        return pl.BlockSpec((HEAD_PAIRS, LANES, Q_TILE),
                            lambda bi, si: (0, 0, bi * nq + key_tile(si, ref_idx)))

    attn = pl.pallas_call(
        _attn_kernel,
        grid=(b, ns),
        in_specs=[q_spec] + [k_spec(r) for r in range(KEY_REFS)]
                 + [vt_spec(r) for r in range(KEY_REFS)]
                 + [_resident((N_HEADS, 1, BIAS_LANES))],
        out_specs=q_spec,
        out_shape=heads_shape,
        scratch_shapes=[pltpu.VMEM((N_HEADS, KEY_TILES * Q_TILE, Q_TILE), _F32)],
        compiler_params=cparams(dimension_semantics=("arbitrary", "arbitrary")),
        name="attn",
    )(q, *([k] * KEY_REFS), *([vt] * KEY_REFS), rb_rows.reshape(N_HEADS, 1, BIAS_LANES))

    mlp_weights = [w.astype(_F32) for w in (w_gate, w_attn_proj, w_conv_proj, w_out, w_up, w_down)]
    out = pl.pallas_call(
        _mix_mlp_kernel,
        grid=(t // ROW_TILE,),
        in_specs=[
            pl.BlockSpec((ROW_TILE, d), lambda i: (i, 0)),
            heads_spec(ROW_TILE),
            pl.BlockSpec((ROW_TILE, d), lambda i: (i, 0)),
            _resident((1, d)),
            _resident((1, 2 * d)),
            _resident((1, d)),
        ] + [pl.BlockSpec(memory_space=pl.ANY)] * len(mlp_weights),
        out_specs=pl.BlockSpec((ROW_TILE, d), lambda i: (i, 0)),
        out_shape=jax.ShapeDtypeStruct((t, d), _F32),
        scratch_shapes=[_packed_weight(w.shape) for w in mlp_weights] + staging,
        compiler_params=cparams(dimension_semantics=("arbitrary",)),
        name="mix_mlp",
    )(xf, attn, yc, row(norm1_g), row(b_gate), row(norm2_g), *mlp_weights)
    return out.reshape(b, s, d)
```

```python
import functools
import math

import jax
import jax.numpy as jnp
from jax import lax
from jax.experimental import pallas as pl
from jax.experimental.pallas import tpu as pltpu

D_MODEL = 1024
N_HEADS = 16
HEAD_DIM = 64
CHUNK = 64
N_PREV_CHUNKS = 8
MAX_REL = 256
CONV_WIDTH = 3
D_FF = 4 * D_MODEL
EPS = 1e-6
NEG_INF = -1e30
LOG2E = math.log2(math.e)

LANES = 128
BF16_ROWS = 16
HEAD_PAIRS = D_MODEL // LANES
MXU_TILE = 256

PROJ_ROWS = 512
ROW_TILE = 512
PROJ_TILE = 2 * MXU_TILE
CONV_TILE = MXU_TILE
Q_TILE = 256
KEY_TILES = 1 + (N_PREV_CHUNKS * CHUNK) // Q_TILE
Q_TILES_PER_STEP = 2
KEY_REFS = KEY_TILES + Q_TILES_PER_STEP - 1
BAND_ROWS = (N_PREV_CHUNKS + LANES // CHUNK) * CHUNK
BIAS_LANES = (KEY_TILES + 1) * Q_TILE
assert 2 * MAX_REL + 1 <= BIAS_LANES and Q_TILE - 1 <= MAX_REL
SCORE_LOOKAHEAD = 2
KEY_BLOCK = LANES
FF_TILE = 1024
PROJ_SUB_TILES = 2
MLP_SUB_TILES = 2
CARRY_ROWS = 8
STAGE_ROWS, STAGE_COLS = 256, D_MODEL
STAGE_SLOTS = 4
VMEM_LIMIT = 56 * 1024 * 1024

_BF16 = jnp.bfloat16
_F32 = jnp.float32
_NT_DIMS = (((1,), (1,)), ((), ()))


def _resident(shape):
    return pl.BlockSpec(shape, lambda *_: (0,) * len(shape), pipeline_mode=pl.Buffered(1))


def _rms_norm_rows(x, g):
    ms = jnp.mean(x * x, axis=-1, keepdims=True)
    return x * lax.rsqrt(ms + EPS) * g


def _packed_weight(shape):
    k, n = shape
    return pltpu.VMEM((k // 2, n), jnp.uint32)


def _pack(block):
    return pltpu.bitcast(block.astype(_BF16), jnp.uint32)


def _unpack(words):
    return pltpu.bitcast(words, _BF16)


def _stage_weights(jobs, stage, sem):
    def copy(n):
        src, r0, c0, _ = jobs[n]
        return pltpu.make_async_copy(src.at[pl.ds(r0, STAGE_ROWS), pl.ds(c0, STAGE_COLS)],
                                     stage.at[n % STAGE_SLOTS], sem.at[n % STAGE_SLOTS])

    ahead = STAGE_SLOTS - 1
    for n in range(min(ahead, len(jobs))):
        copy(n).start(priority=n % 2)
    for n, job in enumerate(jobs):
        if n + ahead < len(jobs):
            copy(n + ahead).start(priority=(n + ahead) % 2)
        copy(n).wait()
        job[3](stage[n % STAGE_SLOTS])


def _weight_jobs(src, dst, row_lo=0, col_lo=0, shape=None):
    k, n = shape if shape is not None else src.shape
    jobs = []
    for r in range(0, k, STAGE_ROWS):
        for c in range(0, n, STAGE_COLS):
            def store(block, r=r, c=c):
                dst[r // 2:(r + STAGE_ROWS) // 2, c:c + STAGE_COLS] = _pack(block)
            jobs.append((src, row_lo + r, col_lo + c, store))
    return jobs


def _proj_kernel(x_ref, g1_ref, w_in_hbm, gq_ref, gk_ref, cw_ref, cb_ref,
                 q_ref, k_ref, vt_ref, yc_ref,
                 u_scr, wqk_ref, wvt_ref, wc_ref, stage, sem, *, tiles_per_seq):
    tm = x_ref.shape[0]

    @pl.when(pl.program_id(0) == 0)
    def _():
        def store_vt(block, r):
            wvt_ref[:, r:r + STAGE_ROWS] = _pack(block.T)

        jobs = _weight_jobs(w_in_hbm, wqk_ref, 0, 0, (D_MODEL, 2 * D_MODEL))
        jobs += _weight_jobs(w_in_hbm, wc_ref, 0, 3 * D_MODEL, (D_MODEL, 3 * D_MODEL))
        jobs += [(w_in_hbm, r, 2 * D_MODEL, functools.partial(store_vt, r=r))
                 for r in range(0, D_MODEL, STAGE_ROWS)]
        _stage_weights(jobs, stage, sem)

    n_col_tiles = D_MODEL // PROJ_TILE
    slabs = PROJ_TILE // LANES

    @pl.when(pl.program_id(0) % tiles_per_seq == 0)
    def _():
        u_scr[0:CARRY_ROWS, :] = jnp.zeros((CARRY_ROWS, D_MODEL), _F32)

    def sub_tile(r0, rows):
        tok = slice(r0, r0 + rows)
        h = _rms_norm_rows(x_ref[tok, :], g1_ref[...]).astype(_BF16)
        low_half = lax.broadcasted_iota(jnp.int32, (rows, LANES), 1) < HEAD_DIM

        def proj(w_ref, j, c, width=PROJ_TILE):
            lo = j * D_MODEL + c * width
            return jnp.dot(h, _unpack(w_ref[:, lo:lo + width]), preferred_element_type=_F32)

        for c in range(D_MODEL // CONV_TILE):
            cols = slice(c * CONV_TILE, (c + 1) * CONV_TILE)
            bg = proj(wc_ref, 0, c, CONV_TILE)
            u = proj(wc_ref, 1, c, CONV_TILE) * proj(wc_ref, 2, c, CONV_TILE)
            base = CARRY_ROWS + r0
            u_scr[base:base + rows, cols] = u
            u1 = u_scr[base - 1:base - 1 + rows, cols]
            u2 = u_scr[base - 2:base - 2 + rows, cols]
            conv = (cb_ref[:, cols] + cw_ref[0:1, cols] * u2 + cw_ref[1:2, cols] * u1
                    + cw_ref[2:3, cols] * u)
            yc_ref[tok, cols] = (bg * conv).astype(_BF16)

        def finish_head_norm(p, g_ref, o_ref, c):
            g = g_ref[:, c * PROJ_TILE:(c + 1) * PROJ_TILE]
            for sl in range(slabs):
                ps = p[:, sl * LANES:(sl + 1) * LANES]
                p2 = ps * ps
                ss_lo = jnp.sum(jnp.where(low_half, p2, 0.0), axis=-1, keepdims=True)
                ss_hi = jnp.sum(jnp.where(low_half, 0.0, p2), axis=-1, keepdims=True)
                ss = jnp.where(low_half, ss_lo, ss_hi)
                pn = ps * lax.rsqrt(ss * (1.0 / HEAD_DIM) + EPS) * g[:, sl * LANES:(sl + 1) * LANES]
                o_ref[c * slabs + sl, tok, :] = pn.astype(_BF16)

        waiting = None
        for j, (g_ref, o_ref) in enumerate(((gq_ref, q_ref), (gk_ref, k_ref))):
            for c in range(n_col_tiles):
                p = proj(wqk_ref, j, c)
                if waiting is not None:
                    finish_head_norm(*waiting)
                waiting = (p, g_ref, o_ref, c)

        for c in range(n_col_tiles):
            w_rows = slice(c * PROJ_TILE // 2, (c + 1) * PROJ_TILE // 2)
            vt = lax.dot_general(_unpack(wvt_ref[w_rows, :]), h, _NT_DIMS,
                                 preferred_element_type=_F32).astype(_BF16)
            if waiting is not None:
                finish_head_norm(*waiting)
                waiting = None
            for sl in range(slabs):
                vt_ref[c * slabs + sl, :, tok] = vt[sl * LANES:(sl + 1) * LANES, :]

    rows = tm // PROJ_SUB_TILES
    for n in range(PROJ_SUB_TILES):
        sub_tile(n * rows, rows)
    u_scr[0:CARRY_ROWS, :] = u_scr[tm:tm + CARRY_ROWS, :]


def _bias_tile(rb):
    kw, tq = KEY_TILES * Q_TILE, Q_TILE
    lane = lax.broadcasted_iota(jnp.int32, rb.shape, 1)
    top = rb[:, 2 * MAX_REL:2 * MAX_REL + 1]
    fwd = jnp.where(lane > 2 * MAX_REL, top, rb) * LOG2E
    rolled = pltpu.roll(jnp.broadcast_to(fwd, (kw, BIAS_LANES)), tq, axis=1,
                        stride=1, stride_axis=0)
    kj = lax.broadcasted_iota(jnp.int32, (kw, tq), 0)
    qi = lax.broadcasted_iota(jnp.int32, (kw, tq), 1)
    dchunk = (qi // CHUNK + N_PREV_CHUNKS) - kj // CHUNK
    band = (dchunk >= 0) & (dchunk <= N_PREV_CHUNKS)
    return jnp.where(band, rolled[:, :tq], NEG_INF)


def _attn_kernel(q_ref, *refs):
    k_refs, v_refs = refs[:KEY_REFS], refs[KEY_REFS:2 * KEY_REFS]
    rb_ref, o_ref, bias_ref = refs[2 * KEY_REFS:]
    tq = Q_TILE
    lane = lax.broadcasted_iota(jnp.int32, (tq, LANES), 1)
    ones_rows = jnp.ones((BF16_ROWS, tq), _BF16)

    @pl.when((pl.program_id(0) == 0) & (pl.program_id(1) == 0))
    def _():
        def build(head, carry):
            bias_ref[head] = _bias_tile(rb_ref[head])
            return carry
        lax.fori_loop(0, N_HEADS, build, 0)

    def scores(sub, tiles, head):
        lo, hi = tiles[0] * tq, (tiles[-1] + 1) * tq
        hp, hh = divmod(head, 2)
        sel = (lane < HEAD_DIM) if hh == 0 else (lane >= HEAD_DIM)
        q = q_ref[hp, sub * tq:(sub + 1) * tq, :]
        qh = jnp.where(sel, q, jnp.zeros_like(q))
        keys = jnp.concatenate([k_refs[sub + t][hp] for t in tiles], axis=0)
        return lax.dot_general(keys, qh, _NT_DIMS, preferred_element_type=_F32)

    def attend(sub, tiles, head, s):
        lo, hi = tiles[0] * tq, (tiles[-1] + 1) * tq
        hp, hh = divmod(head, 2)
        n_cols = tq // LANES
        m = [None] * n_cols
        acc = None
        for b0 in range(lo, hi, KEY_BLOCK):
            p_parts, alpha_parts = [], []
            for c in range(n_cols):
                if b0 + KEY_BLOCK <= c * LANES or b0 >= c * LANES + BAND_ROWS:
                    p_parts.append(jnp.zeros((KEY_BLOCK, LANES), _BF16))
                    alpha_parts.append(jnp.ones((1, LANES), _F32))
                    continue
                sb = (s[b0 - lo:b0 - lo + KEY_BLOCK, c * LANES:(c + 1) * LANES]
                      + bias_ref[head, b0:b0 + KEY_BLOCK, c * LANES:(c + 1) * LANES])
                mb = jnp.max(sb, axis=0, keepdims=True)
                if m[c] is None:
                    m_new = mb
                    alpha_parts.append(jnp.ones((1, LANES), _F32))
                else:
                    m_new = jnp.maximum(m[c], mb)
                    alpha_parts.append(jnp.exp2(m[c] - m_new))
                m[c] = m_new
                p_parts.append(jnp.exp2(sb - m_new).astype(_BF16))
            t, off = divmod(b0, tq)
            v_ext = jnp.concatenate(
                [v_refs[sub + t][hp, hh * HEAD_DIM:(hh + 1) * HEAD_DIM, off:off + KEY_BLOCK],
                 ones_rows[:, :KEY_BLOCK]], axis=0)
            part = jnp.dot(v_ext, jnp.concatenate(p_parts, axis=1),
                           preferred_element_type=_F32)
            acc = part if acc is None else acc * jnp.concatenate(alpha_parts, axis=1) + part
        return acc[:HEAD_DIM] * (1.0 / acc[HEAD_DIM:HEAD_DIM + 1])

    def run(tiles_per_sub):
        work = [(sub, tiles, head) for sub, tiles in enumerate(tiles_per_sub)
                for head in range(N_HEADS)]
        scored = [scores(*w) for w in work[:SCORE_LOOKAHEAD]]
        halves = []
        for n, (sub, tiles, head) in enumerate(work):
            if n + SCORE_LOOKAHEAD < len(work):
                scored.append(scores(*work[n + SCORE_LOOKAHEAD]))
            halves.append(attend(sub, tiles, head, scored.pop(0)))
            if head % 2:
                o_ref[head // 2, sub * tq:(sub + 1) * tq, :] = (
                    jnp.concatenate(halves, axis=0).T.astype(_BF16))
                halves = []

    all_tiles = tuple(range(KEY_TILES))
    first_step = tuple(all_tiles[max(KEY_TILES - 1 - j, 0):] for j in range(Q_TILES_PER_STEP))
    qs = pl.program_id(1)
    pl.when(qs == 0)(functools.partial(run, first_step))
    pl.when(qs > 0)(functools.partial(run, (all_tiles,) * Q_TILES_PER_STEP))


def _mix_mlp_kernel(x_ref, a_ref, yc_ref, g1_ref, bgate_ref, g2_ref,
                    wg_hbm, wap_hbm, wcp_hbm, wout_hbm, wup_hbm, wdown_hbm, o_ref,
                    wg_ref, wap_ref, wcp_ref, wout_ref, wup_ref, wdown_ref, stage, sem):
    @pl.when(pl.program_id(0) == 0)
    def _():
        pairs = ((wg_hbm, wg_ref), (wap_hbm, wap_ref), (wcp_hbm, wcp_ref), (wout_hbm, wout_ref),
                 (wup_hbm, wup_ref), (wdown_hbm, wdown_ref))
        _stage_weights([job for src, dst in pairs for job in _weight_jobs(src, dst)], stage, sem)

    def dot(lhs, w_ref, rows=slice(None), cols=slice(None)):
        return jnp.dot(lhs, _unpack(w_ref[rows, cols]), preferred_element_type=_F32)

    sub_rows = x_ref.shape[0] // MLP_SUB_TILES
    subs = [slice(n * sub_rows, (n + 1) * sub_rows) for n in range(MLP_SUB_TILES)]

    def branches(rows):
        x = x_ref[rows, :]
        h = _rms_norm_rows(x, g1_ref[...]).astype(_BF16)
        gates = jax.nn.sigmoid(dot(h, wg_ref) + bgate_ref[...])
        a = jnp.concatenate([a_ref[hp, rows, :] for hp in range(HEAD_PAIRS)], axis=-1)
        ya = dot(a, wap_ref)
        yc = dot(yc_ref[rows, :], wcp_ref)
        return x, (gates[:, :D_MODEL] * ya + gates[:, D_MODEL:] * yc).astype(_BF16)

    def residual(x, merged):
        return x + dot(merged, wout_ref)

    def mlp(rows, x1):
        h2 = _rms_norm_rows(x1, g2_ref[...]).astype(_BF16)
        acc = x1
        for f in range(D_FF // FF_TILE):
            up = dot(h2, wup_ref, cols=slice(f * FF_TILE, (f + 1) * FF_TILE))
            act = jnp.square(jnp.maximum(up, 0.0)).astype(_BF16)
            acc = acc + dot(act, wdown_ref, rows=slice(f * FF_TILE // 2, (f + 1) * FF_TILE // 2))
        o_ref[rows, :] = acc

    merged = [branches(rows) for rows in subs]
    x1 = [residual(*m) for m in merged]
    for rows, v in zip(subs, x1):
        mlp(rows, v)


def kernel(x, norm1_g, w_in, q_norm_g, k_norm_g, rel_bias, conv_w, conv_b, w_attn_proj,
           w_conv_proj, w_gate, b_gate, w_out, norm2_g, w_up, w_down):
    b, s, d = x.shape
    assert d == D_MODEL and s % ROW_TILE == 0 and s % PROJ_ROWS == 0
    assert s % (Q_TILE * Q_TILES_PER_STEP) == 0
    t = b * s
    xf = x.reshape(t, d)
    row = lambda v: v.reshape(1, -1).astype(_F32)
    cparams = functools.partial(pltpu.CompilerParams, vmem_limit_bytes=VMEM_LIMIT)

    gq = row(jnp.tile(q_norm_g, N_HEADS) * (HEAD_DIM ** -0.5 * LOG2E))
    gk = row(jnp.tile(k_norm_g, N_HEADS))
    staging = [pltpu.VMEM((STAGE_SLOTS, STAGE_ROWS, STAGE_COLS), _F32),
               pltpu.SemaphoreType.DMA((STAGE_SLOTS,))]
    heads_shape = jax.ShapeDtypeStruct((HEAD_PAIRS, t, LANES), _BF16)

    def heads_spec(rows):
        return pl.BlockSpec((HEAD_PAIRS, rows, LANES), lambda i: (0, i, 0))

    q, k, vt, yc = pl.pallas_call(
        functools.partial(_proj_kernel, tiles_per_seq=s // PROJ_ROWS),
        grid=(t // PROJ_ROWS,),
        in_specs=[
            pl.BlockSpec((PROJ_ROWS, d), lambda i: (i, 0)),
            _resident((1, d)),
            pl.BlockSpec(memory_space=pl.ANY),
            _resident((1, d)),
            _resident((1, d)),
            _resident((CONV_WIDTH, d)),
            _resident((1, d)),
        ],
        out_specs=[heads_spec(PROJ_ROWS), heads_spec(PROJ_ROWS),
                   pl.BlockSpec((HEAD_PAIRS, LANES, PROJ_ROWS), lambda i: (0, 0, i)),
                   pl.BlockSpec((PROJ_ROWS, d), lambda i: (i, 0))],
        out_shape=[heads_shape, heads_shape,
                   jax.ShapeDtypeStruct((HEAD_PAIRS, LANES, t), _BF16),
                   jax.ShapeDtypeStruct((t, d), _BF16)],
        scratch_shapes=[pltpu.VMEM((PROJ_ROWS + CARRY_ROWS, d), _F32),
                        _packed_weight((d, 2 * d)), _packed_weight((d, d)),
                        _packed_weight((d, 3 * d))] + staging,
        compiler_params=cparams(dimension_semantics=("arbitrary",)),
        name="proj",
    )(xf, row(norm1_g), w_in.astype(_F32), gq, gk, conv_w.astype(_F32), row(conv_b))

    n_rel = rel_bias.shape[1]
    rb_rows = jnp.pad(rel_bias.astype(_F32), ((0, 0), (0, BIAS_LANES - n_rel)))
    nq = s // Q_TILE
    ns = nq // Q_TILES_PER_STEP
    q_spec = pl.BlockSpec((HEAD_PAIRS, Q_TILES_PER_STEP * Q_TILE, LANES),
                          lambda bi, si: (0, bi * ns + si, 0))

    def key_tile(si, ref_idx):
        return jnp.maximum(si * Q_TILES_PER_STEP - (KEY_TILES - 1) + ref_idx, 0)

    def k_spec(ref_idx):
        return pl.BlockSpec((HEAD_PAIRS, Q_TILE, LANES),
                            lambda bi, si: (0, bi * nq + key_tile(si, ref_idx), 0))

    def vt_spec(ref_idx):
        return pl.BlockSpec((HEAD_PAIRS, LANES, Q_TILE),
                            lambda bi, si: (0, 0, bi * nq + key_tile(si, ref_idx)))

    attn = pl.pallas_call(
        _attn_kernel,
        grid=(b, ns),
        in_specs=[q_spec] + [k_spec(r) for r in range(KEY_REFS)]
                 + [vt_spec(r) for r in range(KEY_REFS)]
                 + [_resident((N_HEADS, 1, BIAS_LANES))],
        out_specs=q_spec,
        out_shape=heads_shape,
        scratch_shapes=[pltpu.VMEM((N_HEADS, KEY_TILES * Q_TILE, Q_TILE), _F32)],
        compiler_params=cparams(dimension_semantics=("arbitrary", "arbitrary")),
        name="attn",
    )(q, *([k] * KEY_REFS), *([vt] * KEY_REFS), rb_rows.reshape(N_HEADS, 1, BIAS_LANES))

    mlp_weights = [w.astype(_F32) for w in (w_gate, w_attn_proj, w_conv_proj, w_out, w_up, w_down)]
    out = pl.pallas_call(
        _mix_mlp_kernel,
        grid=(t // ROW_TILE,),
        in_specs=[
            pl.BlockSpec((ROW_TILE, d), lambda i: (i, 0)),
            heads_spec(ROW_TILE),
            pl.BlockSpec((ROW_TILE, d), lambda i: (i, 0)),
            _resident((1, d)),
            _resident((1, 2 * d)),
            _resident((1, d)),
        ] + [pl.BlockSpec(memory_space=pl.ANY)] * len(mlp_weights),
        out_specs=pl.BlockSpec((ROW_TILE, d), lambda i: (i, 0)),
        out_shape=jax.ShapeDtypeStruct((t, d), _F32),
        scratch_shapes=[_packed_weight(w.shape) for w in mlp_weights] + staging,
        compiler_params=cparams(dimension_semantics=("arbitrary",)),
        name="mix_mlp",
    )(xf, attn, yc, row(norm1_g), row(b_gate), row(norm2_g), *mlp_weights)
    return out.reshape(b, s, d)
```

```python
import functools
import math

import jax
import jax.numpy as jnp
from jax import lax
from jax.experimental import pallas as pl
from jax.experimental.pallas import tpu as pltpu

D_MODEL = 1024
N_HEADS = 16
HEAD_DIM = 64
CHUNK = 64
N_PREV_CHUNKS = 8
MAX_REL = 256
CONV_WIDTH = 3
D_FF = 4 * D_MODEL
EPS = 1e-6
NEG_INF = -1e30
LOG2E = math.log2(math.e)

LANES = 128
BF16_ROWS = 16
HEAD_PAIRS = D_MODEL // LANES
MXU_TILE = 256

PROJ_ROWS = 512
ROW_TILE = 512
PROJ_TILE = 2 * MXU_TILE
CONV_TILE = MXU_TILE
Q_TILE = 256
KEY_TILES = 1 + (N_PREV_CHUNKS * CHUNK) // Q_TILE
Q_TILES_PER_STEP = 2
KEY_REFS = KEY_TILES + Q_TILES_PER_STEP - 1
BAND_ROWS = (N_PREV_CHUNKS + LANES // CHUNK) * CHUNK
BIAS_LANES = (KEY_TILES + 1) * Q_TILE
assert 2 * MAX_REL + 1 <= BIAS_LANES and Q_TILE - 1 <= MAX_REL
SCORE_LOOKAHEAD = 3
KEY_BLOCK = 2 * LANES
BAND_BLOCK = LANES
FF_TILE = 1024
PROJ_SUB_TILES = 2
MLP_SUB_TILES = 2
CARRY_ROWS = 8
STAGE_ROWS, STAGE_COLS = 256, D_MODEL
STAGE_SLOTS = 4
VMEM_LIMIT = 56 * 1024 * 1024

_BF16 = jnp.bfloat16
_F32 = jnp.float32
_NT_DIMS = (((1,), (1,)), ((), ()))


def _resident(shape):
    return pl.BlockSpec(shape, lambda *_: (0,) * len(shape), pipeline_mode=pl.Buffered(1))


def _rms_norm_rows(x, g):
    ms = jnp.mean(x * x, axis=-1, keepdims=True)
    return x * lax.rsqrt(ms + EPS) * g


def _packed_weight(shape):
    k, n = shape
    return pltpu.VMEM((k // 2, n), jnp.uint32)


def _pack(block):
    return pltpu.bitcast(block.astype(_BF16), jnp.uint32)


def _unpack(words):
    return pltpu.bitcast(words, _BF16)


def _stage_weights(jobs, stage, sem):
    def copy(n):
        src, r0, c0, _ = jobs[n]
        return pltpu.make_async_copy(src.at[pl.ds(r0, STAGE_ROWS), pl.ds(c0, STAGE_COLS)],
                                     stage.at[n % STAGE_SLOTS], sem.at[n % STAGE_SLOTS])

    ahead = STAGE_SLOTS - 1
    for n in range(min(ahead, len(jobs))):
        copy(n).start(priority=n % 2)
    for n, job in enumerate(jobs):
        if n + ahead < len(jobs):
            copy(n + ahead).start(priority=(n + ahead) % 2)
        copy(n).wait()
        job[3](stage[n % STAGE_SLOTS])


def _weight_jobs(src, dst, row_lo=0, col_lo=0, shape=None):
    k, n = shape if shape is not None else src.shape
    jobs = []
    for r in range(0, k, STAGE_ROWS):
        for c in range(0, n, STAGE_COLS):
            def store(block, r=r, c=c):
                dst[r // 2:(r + STAGE_ROWS) // 2, c:c + STAGE_COLS] = _pack(block)
            jobs.append((src, row_lo + r, col_lo + c, store))
    return jobs


def _proj_kernel(x_ref, g1_ref, w_in_hbm, gq_ref, gk_ref, cw_ref, cb_ref,
                 q_ref, k_ref, vt_ref, yc_ref,
                 u_scr, wqk_ref, wvt_ref, wc_ref, stage, sem, *, tiles_per_seq):
    tm = x_ref.shape[0]

    @pl.when(pl.program_id(0) == 0)
    def _():
        def store_vt(block, r):
            wvt_ref[:, r:r + STAGE_ROWS] = _pack(block.T)

        jobs = _weight_jobs(w_in_hbm, wqk_ref, 0, 0, (D_MODEL, 2 * D_MODEL))
        jobs += _weight_jobs(w_in_hbm, wc_ref, 0, 3 * D_MODEL, (D_MODEL, 3 * D_MODEL))
        jobs += [(w_in_hbm, r, 2 * D_MODEL, functools.partial(store_vt, r=r))
                 for r in range(0, D_MODEL, STAGE_ROWS)]
        _stage_weights(jobs, stage, sem)

    n_col_tiles = D_MODEL // PROJ_TILE
    slabs = PROJ_TILE // LANES

    @pl.when(pl.program_id(0) % tiles_per_seq == 0)
    def _():
        u_scr[0:CARRY_ROWS, :] = jnp.zeros((CARRY_ROWS, D_MODEL), _F32)

    def sub_tile(r0, rows):
        tok = slice(r0, r0 + rows)
        h = _rms_norm_rows(x_ref[tok, :], g1_ref[...]).astype(_BF16)
        low_half = lax.broadcasted_iota(jnp.int32, (rows, LANES), 1) < HEAD_DIM

        def proj(w_ref, j, c, width=PROJ_TILE):
            lo = j * D_MODEL + c * width
            return jnp.dot(h, _unpack(w_ref[:, lo:lo + width]), preferred_element_type=_F32)

        for c in range(D_MODEL // CONV_TILE):
            cols = slice(c * CONV_TILE, (c + 1) * CONV_TILE)
            bg = proj(wc_ref, 0, c, CONV_TILE)
            u = proj(wc_ref, 1, c, CONV_TILE) * proj(wc_ref, 2, c, CONV_TILE)
            base = CARRY_ROWS + r0
            u_scr[base:base + rows, cols] = u
            u1 = u_scr[base - 1:base - 1 + rows, cols]
            u2 = u_scr[base - 2:base - 2 + rows, cols]
            conv = (cb_ref[:, cols] + cw_ref[0:1, cols] * u2 + cw_ref[1:2, cols] * u1
                    + cw_ref[2:3, cols] * u)
            yc_ref[tok, cols] = (bg * conv).astype(_BF16)

        def finish_head_norm(p, g_ref, o_ref, c):
            g = g_ref[:, c * PROJ_TILE:(c + 1) * PROJ_TILE]
            for sl in range(slabs):
                ps = p[:, sl * LANES:(sl + 1) * LANES]
                p2 = ps * ps
                ss_lo = jnp.sum(jnp.where(low_half, p2, 0.0), axis=-1, keepdims=True)
                ss_hi = jnp.sum(jnp.where(low_half, 0.0, p2), axis=-1, keepdims=True)
                ss = jnp.where(low_half, ss_lo, ss_hi)
                pn = ps * lax.rsqrt(ss * (1.0 / HEAD_DIM) + EPS) * g[:, sl * LANES:(sl + 1) * LANES]
                o_ref[c * slabs + sl, tok, :] = pn.astype(_BF16)

        waiting = None
        for j, (g_ref, o_ref) in enumerate(((gq_ref, q_ref), (gk_ref, k_ref))):
            for c in range(n_col_tiles):
                p = proj(wqk_ref, j, c)
                if waiting is not None:
                    finish_head_norm(*waiting)
                waiting = (p, g_ref, o_ref, c)

        for c in range(n_col_tiles):
            w_rows = slice(c * PROJ_TILE // 2, (c + 1) * PROJ_TILE // 2)
            vt = lax.dot_general(_unpack(wvt_ref[w_rows, :]), h, _NT_DIMS,
                                 preferred_element_type=_F32).astype(_BF16)
            if waiting is not None:
                finish_head_norm(*waiting)
                waiting = None
            for sl in range(slabs):
                vt_ref[c * slabs + sl, :, tok] = vt[sl * LANES:(sl + 1) * LANES, :]

    rows = tm // PROJ_SUB_TILES
    for n in range(PROJ_SUB_TILES):
        sub_tile(n * rows, rows)
    u_scr[0:CARRY_ROWS, :] = u_scr[tm:tm + CARRY_ROWS, :]


def _bias_tile(rb):
    kw, tq = KEY_TILES * Q_TILE, Q_TILE
    lane = lax.broadcasted_iota(jnp.int32, rb.shape, 1)
    top = rb[:, 2 * MAX_REL:2 * MAX_REL + 1]
    fwd = jnp.where(lane > 2 * MAX_REL, top, rb) * LOG2E
    rolled = pltpu.roll(jnp.broadcast_to(fwd, (kw, BIAS_LANES)), tq, axis=1,
                        stride=1, stride_axis=0)
    kj = lax.broadcasted_iota(jnp.int32, (kw, tq), 0)
    qi = lax.broadcasted_iota(jnp.int32, (kw, tq), 1)
    dchunk = (qi // CHUNK + N_PREV_CHUNKS) - kj // CHUNK
    band = (dchunk >= 0) & (dchunk <= N_PREV_CHUNKS)
    return jnp.where(band, rolled[:, :tq], NEG_INF)


def _attn_kernel(q_ref, *refs):
    k_refs, v_refs = refs[:KEY_REFS], refs[KEY_REFS:2 * KEY_REFS]
    rb_ref, o_ref, bias_ref = refs[2 * KEY_REFS:]
    tq = Q_TILE
    lane = lax.broadcasted_iota(jnp.int32, (tq, LANES), 1)
    ones_rows = jnp.ones((BF16_ROWS, tq), _BF16)

    @pl.when((pl.program_id(0) == 0) & (pl.program_id(1) == 0))
    def _():
        def build(head, carry):
            bias_ref[head] = _bias_tile(rb_ref[head])
            return carry
        lax.fori_loop(0, N_HEADS, build, 0)

    def scores(sub, tiles, head):
        lo, hi = tiles[0] * tq, (tiles[-1] + 1) * tq
        hp, hh = divmod(head, 2)
        sel = (lane < HEAD_DIM) if hh == 0 else (lane >= HEAD_DIM)
        q = q_ref[hp, sub * tq:(sub + 1) * tq, :]
        qh = jnp.where(sel, q, jnp.zeros_like(q))
        keys = jnp.concatenate([k_refs[sub + t][hp] for t in tiles], axis=0)
        return lax.dot_general(keys, qh, _NT_DIMS, preferred_element_type=_F32)

    def attend(sub, tiles, head, s):
        lo, hi = tiles[0] * tq, (tiles[-1] + 1) * tq
        hp, hh = divmod(head, 2)
        n_cols = tq // LANES
        m = [None] * n_cols
        acc = None
        for b0 in range(lo, hi, KEY_BLOCK):
            p_parts, alpha_parts = [], []
            for c in range(n_cols):
                starts = range(b0, b0 + KEY_BLOCK, BAND_BLOCK)
                seen = [c * LANES <= r < c * LANES + BAND_ROWS for r in starts]
                sbs = [s[r - lo:r - lo + BAND_BLOCK, c * LANES:(c + 1) * LANES]
                       + bias_ref[head, r:r + BAND_BLOCK, c * LANES:(c + 1) * LANES]
                       if ok else None for r, ok in zip(starts, seen)]
                if not any(seen):
                    p_parts.append(jnp.zeros((KEY_BLOCK, LANES), _BF16))
                    alpha_parts.append(jnp.ones((1, LANES), _F32))
                    continue
                m_new = functools.reduce(
                    jnp.maximum, [jnp.max(sb, axis=0, keepdims=True) for sb in sbs if sb is not None]
                    + ([] if m[c] is None else [m[c]]))
                alpha_parts.append(jnp.ones((1, LANES), _F32) if m[c] is None
                                   else jnp.exp2(m[c] - m_new))
                m[c] = m_new
                p_parts.append(jnp.concatenate(
                    [jnp.zeros((BAND_BLOCK, LANES), _BF16) if sb is None
                     else jnp.exp2(sb - m_new).astype(_BF16) for sb in sbs], axis=0))
            t, off = divmod(b0, tq)
            v_ext = jnp.concatenate(
                [v_refs[sub + t][hp, hh * HEAD_DIM:(hh + 1) * HEAD_DIM, off:off + KEY_BLOCK],
                 ones_rows[:, :KEY_BLOCK]], axis=0)
            part = jnp.dot(v_ext, jnp.concatenate(p_parts, axis=1),
                           preferred_element_type=_F32)
            acc = part if acc is None else acc * jnp.concatenate(alpha_parts, axis=1) + part
        return acc[:HEAD_DIM] * (1.0 / acc[HEAD_DIM:HEAD_DIM + 1])

    def run(tiles_per_sub):
        work = [(sub, tiles, head) for sub, tiles in enumerate(tiles_per_sub)
                for head in range(N_HEADS)]
        scored = [scores(*w) for w in work[:SCORE_LOOKAHEAD]]
        halves = []
        for n, (sub, tiles, head) in enumerate(work):
            if n + SCORE_LOOKAHEAD < len(work):
                scored.append(scores(*work[n + SCORE_LOOKAHEAD]))
            halves.append(attend(sub, tiles, head, scored.pop(0)))
            if head % 2:
                o_ref[head // 2, sub * tq:(sub + 1) * tq, :] = (
                    jnp.concatenate(halves, axis=0).T.astype(_BF16))
                halves = []

    all_tiles = tuple(range(KEY_TILES))
    first_step = tuple(all_tiles[max(KEY_TILES - 1 - j, 0):] for j in range(Q_TILES_PER_STEP))
    qs = pl.program_id(1)
    pl.when(qs == 0)(functools.partial(run, first_step))
    pl.when(qs > 0)(functools.partial(run, (all_tiles,) * Q_TILES_PER_STEP))


def _mix_mlp_kernel(x_ref, a_ref, yc_ref, g1_ref, bgate_ref, g2_ref,
                    wg_hbm, wap_hbm, wcp_hbm, wout_hbm, wup_hbm, wdown_hbm, o_ref,
                    wg_ref, wap_ref, wcp_ref, wout_ref, wup_ref, wdown_ref, stage, sem):
    @pl.when(pl.program_id(0) == 0)
    def _():
        pairs = ((wg_hbm, wg_ref), (wap_hbm, wap_ref), (wcp_hbm, wcp_ref), (wout_hbm, wout_ref),
                 (wup_hbm, wup_ref), (wdown_hbm, wdown_ref))
        _stage_weights([job for src, dst in pairs for job in _weight_jobs(src, dst)], stage, sem)

    def dot(lhs, w_ref, rows=slice(None), cols=slice(None)):
        return jnp.dot(lhs, _unpack(w_ref[rows, cols]), preferred_element_type=_F32)

    sub_rows = x_ref.shape[0] // MLP_SUB_TILES
    subs = [slice(n * sub_rows, (n + 1) * sub_rows) for n in range(MLP_SUB_TILES)]

    def branches(rows):
        x = x_ref[rows, :]
        h = _rms_norm_rows(x, g1_ref[...]).astype(_BF16)
        gates = jax.nn.sigmoid(dot(h, wg_ref) + bgate_ref[...])
        a = jnp.concatenate([a_ref[hp, rows, :] for hp in range(HEAD_PAIRS)], axis=-1)
        ya = dot(a, wap_ref)
        yc = dot(yc_ref[rows, :], wcp_ref)
        return x, (gates[:, :D_MODEL] * ya + gates[:, D_MODEL:] * yc).astype(_BF16)

    def residual(x, merged):
        return x + dot(merged, wout_ref)

    def mlp(rows, x1):
        h2 = _rms_norm_rows(x1, g2_ref[...]).astype(_BF16)
        acc = x1
        for f in range(D_FF // FF_TILE):
            up = dot(h2, wup_ref, cols=slice(f * FF_TILE, (f + 1) * FF_TILE))
            act = jnp.square(jnp.maximum(up, 0.0)).astype(_BF16)
            acc = acc + dot(act, wdown_ref, rows=slice(f * FF_TILE // 2, (f + 1) * FF_TILE // 2))
        o_ref[rows, :] = acc

    merged = [branches(rows) for rows in subs]
    x1 = [residual(*m) for m in merged]
    for rows, v in zip(subs, x1):
        mlp(rows, v)


def kernel(x, norm1_g, w_in, q_norm_g, k_norm_g, rel_bias, conv_w, conv_b, w_attn_proj,
           w_conv_proj, w_gate, b_gate, w_out, norm2_g, w_up, w_down):
    b, s, d = x.shape
    assert d == D_MODEL and s % ROW_TILE == 0 and s % PROJ_ROWS == 0
    assert s % (Q_TILE * Q_TILES_PER_STEP) == 0
    t = b * s
    xf = x.reshape(t, d)
    row = lambda v: v.reshape(1, -1).astype(_F32)
    cparams = functools.partial(pltpu.CompilerParams, vmem_limit_bytes=VMEM_LIMIT)

    gq = row(jnp.tile(q_norm_g, N_HEADS) * (HEAD_DIM ** -0.5 * LOG2E))
    gk = row(jnp.tile(k_norm_g, N_HEADS))
    staging = [pltpu.VMEM((STAGE_SLOTS, STAGE_ROWS, STAGE_COLS), _F32),
               pltpu.SemaphoreType.DMA((STAGE_SLOTS,))]
    heads_shape = jax.ShapeDtypeStruct((HEAD_PAIRS, t, LANES), _BF16)

    def heads_spec(rows):
        return pl.BlockSpec((HEAD_PAIRS, rows, LANES), lambda i: (0, i, 0))

    q, k, vt, yc = pl.pallas_call(
        functools.partial(_proj_kernel, tiles_per_seq=s // PROJ_ROWS),
        grid=(t // PROJ_ROWS,),
        in_specs=[
            pl.BlockSpec((PROJ_ROWS, d), lambda i: (i, 0)),
            _resident((1, d)),
            pl.BlockSpec(memory_space=pl.ANY),
            _resident((1, d)),
            _resident((1, d)),
            _resident((CONV_WIDTH, d)),
            _resident((1, d)),
        ],
        out_specs=[heads_spec(PROJ_ROWS), heads_spec(PROJ_ROWS),
                   pl.BlockSpec((HEAD_PAIRS, LANES, PROJ_ROWS), lambda i: (0, 0, i)),
                   pl.BlockSpec((PROJ_ROWS, d), lambda i: (i, 0))],
        out_shape=[heads_shape, heads_shape,
                   jax.ShapeDtypeStruct((HEAD_PAIRS, LANES, t), _BF16),
                   jax.ShapeDtypeStruct((t, d), _BF16)],
        scratch_shapes=[pltpu.VMEM((PROJ_ROWS + CARRY_ROWS, d), _F32),
                        _packed_weight((d, 2 * d)), _packed_weight((d, d)),
                        _packed_weight((d, 3 * d))] + staging,
        compiler_params=cparams(dimension_semantics=("arbitrary",)),
        name="proj",
    )(xf, row(norm1_g), w_in.astype(_F32), gq, gk, conv_w.astype(_F32), row(conv_b))

    n_rel = rel_bias.shape[1]
    rb_rows = jnp.pad(rel_bias.astype(_F32), ((0, 0), (0, BIAS_LANES - n_rel)))
    nq = s // Q_TILE
    ns = nq // Q_TILES_PER_STEP
    q_spec = pl.BlockSpec((HEAD_PAIRS, Q_TILES_PER_STEP * Q_TILE, LANES),
                          lambda bi, si: (0, bi * ns + si, 0))

    def key_tile(si, ref_idx):
        return jnp.maximum(si * Q_TILES_PER_STEP - (KEY_TILES - 1) + ref_idx, 0)

    def k_spec(ref_idx):
        return pl.BlockSpec((HEAD_PAIRS, Q_TILE, LANES),
                            lambda bi, si: (0, bi * nq + key_tile(si, ref_idx), 0))

    def vt_spec(ref_idx):
        return pl.BlockSpec((HEAD_PAIRS, LANES, Q_TILE),
                            lambda bi, si: (0, 0, bi * nq + key_tile(si, ref_idx)))

    attn = pl.pallas_call(
        _attn_kernel,
        grid=(b, ns),
        in_specs=[q_spec] + [k_spec(r) for r in range(KEY_REFS)]
                 + [vt_spec(r) for r in range(KEY_REFS)]
                 + [_resident((N_HEADS, 1, BIAS_LANES))],
        out_specs=q_spec,
        out_shape=heads_shape,
        scratch_shapes=[pltpu.VMEM((N_HEADS, KEY_TILES * Q_TILE, Q_TILE), _F32)],
        compiler_params=cparams(dimension_semantics=("arbitrary", "arbitrary")),
        name="attn",
    )(q, *([k] * KEY_REFS), *([vt] * KEY_REFS), rb_rows.reshape(N_HEADS, 1, BIAS_LANES))

    mlp_weights = [w.astype(_F32) for w in (w_gate, w_attn_proj, w_conv_proj, w_out, w_up, w_down)]
    out = pl.pallas_call(
        _mix_mlp_kernel,
        grid=(t // ROW_TILE,),
        in_specs=[
            pl.BlockSpec((ROW_TILE, d), lambda i: (i, 0)),
            heads_spec(ROW_TILE),
            pl.BlockSpec((ROW_TILE, d), lambda i: (i, 0)),
            _resident((1, d)),
            _resident((1, 2 * d)),
            _resident((1, d)),
        ] + [pl.BlockSpec(memory_space=pl.ANY)] * len(mlp_weights),
        out_specs=pl.BlockSpec((ROW_TILE, d), lambda i: (i, 0)),
        out_shape=jax.ShapeDtypeStruct((t, d), _F32),
        scratch_shapes=[_packed_weight(w.shape) for w in mlp_weights] + staging,
        compiler_params=cparams(dimension_semantics=("arbitrary",)),
        name="mix_mlp",
    )(xf, attn, yc, row(norm1_g), row(b_gate), row(norm2_g), *mlp_weights)
    return out.reshape(b, s, d)
```

```python
import functools
import math

import jax
import jax.numpy as jnp
from jax import lax
from jax.experimental import pallas as pl
from jax.experimental.pallas import tpu as pltpu

D_MODEL = 1024
N_HEADS = 16
HEAD_DIM = 64
CHUNK = 64
N_PREV_CHUNKS = 8
MAX_REL = 256
CONV_WIDTH = 3
D_FF = 4 * D_MODEL
EPS = 1e-6
NEG_INF = -1e30
LOG2E = math.log2(math.e)

LANES = 128
BF16_ROWS = 16
HEAD_PAIRS = D_MODEL // LANES
MXU_TILE = 256

PROJ_ROWS = 512
ROW_TILE = 512
PROJ_TILE = 2 * MXU_TILE
CONV_TILE = MXU_TILE
Q_TILE = 256
KEY_TILES = 1 + (N_PREV_CHUNKS * CHUNK) // Q_TILE
Q_TILES_PER_STEP = 2
KEY_REFS = KEY_TILES + Q_TILES_PER_STEP - 1
BAND_ROWS = (N_PREV_CHUNKS + LANES // CHUNK) * CHUNK
BIAS_LANES = (KEY_TILES + 1) * Q_TILE
assert 2 * MAX_REL + 1 <= BIAS_LANES and Q_TILE - 1 <= MAX_REL
SCORE_LOOKAHEAD = 3
KEY_BLOCK = LANES
FF_TILE = 1024
PROJ_SUB_TILES = 2
MLP_SUB_TILES = 2
CARRY_ROWS = 8
STAGE_ROWS, STAGE_COLS = 256, D_MODEL
STAGE_SLOTS = 4
VMEM_LIMIT = 56 * 1024 * 1024

_BF16 = jnp.bfloat16
_F32 = jnp.float32
_NT_DIMS = (((1,), (1,)), ((), ()))


def _resident(shape):
    return pl.BlockSpec(shape, lambda *_: (0,) * len(shape), pipeline_mode=pl.Buffered(1))


def _rms_norm_rows(x, g):
    ms = jnp.mean(x * x, axis=-1, keepdims=True)
    return x * lax.rsqrt(ms + EPS) * g


def _packed_weight(shape):
    k, n = shape
    return pltpu.VMEM((k // 2, n), jnp.uint32)


def _pack(block):
    return pltpu.bitcast(block.astype(_BF16), jnp.uint32)


def _unpack(words):
    return pltpu.bitcast(words, _BF16)


def _stage_weights(jobs, stage, sem):
    def copy(n):
        src, r0, c0, _ = jobs[n]
        return pltpu.make_async_copy(src.at[pl.ds(r0, STAGE_ROWS), pl.ds(c0, STAGE_COLS)],
                                     stage.at[n % STAGE_SLOTS], sem.at[n % STAGE_SLOTS])

    ahead = STAGE_SLOTS - 1
    for n in range(min(ahead, len(jobs))):
        copy(n).start(priority=n % 2)
    for n, job in enumerate(jobs):
        if n + ahead < len(jobs):
            copy(n + ahead).start(priority=(n + ahead) % 2)
        copy(n).wait()
        job[3](stage[n % STAGE_SLOTS])


def _weight_jobs(src, dst, row_lo=0, col_lo=0, shape=None):
    k, n = shape if shape is not None else src.shape
    jobs = []
    for r in range(0, k, STAGE_ROWS):
        for c in range(0, n, STAGE_COLS):
            def store(block, r=r, c=c):
                dst[r // 2:(r + STAGE_ROWS) // 2, c:c + STAGE_COLS] = _pack(block)
            jobs.append((src, row_lo + r, col_lo + c, store))
    return jobs


def _proj_kernel(x_ref, g1_ref, w_in_hbm, gq_ref, gk_ref, cw_ref, cb_ref,
                 q_ref, k_ref, vt_ref, yc_ref,
                 u_scr, wqk_ref, wvt_ref, wc_ref, stage, sem, *, tiles_per_seq):
    tm = x_ref.shape[0]

    @pl.when(pl.program_id(0) == 0)
    def _():
        def store_vt(block, r):
            wvt_ref[:, r:r + STAGE_ROWS] = _pack(block.T)

        jobs = _weight_jobs(w_in_hbm, wqk_ref, 0, 0, (D_MODEL, 2 * D_MODEL))
        jobs += _weight_jobs(w_in_hbm, wc_ref, 0, 3 * D_MODEL, (D_MODEL, 3 * D_MODEL))
        jobs += [(w_in_hbm, r, 2 * D_MODEL, functools.partial(store_vt, r=r))
                 for r in range(0, D_MODEL, STAGE_ROWS)]
        _stage_weights(jobs, stage, sem)

    n_col_tiles = D_MODEL // PROJ_TILE
    slabs = PROJ_TILE // LANES

    @pl.when(pl.program_id(0) % tiles_per_seq == 0)
    def _():
        u_scr[0:CARRY_ROWS, :] = jnp.zeros((CARRY_ROWS, D_MODEL), _F32)

    def sub_tile(r0, rows):
        tok = slice(r0, r0 + rows)
        h = _rms_norm_rows(x_ref[tok, :], g1_ref[...]).astype(_BF16)
        low_half = lax.broadcasted_iota(jnp.int32, (rows, LANES), 1) < HEAD_DIM

        def proj(w_ref, j, c, width=PROJ_TILE):
            lo = j * D_MODEL + c * width
            return jnp.dot(h, _unpack(w_ref[:, lo:lo + width]), preferred_element_type=_F32)

        for c in range(D_MODEL // CONV_TILE):
            cols = slice(c * CONV_TILE, (c + 1) * CONV_TILE)
            bg = proj(wc_ref, 0, c, CONV_TILE)
            u = proj(wc_ref, 1, c, CONV_TILE) * proj(wc_ref, 2, c, CONV_TILE)
            base = CARRY_ROWS + r0
            u_scr[base:base + rows, cols] = u
            u1 = u_scr[base - 1:base - 1 + rows, cols]
            u2 = u_scr[base - 2:base - 2 + rows, cols]
            conv = (cb_ref[:, cols] + cw_ref[0:1, cols] * u2 + cw_ref[1:2, cols] * u1
                    + cw_ref[2:3, cols] * u)
            yc_ref[tok, cols] = (bg * conv).astype(_BF16)

        def finish_head_norm(p, g_ref, o_ref, c):
            g = g_ref[:, c * PROJ_TILE:(c + 1) * PROJ_TILE]
            for sl in range(slabs):
                ps = p[:, sl * LANES:(sl + 1) * LANES]
                p2 = ps * ps
                ss_lo = jnp.sum(jnp.where(low_half, p2, 0.0), axis=-1, keepdims=True)
                ss_hi = jnp.sum(jnp.where(low_half, 0.0, p2), axis=-1, keepdims=True)
                ss = jnp.where(low_half, ss_lo, ss_hi)
                pn = ps * lax.rsqrt(ss * (1.0 / HEAD_DIM) + EPS) * g[:, sl * LANES:(sl + 1) * LANES]
                o_ref[c * slabs + sl, tok, :] = pn.astype(_BF16)

        waiting = None
        for j, (g_ref, o_ref) in enumerate(((gq_ref, q_ref), (gk_ref, k_ref))):
            for c in range(n_col_tiles):
                p = proj(wqk_ref, j, c)
                if waiting is not None:
                    finish_head_norm(*waiting)
                waiting = (p, g_ref, o_ref, c)

        for c in range(n_col_tiles):
            w_rows = slice(c * PROJ_TILE // 2, (c + 1) * PROJ_TILE // 2)
            vt = lax.dot_general(_unpack(wvt_ref[w_rows, :]), h, _NT_DIMS,
                                 preferred_element_type=_F32).astype(_BF16)
            if waiting is not None:
                finish_head_norm(*waiting)
                waiting = None
            for sl in range(slabs):
                vt_ref[c * slabs + sl, :, tok] = vt[sl * LANES:(sl + 1) * LANES, :]

    rows = tm // PROJ_SUB_TILES
    for n in range(PROJ_SUB_TILES):
        sub_tile(n * rows, rows)
    u_scr[0:CARRY_ROWS, :] = u_scr[tm:tm + CARRY_ROWS, :]


def _bias_tile(rb):
    kw, tq = KEY_TILES * Q_TILE, Q_TILE
    lane = lax.broadcasted_iota(jnp.int32, rb.shape, 1)
    top = rb[:, 2 * MAX_REL:2 * MAX_REL + 1]
    fwd = jnp.where(lane > 2 * MAX_REL, top, rb) * LOG2E
    rolled = pltpu.roll(jnp.broadcast_to(fwd, (kw, BIAS_LANES)), tq, axis=1,
                        stride=1, stride_axis=0)
    kj = lax.broadcasted_iota(jnp.int32, (kw, tq), 0)
    qi = lax.broadcasted_iota(jnp.int32, (kw, tq), 1)
    dchunk = (qi // CHUNK + N_PREV_CHUNKS) - kj // CHUNK
    band = (dchunk >= 0) & (dchunk <= N_PREV_CHUNKS)
    return jnp.where(band, rolled[:, :tq], NEG_INF)


def _attn_kernel(q_ref, *refs):
    k_refs, v_refs = refs[:KEY_REFS], refs[KEY_REFS:2 * KEY_REFS]
    rb_ref, o_ref, bias_ref = refs[2 * KEY_REFS:]
    tq = Q_TILE
    lane = lax.broadcasted_iota(jnp.int32, (tq, LANES), 1)
    ones_rows = jnp.ones((BF16_ROWS, tq), _BF16)

    @pl.when((pl.program_id(0) == 0) & (pl.program_id(1) == 0))
    def _():
        def build(head, carry):
            bias_ref[head] = _bias_tile(rb_ref[head])
            return carry
        lax.fori_loop(0, N_HEADS, build, 0)

    def scores(sub, tiles, head):
        hp, hh = divmod(head, 2)
        sel = (lane < HEAD_DIM) if hh == 0 else (lane >= HEAD_DIM)
        q = q_ref[hp, sub * tq:(sub + 1) * tq, :]
        qh = jnp.where(sel, q, jnp.zeros_like(q))
        keys = jnp.concatenate([k_refs[sub + t][hp] for t in tiles], axis=0)
        return lax.dot_general(keys, qh, _NT_DIMS, preferred_element_type=_F32)

    def attend(sub, tiles, head, s):
        lo, hi = tiles[0] * tq, (tiles[-1] + 1) * tq
        hp, hh = divmod(head, 2)
        n_cols = tq // LANES
        m = [None] * n_cols
        acc = None
        for b0 in range(lo, hi, KEY_BLOCK):
            p_parts, alpha_parts = [], []
            for c in range(n_cols):
                if not (max(lo, c * LANES) <= b0 < min(hi, c * LANES + BAND_ROWS)):
                    p_parts.append(jnp.zeros((KEY_BLOCK, LANES), _BF16))
                    alpha_parts.append(jnp.ones((1, LANES), _F32))
                    continue
                sb = (s[b0 - lo:b0 - lo + KEY_BLOCK, c * LANES:(c + 1) * LANES]
                      + bias_ref[head, b0:b0 + KEY_BLOCK, c * LANES:(c + 1) * LANES])
                mb = jnp.max(sb, axis=0, keepdims=True)
                if m[c] is None:
                    m_new = mb
                    alpha_parts.append(jnp.ones((1, LANES), _F32))
                else:
                    m_new = jnp.maximum(m[c], mb)
                    alpha_parts.append(jnp.exp2(m[c] - m_new))
                m[c] = m_new
                p_parts.append(jnp.exp2(sb - m_new).astype(_BF16))
            t, off = divmod(b0, tq)
            v_ext = jnp.concatenate(
                [v_refs[sub + t][hp, hh * HEAD_DIM:(hh + 1) * HEAD_DIM, off:off + KEY_BLOCK],
                 ones_rows[:, :KEY_BLOCK]], axis=0)
            part = jnp.dot(v_ext, jnp.concatenate(p_parts, axis=1),
                           preferred_element_type=_F32)
            acc = part if acc is None else acc * jnp.concatenate(alpha_parts, axis=1) + part
        return acc[:HEAD_DIM] * (1.0 / acc[HEAD_DIM:HEAD_DIM + 1])

    def run(tiles_per_sub):
        work = [(sub, tiles, head) for sub, tiles in enumerate(tiles_per_sub)
                for head in range(N_HEADS)]
        scored = [scores(*w) for w in work[:SCORE_LOOKAHEAD]]
        halves = []
        for n, (sub, tiles, head) in enumerate(work):
            if n + SCORE_LOOKAHEAD < len(work):
                scored.append(scores(*work[n + SCORE_LOOKAHEAD]))
            halves.append(attend(sub, tiles, head, scored.pop(0)))
            if head % 2:
                o_ref[head // 2, sub * tq:(sub + 1) * tq, :] = (
                    jnp.concatenate(halves, axis=0).T.astype(_BF16))
                halves = []

    all_tiles = tuple(range(KEY_TILES))
    first_step = tuple(all_tiles[max(KEY_TILES - 1 - j, 0):] for j in range(Q_TILES_PER_STEP))
    qs = pl.program_id(1)
    pl.when(qs == 0)(functools.partial(run, first_step))
    pl.when(qs > 0)(functools.partial(run, (all_tiles,) * Q_TILES_PER_STEP))


def _mix_mlp_kernel(x_ref, a_ref, yc_ref, g1_ref, bgate_ref, g2_ref,
                    wg_hbm, wap_hbm, wcp_hbm, wout_hbm, wup_hbm, wdown_hbm, o_ref,
                    wg_ref, wap_ref, wcp_ref, wout_ref, wup_ref, wdown_ref, stage, sem):
    @pl.when(pl.program_id(0) == 0)
    def _():
        pairs = ((wg_hbm, wg_ref), (wap_hbm, wap_ref), (wcp_hbm, wcp_ref), (wout_hbm, wout_ref),
                 (wup_hbm, wup_ref), (wdown_hbm, wdown_ref))
        _stage_weights([job for src, dst in pairs for job in _weight_jobs(src, dst)], stage, sem)

    def dot(lhs, w_ref, rows=slice(None), cols=slice(None)):
        return jnp.dot(lhs, _unpack(w_ref[rows, cols]), preferred_element_type=_F32)

    sub_rows = x_ref.shape[0] // MLP_SUB_TILES
    subs = [slice(n * sub_rows, (n + 1) * sub_rows) for n in range(MLP_SUB_TILES)]

    def branches(rows):
        x = x_ref[rows, :]
        h = _rms_norm_rows(x, g1_ref[...]).astype(_BF16)
        gates = jax.nn.sigmoid(dot(h, wg_ref) + bgate_ref[...])
        a = jnp.concatenate([a_ref[hp, rows, :] for hp in range(HEAD_PAIRS)], axis=-1)
        ya = dot(a, wap_ref)
        yc = dot(yc_ref[rows, :], wcp_ref)
        return x, (gates[:, :D_MODEL] * ya + gates[:, D_MODEL:] * yc).astype(_BF16)

    def residual(x, merged):
        return x + dot(merged, wout_ref)

    def mlp(rows, x1):
        h2 = _rms_norm_rows(x1, g2_ref[...]).astype(_BF16)
        acc = x1
        for f in range(D_FF // FF_TILE):
            up = dot(h2, wup_ref, cols=slice(f * FF_TILE, (f + 1) * FF_TILE))
            act = jnp.square(jnp.maximum(up, 0.0)).astype(_BF16)
            acc = acc + dot(act, wdown_ref, rows=slice(f * FF_TILE // 2, (f + 1) * FF_TILE // 2))
        o_ref[rows, :] = acc

    merged = [branches(rows) for rows in subs]
    x1 = [residual(*m) for m in merged]
    for rows, v in zip(subs, x1):
        mlp(rows, v)


def kernel(x, norm1_g, w_in, q_norm_g, k_norm_g, rel_bias, conv_w, conv_b, w_attn_proj,
           w_conv_proj, w_gate, b_gate, w_out, norm2_g, w_up, w_down):
    b, s, d = x.shape
    assert d == D_MODEL and s % ROW_TILE == 0 and s % PROJ_ROWS == 0
    assert s % (Q_TILE * Q_TILES_PER_STEP) == 0
    t = b * s
    xf = x.reshape(t, d)
    row = lambda v: v.reshape(1, -1).astype(_F32)
    cparams = functools.partial(pltpu.CompilerParams, vmem_limit_bytes=VMEM_LIMIT)

    gq = row(jnp.tile(q_norm_g, N_HEADS) * (HEAD_DIM ** -0.5 * LOG2E))
    gk = row(jnp.tile(k_norm_g, N_HEADS))
    staging = [pltpu.VMEM((STAGE_SLOTS, STAGE_ROWS, STAGE_COLS), _F32),
               pltpu.SemaphoreType.DMA((STAGE_SLOTS,))]
    heads_shape = jax.ShapeDtypeStruct((HEAD_PAIRS, t, LANES), _BF16)

    def heads_spec(rows):
        return pl.BlockSpec((HEAD_PAIRS, rows, LANES), lambda i: (0, i, 0))

    q, k, vt, yc = pl.pallas_call(
        functools.partial(_proj_kernel, tiles_per_seq=s // PROJ_ROWS),
        grid=(t // PROJ_ROWS,),
        in_specs=[
            pl.BlockSpec((PROJ_ROWS, d), lambda i: (i, 0)),
            _resident((1, d)),
            pl.BlockSpec(memory_space=pl.ANY),
            _resident((1, d)),
            _resident((1, d)),
            _resident((CONV_WIDTH, d)),
            _resident((1, d)),
        ],
        out_specs=[heads_spec(PROJ_ROWS), heads_spec(PROJ_ROWS),
                   pl.BlockSpec((HEAD_PAIRS, LANES, PROJ_ROWS), lambda i: (0, 0, i)),
                   pl.BlockSpec((PROJ_ROWS, d), lambda i: (i, 0))],
        out_shape=[heads_shape, heads_shape,
                   jax.ShapeDtypeStruct((HEAD_PAIRS, LANES, t), _BF16),
                   jax.ShapeDtypeStruct((t, d), _BF16)],
        scratch_shapes=[pltpu.VMEM((PROJ_ROWS + CARRY_ROWS, d), _F32),
                        _packed_weight((d, 2 * d)), _packed_weight((d, d)),
                        _packed_weight((d, 3 * d))] + staging,
        compiler_params=cparams(dimension_semantics=("arbitrary",)),
        name="proj",
    )(xf, row(norm1_g), w_in.astype(_F32), gq, gk, conv_w.astype(_F32), row(conv_b))

    n_rel = rel_bias.shape[1]
    rb_rows = jnp.pad(rel_bias.astype(_F32), ((0, 0), (0, BIAS_LANES - n_rel)))
    nq = s // Q_TILE
    ns = nq // Q_TILES_PER_STEP
    q_spec = pl.BlockSpec((HEAD_PAIRS, Q_TILES_PER_STEP * Q_TILE, LANES),
                          lambda bi, si: (0, bi * ns + si, 0))

    def key_tile(si, ref_idx):
        return jnp.maximum(si * Q_TILES_PER_STEP - (KEY_TILES - 1) + ref_idx, 0)

    def k_spec(ref_idx):
        return pl.BlockSpec((HEAD_PAIRS, Q_TILE, LANES),
                            lambda bi, si: (0, bi * nq + key_tile(si, ref_idx), 0))

    def vt_spec(ref_idx):
        return pl.BlockSpec((HEAD_PAIRS, LANES, Q_TILE),
                            lambda bi, si: (0, 0, bi * nq + key_tile(si, ref_idx)))

    attn = pl.pallas_call(
        _attn_kernel,
        grid=(b, ns),
        in_specs=[q_spec] + [k_spec(r) for r in range(KEY_REFS)]
                 + [vt_spec(r) for r in range(KEY_REFS)]
                 + [_resident((N_HEADS, 1, BIAS_LANES))],
        out_specs=q_spec,
        out_shape=heads_shape,
        scratch_shapes=[pltpu.VMEM((N_HEADS, KEY_TILES * Q_TILE, Q_TILE), _F32)],
        compiler_params=cparams(dimension_semantics=("arbitrary", "arbitrary")),
        name="attn",
    )(q, *([k] * KEY_REFS), *([vt] * KEY_REFS), rb_rows.reshape(N_HEADS, 1, BIAS_LANES))

    mlp_weights = [w.astype(_F32) for w in (w_gate, w_attn_proj, w_conv_proj, w_out, w_up, w_down)]
    out = pl.pallas_call(
        _mix_mlp_kernel,
        grid=(t // ROW_TILE,),
        in_specs=[
            pl.BlockSpec((ROW_TILE, d), lambda i: (i, 0)),
            heads_spec(ROW_TILE),
            pl.BlockSpec((ROW_TILE, d), lambda i: (i, 0)),
            _resident((1, d)),
            _resident((1, 2 * d)),
            _resident((1, d)),
        ] + [pl.BlockSpec(memory_space=pl.ANY)] * len(mlp_weights),
        out_specs=pl.BlockSpec((ROW_TILE, d), lambda i: (i, 0)),
        out_shape=jax.ShapeDtypeStruct((t, d), _F32),
        scratch_shapes=[_packed_weight(w.shape) for w in mlp_weights] + staging,
        compiler_params=cparams(dimension_semantics=("arbitrary",)),
        name="mix_mlp",
    )(xf, attn, yc, row(norm1_g), row(b_gate), row(norm2_g), *mlp_weights)
    return out.reshape(b, s, d)
```

```python
import functools
import math

import jax
import jax.numpy as jnp
from jax import lax
from jax.experimental import pallas as pl
from jax.experimental.pallas import tpu as pltpu

D_MODEL = 1024
N_HEADS = 16
HEAD_DIM = 64
CHUNK = 64
N_PREV_CHUNKS = 8
MAX_REL = 256
CONV_WIDTH = 3
D_FF = 4 * D_MODEL
EPS = 1e-6
NEG_INF = -1e30
LOG2E = math.log2(math.e)

LANES = 128
BF16_ROWS = 16
HEAD_PAIRS = D_MODEL // LANES
MXU_TILE = 256

PROJ_ROWS = 512
ROW_TILE = 512
PROJ_TILE = 2 * MXU_TILE
CONV_TILE = MXU_TILE
Q_TILE = 256
KEY_TILES = 1 + (N_PREV_CHUNKS * CHUNK) // Q_TILE
Q_TILES_PER_STEP = 2
KEY_REFS = KEY_TILES + Q_TILES_PER_STEP - 1
BAND_ROWS = (N_PREV_CHUNKS + LANES // CHUNK) * CHUNK
BIAS_LANES = (KEY_TILES + 1) * Q_TILE
assert 2 * MAX_REL + 1 <= BIAS_LANES and Q_TILE - 1 <= MAX_REL
SCORE_LOOKAHEAD = 3
KEY_BLOCK = LANES
FF_TILE = 1024
PROJ_SUB_TILES = 2
MLP_SUB_TILES = 2
CARRY_ROWS = 8
STAGE_ROWS, STAGE_COLS = 256, D_MODEL
STAGE_SLOTS = 4
VMEM_LIMIT = 56 * 1024 * 1024

_BF16 = jnp.bfloat16
_F32 = jnp.float32
_NT_DIMS = (((1,), (1,)), ((), ()))


def _resident(shape):
    return pl.BlockSpec(shape, lambda *_: (0,) * len(shape), pipeline_mode=pl.Buffered(1))


def _rms_norm_rows(x, g):
    ms = jnp.mean(x * x, axis=-1, keepdims=True)
    return x * lax.rsqrt(ms + EPS) * g


def _packed_weight(shape):
    k, n = shape
    return pltpu.VMEM((k // 2, n), jnp.uint32)


def _pack(block):
    return pltpu.bitcast(block.astype(_BF16), jnp.uint32)


def _unpack(words):
    return pltpu.bitcast(words, _BF16)


def _stage_weights(jobs, stage, sem):
    def copy(n):
        src, r0, c0, _ = jobs[n]
        return pltpu.make_async_copy(src.at[pl.ds(r0, STAGE_ROWS), pl.ds(c0, STAGE_COLS)],
                                     stage.at[n % STAGE_SLOTS], sem.at[n % STAGE_SLOTS])

    ahead = STAGE_SLOTS - 1
    for n in range(min(ahead, len(jobs))):
        copy(n).start(priority=n % 2)
    for n, job in enumerate(jobs):
        if n + ahead < len(jobs):
            copy(n + ahead).start(priority=(n + ahead) % 2)
        copy(n).wait()
        job[3](stage[n % STAGE_SLOTS])


def _weight_jobs(src, dst, row_lo=0, col_lo=0, shape=None):
    k, n = shape if shape is not None else src.shape
    jobs = []
    for r in range(0, k, STAGE_ROWS):
        for c in range(0, n, STAGE_COLS):
            def store(block, r=r, c=c):
                dst[r // 2:(r + STAGE_ROWS) // 2, c:c + STAGE_COLS] = _pack(block)
            jobs.append((src, row_lo + r, col_lo + c, store))
    return jobs


def _proj_kernel(x_ref, g1_ref, w_in_hbm, gq_ref, gk_ref, cw_ref, cb_ref,
                 q_ref, k_ref, vt_ref, yc_ref,
                 u_scr, wqk_ref, wvt_ref, wc_ref, stage, sem, *, tiles_per_seq):
    tm = x_ref.shape[0]

    @pl.when(pl.program_id(0) == 0)
    def _():
        def store_vt(block, r):
            wvt_ref[:, r:r + STAGE_ROWS] = _pack(block.T)

        jobs = _weight_jobs(w_in_hbm, wqk_ref, 0, 0, (D_MODEL, 2 * D_MODEL))
        jobs += _weight_jobs(w_in_hbm, wc_ref, 0, 3 * D_MODEL, (D_MODEL, 3 * D_MODEL))
        jobs += [(w_in_hbm, r, 2 * D_MODEL, functools.partial(store_vt, r=r))
                 for r in range(0, D_MODEL, STAGE_ROWS)]
        _stage_weights(jobs, stage, sem)

    n_col_tiles = D_MODEL // PROJ_TILE
    slabs = PROJ_TILE // LANES

    @pl.when(pl.program_id(0) % tiles_per_seq == 0)
    def _():
        u_scr[0:CARRY_ROWS, :] = jnp.zeros((CARRY_ROWS, D_MODEL), _F32)

    def sub_tile(r0, rows):
        tok = slice(r0, r0 + rows)
        h = _rms_norm_rows(x_ref[tok, :], g1_ref[...]).astype(_BF16)
        low_half = lax.broadcasted_iota(jnp.int32, (rows, LANES), 1) < HEAD_DIM

        def proj(w_ref, j, c, width=PROJ_TILE):
            lo = j * D_MODEL + c * width
            return jnp.dot(h, _unpack(w_ref[:, lo:lo + width]), preferred_element_type=_F32)

        for c in range(D_MODEL // CONV_TILE):
            cols = slice(c * CONV_TILE, (c + 1) * CONV_TILE)
            bg = proj(wc_ref, 0, c, CONV_TILE)
            u = proj(wc_ref, 1, c, CONV_TILE) * proj(wc_ref, 2, c, CONV_TILE)
            base = CARRY_ROWS + r0
            u_scr[base:base + rows, cols] = u
            u1 = u_scr[base - 1:base - 1 + rows, cols]
            u2 = u_scr[base - 2:base - 2 + rows, cols]
            conv = (cb_ref[:, cols] + cw_ref[0:1, cols] * u2 + cw_ref[1:2, cols] * u1
                    + cw_ref[2:3, cols] * u)
            yc_ref[tok, cols] = (bg * conv).astype(_BF16)

        def finish_head_norm(p, g_ref, o_ref, c):
            g = g_ref[:, c * PROJ_TILE:(c + 1) * PROJ_TILE]
            for sl in range(slabs):
                ps = p[:, sl * LANES:(sl + 1) * LANES]
                p2 = ps * ps
                ss_lo = jnp.sum(jnp.where(low_half, p2, 0.0), axis=-1, keepdims=True)
                ss_hi = jnp.sum(jnp.where(low_half, 0.0, p2), axis=-1, keepdims=True)
                ss = jnp.where(low_half, ss_lo, ss_hi)
                pn = ps * lax.rsqrt(ss * (1.0 / HEAD_DIM) + EPS) * g[:, sl * LANES:(sl + 1) * LANES]
                o_ref[c * slabs + sl, tok, :] = pn.astype(_BF16)

        waiting = None
        for j, (g_ref, o_ref) in enumerate(((gq_ref, q_ref), (gk_ref, k_ref))):
            for c in range(n_col_tiles):
                p = proj(wqk_ref, j, c)
                if waiting is not None:
                    finish_head_norm(*waiting)
                waiting = (p, g_ref, o_ref, c)

        for c in range(n_col_tiles):
            w_rows = slice(c * PROJ_TILE // 2, (c + 1) * PROJ_TILE // 2)
            vt = lax.dot_general(_unpack(wvt_ref[w_rows, :]), h, _NT_DIMS,
                                 preferred_element_type=_F32).astype(_BF16)
            if waiting is not None:
                finish_head_norm(*waiting)
                waiting = None
            for sl in range(slabs):
                vt_ref[c * slabs + sl, :, tok] = vt[sl * LANES:(sl + 1) * LANES, :]

    rows = tm // PROJ_SUB_TILES
    for n in range(PROJ_SUB_TILES):
        sub_tile(n * rows, rows)
    u_scr[0:CARRY_ROWS, :] = u_scr[tm:tm + CARRY_ROWS, :]


def _bias_tile(rb):
    lane = lax.broadcasted_iota(jnp.int32, rb.shape, 1)
    top = rb[:, 2 * MAX_REL:2 * MAX_REL + 1]
    fwd = jnp.where(lane > 2 * MAX_REL, top, rb) * LOG2E
    rolled = pltpu.roll(jnp.broadcast_to(fwd, (BAND_ROWS, BIAS_LANES)), Q_TILE, axis=1,
                        stride=1, stride_axis=0)
    kj = lax.broadcasted_iota(jnp.int32, (BAND_ROWS, LANES), 0)
    qi = lax.broadcasted_iota(jnp.int32, (BAND_ROWS, LANES), 1)
    dchunk = (qi // CHUNK + N_PREV_CHUNKS) - kj // CHUNK
    band = (dchunk >= 0) & (dchunk <= N_PREV_CHUNKS)
    return jnp.where(band, rolled[:, :LANES], NEG_INF)


def _attn_kernel(q_ref, *refs):
    k_refs, v_refs = refs[:KEY_REFS], refs[KEY_REFS:2 * KEY_REFS]
    rb_ref, o_ref, bias_ref = refs[2 * KEY_REFS:]
    tq = Q_TILE
    lane = lax.broadcasted_iota(jnp.int32, (tq, LANES), 1)
    ones_rows = jnp.ones((BF16_ROWS, tq), _BF16)

    @pl.when((pl.program_id(0) == 0) & (pl.program_id(1) == 0))
    def _():
        def build(head, carry):
            bias_ref[head] = _bias_tile(rb_ref[head])
            return carry
        lax.fori_loop(0, N_HEADS, build, 0)

    def scores(sub, tiles, head):
        hp, hh = divmod(head, 2)
        sel = (lane < HEAD_DIM) if hh == 0 else (lane >= HEAD_DIM)
        q = q_ref[hp, sub * tq:(sub + 1) * tq, :]
        qh = jnp.where(sel, q, jnp.zeros_like(q))
        keys = jnp.concatenate([k_refs[sub + t][hp] for t in tiles], axis=0)
        return lax.dot_general(keys, qh, _NT_DIMS, preferred_element_type=_F32)

    def attend(sub, tiles, head, s):
        lo, hi = tiles[0] * tq, (tiles[-1] + 1) * tq
        hp, hh = divmod(head, 2)
        n_cols = tq // LANES
        m = [None] * n_cols
        acc = None
        for b0 in range(lo, hi, KEY_BLOCK):
            p_parts, alpha_parts = [], []
            for c in range(n_cols):
                if not (max(lo, c * LANES) <= b0 < min(hi, c * LANES + BAND_ROWS)):
                    p_parts.append(jnp.zeros((KEY_BLOCK, LANES), _BF16))
                    alpha_parts.append(jnp.ones((1, LANES), _F32))
                    continue
                sb = (s[b0 - lo:b0 - lo + KEY_BLOCK, c * LANES:(c + 1) * LANES]
                      + bias_ref[head, b0 - c * LANES:b0 - c * LANES + KEY_BLOCK, :])
                mb = jnp.max(sb, axis=0, keepdims=True)
                if m[c] is None:
                    m_new = mb
                    alpha_parts.append(jnp.ones((1, LANES), _F32))
                else:
                    m_new = jnp.maximum(m[c], mb)
                    alpha_parts.append(jnp.exp2(m[c] - m_new))
                m[c] = m_new
                p_parts.append(jnp.exp2(sb - m_new).astype(_BF16))
            t, off = divmod(b0, tq)
            v_ext = jnp.concatenate(
                [v_refs[sub + t][hp, hh * HEAD_DIM:(hh + 1) * HEAD_DIM, off:off + KEY_BLOCK],
                 ones_rows[:, :KEY_BLOCK]], axis=0)
            part = jnp.dot(v_ext, jnp.concatenate(p_parts, axis=1),
                           preferred_element_type=_F32)
            acc = part if acc is None else acc * jnp.concatenate(alpha_parts, axis=1) + part
        return acc[:HEAD_DIM] * (1.0 / acc[HEAD_DIM:HEAD_DIM + 1])

    def run(tiles_per_sub):
        work = [(sub, tiles, head) for sub, tiles in enumerate(tiles_per_sub)
                for head in range(N_HEADS)]
        scored = [scores(*w) for w in work[:SCORE_LOOKAHEAD]]
        halves = []
        for n, (sub, tiles, head) in enumerate(work):
            if n + SCORE_LOOKAHEAD < len(work):
                scored.append(scores(*work[n + SCORE_LOOKAHEAD]))
            halves.append(attend(sub, tiles, head, scored.pop(0)))
            if head % 2:
                o_ref[head // 2, sub * tq:(sub + 1) * tq, :] = (
                    jnp.concatenate(halves, axis=0).T.astype(_BF16))
                halves = []

    all_tiles = tuple(range(KEY_TILES))
    first_step = tuple(all_tiles[max(KEY_TILES - 1 - j, 0):] for j in range(Q_TILES_PER_STEP))
    qs = pl.program_id(1)
    pl.when(qs == 0)(functools.partial(run, first_step))
    pl.when(qs > 0)(functools.partial(run, (all_tiles,) * Q_TILES_PER_STEP))


def _mix_mlp_kernel(x_ref, a_ref, yc_ref, g1_ref, bgate_ref, g2_ref,
                    wg_hbm, wap_hbm, wcp_hbm, wout_hbm, wup_hbm, wdown_hbm, o_ref,
                    wg_ref, wap_ref, wcp_ref, wout_ref, wup_ref, wdown_ref, stage, sem):
    @pl.when(pl.program_id(0) == 0)
    def _():
        pairs = ((wg_hbm, wg_ref), (wap_hbm, wap_ref), (wcp_hbm, wcp_ref), (wout_hbm, wout_ref),
                 (wup_hbm, wup_ref), (wdown_hbm, wdown_ref))
        _stage_weights([job for src, dst in pairs for job in _weight_jobs(src, dst)], stage, sem)

    def dot(lhs, w_ref, rows=slice(None), cols=slice(None)):
        return jnp.dot(lhs, _unpack(w_ref[rows, cols]), preferred_element_type=_F32)

    sub_rows = x_ref.shape[0] // MLP_SUB_TILES
    subs = [slice(n * sub_rows, (n + 1) * sub_rows) for n in range(MLP_SUB_TILES)]

    def branches(rows):
        x = x_ref[rows, :]
        h = _rms_norm_rows(x, g1_ref[...]).astype(_BF16)
        gates = jax.nn.sigmoid(dot(h, wg_ref) + bgate_ref[...])
        a = jnp.concatenate([a_ref[hp, rows, :] for hp in range(HEAD_PAIRS)], axis=-1)
        ya = dot(a, wap_ref)
        yc = dot(yc_ref[rows, :], wcp_ref)
        return x, (gates[:, :D_MODEL] * ya + gates[:, D_MODEL:] * yc).astype(_BF16)

    def residual(x, merged):
        return x + dot(merged, wout_ref)

    def mlp(rows, x1):
        h2 = _rms_norm_rows(x1, g2_ref[...]).astype(_BF16)
        acc = x1
        for f in range(D_FF // FF_TILE):
            up = dot(h2, wup_ref, cols=slice(f * FF_TILE, (f + 1) * FF_TILE))
            act = jnp.square(jnp.maximum(up, 0.0)).astype(_BF16)
            acc = acc + dot(act, wdown_ref, rows=slice(f * FF_TILE // 2, (f + 1) * FF_TILE // 2))
        o_ref[rows, :] = acc

    merged = [branches(rows) for rows in subs]
    x1 = [residual(*m) for m in merged]
    for rows, v in zip(subs, x1):
        mlp(rows, v)


def kernel(x, norm1_g, w_in, q_norm_g, k_norm_g, rel_bias, conv_w, conv_b, w_attn_proj,
           w_conv_proj, w_gate, b_gate, w_out, norm2_g, w_up, w_down):
    b, s, d = x.shape
    assert d == D_MODEL and s % ROW_TILE == 0 and s % PROJ_ROWS == 0
    assert s % (Q_TILE * Q_TILES_PER_STEP) == 0
    t = b * s
    xf = x.reshape(t, d)
    row = lambda v: v.reshape(1, -1).astype(_F32)
    cparams = functools.partial(pltpu.CompilerParams, vmem_limit_bytes=VMEM_LIMIT)

    gq = row(jnp.tile(q_norm_g, N_HEADS) * (HEAD_DIM ** -0.5 * LOG2E))
    gk = row(jnp.tile(k_norm_g, N_HEADS))
    staging = [pltpu.VMEM((STAGE_SLOTS, STAGE_ROWS, STAGE_COLS), _F32),
               pltpu.SemaphoreType.DMA((STAGE_SLOTS,))]
    heads_shape = jax.ShapeDtypeStruct((HEAD_PAIRS, t, LANES), _BF16)

    def heads_spec(rows):
        return pl.BlockSpec((HEAD_PAIRS, rows, LANES), lambda i: (0, i, 0))

    q, k, vt, yc = pl.pallas_call(
        functools.partial(_proj_kernel, tiles_per_seq=s // PROJ_ROWS),
        grid=(t // PROJ_ROWS,),
        in_specs=[
            pl.BlockSpec((PROJ_ROWS, d), lambda i: (i, 0)),
            _resident((1, d)),
            pl.BlockSpec(memory_space=pl.ANY),
            _resident((1, d)),
            _resident((1, d)),
            _resident((CONV_WIDTH, d)),
            _resident((1, d)),
        ],
        out_specs=[heads_spec(PROJ_ROWS), heads_spec(PROJ_ROWS),
                   pl.BlockSpec((HEAD_PAIRS, LANES, PROJ_ROWS), lambda i: (0, 0, i)),
                   pl.BlockSpec((PROJ_ROWS, d), lambda i: (i, 0))],
        out_shape=[heads_shape, heads_shape,
                   jax.ShapeDtypeStruct((HEAD_PAIRS, LANES, t), _BF16),
                   jax.ShapeDtypeStruct((t, d), _BF16)],
        scratch_shapes=[pltpu.VMEM((PROJ_ROWS + CARRY_ROWS, d), _F32),
                        _packed_weight((d, 2 * d)), _packed_weight((d, d)),
                        _packed_weight((d, 3 * d))] + staging,
        compiler_params=cparams(dimension_semantics=("arbitrary",)),
        name="proj",
    )(xf, row(norm1_g), w_in.astype(_F32), gq, gk, conv_w.astype(_F32), row(conv_b))

    n_rel = rel_bias.shape[1]
    rb_rows = jnp.pad(rel_bias.astype(_F32), ((0, 0), (0, BIAS_LANES - n_rel)))
    nq = s // Q_TILE
    ns = nq // Q_TILES_PER_STEP
    q_spec = pl.BlockSpec((HEAD_PAIRS, Q_TILES_PER_STEP * Q_TILE, LANES),
                          lambda bi, si: (0, bi * ns + si, 0))

    def key_tile(si, ref_idx):
        return jnp.maximum(si * Q_TILES_PER_STEP - (KEY_TILES - 1) + ref_idx, 0)

    def k_spec(ref_idx):
        return pl.BlockSpec((HEAD_PAIRS, Q_TILE, LANES),
                            lambda bi, si: (0, bi * nq + key_tile(si, ref_idx), 0))

    def vt_spec(ref_idx):
        return pl.BlockSpec((HEAD_PAIRS, LANES, Q_TILE),
                            lambda bi, si: (0, 0, bi * nq + key_tile(si, ref_idx)))

    attn = pl.pallas_call(
        _attn_kernel,
        grid=(b, ns),
        in_specs=[q_spec] + [k_spec(r) for r in range(KEY_REFS)]
                 + [vt_spec(r) for r in range(KEY_REFS)]
                 + [_resident((N_HEADS, 1, BIAS_LANES))],
        out_specs=q_spec,
        out_shape=heads_shape,
        scratch_shapes=[pltpu.VMEM((N_HEADS, BAND_ROWS, LANES), _F32)],
        compiler_params=cparams(dimension_semantics=("arbitrary", "arbitrary")),
        name="attn",
    )(q, *([k] * KEY_REFS), *([vt] * KEY_REFS), rb_rows.reshape(N_HEADS, 1, BIAS_LANES))

    mlp_weights = [w.astype(_F32) for w in (w_gate, w_attn_proj, w_conv_proj, w_out, w_up, w_down)]
    out = pl.pallas_call(
        _mix_mlp_kernel,
        grid=(t // ROW_TILE,),
        in_specs=[
            pl.BlockSpec((ROW_TILE, d), lambda i: (i, 0)),
            heads_spec(ROW_TILE),
            pl.BlockSpec((ROW_TILE, d), lambda i: (i, 0)),
            _resident((1, d)),
            _resident((1, 2 * d)),
            _resident((1, d)),
        ] + [pl.BlockSpec(memory_space=pl.ANY)] * len(mlp_weights),
        out_specs=pl.BlockSpec((ROW_TILE, d), lambda i: (i, 0)),
        out_shape=jax.ShapeDtypeStruct((t, d), _F32),
        scratch_shapes=[_packed_weight(w.shape) for w in mlp_weights] + staging,
        compiler_params=cparams(dimension_semantics=("arbitrary",)),
        name="mix_mlp",
    )(xf, attn, yc, row(norm1_g), row(b_gate), row(norm2_g), *mlp_weights)
    return out.reshape(b, s, d)
```

```python
import functools
import math

import jax
import jax.numpy as jnp
from jax import lax
from jax.experimental import pallas as pl
from jax.experimental.pallas import tpu as pltpu

D_MODEL = 1024
N_HEADS = 16
HEAD_DIM = 64
CHUNK = 64
N_PREV_CHUNKS = 8
MAX_REL = 256
CONV_WIDTH = 3
D_FF = 4 * D_MODEL
EPS = 1e-6
NEG_INF = -1e30
LOG2E = math.log2(math.e)

LANES = 128
BF16_ROWS = 16
HEAD_PAIRS = D_MODEL // LANES
MXU_TILE = 256

PROJ_ROWS = 1024
ROW_TILE = 512
PROJ_TILE = 2 * MXU_TILE
CONV_TILE = MXU_TILE
Q_TILE = 256
KEY_TILES = 1 + (N_PREV_CHUNKS * CHUNK) // Q_TILE
Q_TILES_PER_STEP = 2
KEY_REFS = KEY_TILES + Q_TILES_PER_STEP - 1
BAND_ROWS = (N_PREV_CHUNKS + LANES // CHUNK) * CHUNK
BIAS_LANES = (KEY_TILES + 1) * Q_TILE
assert 2 * MAX_REL + 1 <= BIAS_LANES and Q_TILE - 1 <= MAX_REL
SCORE_LOOKAHEAD = 3
KEY_BLOCK = LANES
FF_TILE = 1024
PROJ_SUB_TILES = 4
MLP_SUB_TILES = 2
CARRY_ROWS = 8
STAGE_ROWS, STAGE_COLS = 256, D_MODEL
STAGE_SLOTS = 4
VMEM_LIMIT = 56 * 1024 * 1024

_BF16 = jnp.bfloat16
_F32 = jnp.float32
_NT_DIMS = (((1,), (1,)), ((), ()))


def _resident(shape):
    return pl.BlockSpec(shape, lambda *_: (0,) * len(shape), pipeline_mode=pl.Buffered(1))


def _rms_norm_rows(x, g):
    ms = jnp.mean(x * x, axis=-1, keepdims=True)
    return x * lax.rsqrt(ms + EPS) * g


def _packed_weight(shape):
    k, n = shape
    return pltpu.VMEM((k // 2, n), jnp.uint32)


def _pack(block):
    return pltpu.bitcast(block.astype(_BF16), jnp.uint32)


def _unpack(words):
    return pltpu.bitcast(words, _BF16)


def _stage_weights(jobs, stage, sem):
    def copy(n):
        src, r0, c0, _ = jobs[n]
        return pltpu.make_async_copy(src.at[pl.ds(r0, STAGE_ROWS), pl.ds(c0, STAGE_COLS)],
                                     stage.at[n % STAGE_SLOTS], sem.at[n % STAGE_SLOTS])

    ahead = STAGE_SLOTS - 1
    for n in range(min(ahead, len(jobs))):
        copy(n).start(priority=n % 2)
    for n, job in enumerate(jobs):
        if n + ahead < len(jobs):
            copy(n + ahead).start(priority=(n + ahead) % 2)
        copy(n).wait()
        job[3](stage[n % STAGE_SLOTS])


def _weight_jobs(src, dst, row_lo=0, col_lo=0, shape=None):
    k, n = shape if shape is not None else src.shape
    jobs = []
    for r in range(0, k, STAGE_ROWS):
        for c in range(0, n, STAGE_COLS):
            def store(block, r=r, c=c):
                dst[r // 2:(r + STAGE_ROWS) // 2, c:c + STAGE_COLS] = _pack(block)
            jobs.append((src, row_lo + r, col_lo + c, store))
    return jobs


def _proj_kernel(x_ref, g1_ref, w_in_hbm, gq_ref, gk_ref, cw_ref, cb_ref,
                 q_ref, k_ref, vt_ref, yc_ref,
                 u_scr, wqk_ref, wvt_ref, wc_ref, stage, sem, *, tiles_per_seq):
    tm = x_ref.shape[0]

    @pl.when(pl.program_id(0) == 0)
    def _():
        def store_vt(block, r):
            wvt_ref[:, r:r + STAGE_ROWS] = _pack(block.T)

        jobs = _weight_jobs(w_in_hbm, wqk_ref, 0, 0, (D_MODEL, 2 * D_MODEL))
        jobs += _weight_jobs(w_in_hbm, wc_ref, 0, 3 * D_MODEL, (D_MODEL, 3 * D_MODEL))
        jobs += [(w_in_hbm, r, 2 * D_MODEL, functools.partial(store_vt, r=r))
                 for r in range(0, D_MODEL, STAGE_ROWS)]
        _stage_weights(jobs, stage, sem)

    n_col_tiles = D_MODEL // PROJ_TILE
    slabs = PROJ_TILE // LANES

    @pl.when(pl.program_id(0) % tiles_per_seq == 0)
    def _():
        u_scr[0:CARRY_ROWS, :] = jnp.zeros((CARRY_ROWS, D_MODEL), _F32)

    def sub_tile(r0, rows):
        tok = slice(r0, r0 + rows)
        h = _rms_norm_rows(x_ref[tok, :], g1_ref[...]).astype(_BF16)
        low_half = lax.broadcasted_iota(jnp.int32, (rows, LANES), 1) < HEAD_DIM

        def proj(w_ref, j, c, width=PROJ_TILE):
            lo = j * D_MODEL + c * width
            return jnp.dot(h, _unpack(w_ref[:, lo:lo + width]), preferred_element_type=_F32)

        for c in range(D_MODEL // CONV_TILE):
            cols = slice(c * CONV_TILE, (c + 1) * CONV_TILE)
            bg = proj(wc_ref, 0, c, CONV_TILE)
            u = proj(wc_ref, 1, c, CONV_TILE) * proj(wc_ref, 2, c, CONV_TILE)
            base = CARRY_ROWS + r0
            u_scr[base:base + rows, cols] = u
            u1 = u_scr[base - 1:base - 1 + rows, cols]
            u2 = u_scr[base - 2:base - 2 + rows, cols]
            conv = (cb_ref[:, cols] + cw_ref[0:1, cols] * u2 + cw_ref[1:2, cols] * u1
                    + cw_ref[2:3, cols] * u)
            yc_ref[tok, cols] = (bg * conv).astype(_BF16)

        def finish_head_norm(p, g_ref, o_ref, c):
            g = g_ref[:, c * PROJ_TILE:(c + 1) * PROJ_TILE]
            for sl in range(slabs):
                ps = p[:, sl * LANES:(sl + 1) * LANES]
                p2 = ps * ps
                ss_lo = jnp.sum(jnp.where(low_half, p2, 0.0), axis=-1, keepdims=True)
                ss_hi = jnp.sum(jnp.where(low_half, 0.0, p2), axis=-1, keepdims=True)
                ss = jnp.where(low_half, ss_lo, ss_hi)
                pn = ps * lax.rsqrt(ss * (1.0 / HEAD_DIM) + EPS) * g[:, sl * LANES:(sl + 1) * LANES]
                o_ref[c * slabs + sl, tok, :] = pn.astype(_BF16)

        waiting = None
        for j, (g_ref, o_ref) in enumerate(((gq_ref, q_ref), (gk_ref, k_ref))):
            for c in range(n_col_tiles):
                p = proj(wqk_ref, j, c)
                if waiting is not None:
                    finish_head_norm(*waiting)
                waiting = (p, g_ref, o_ref, c)

        for c in range(n_col_tiles):
            w_rows = slice(c * PROJ_TILE // 2, (c + 1) * PROJ_TILE // 2)
            vt = lax.dot_general(_unpack(wvt_ref[w_rows, :]), h, _NT_DIMS,
                                 preferred_element_type=_F32).astype(_BF16)
            if waiting is not None:
                finish_head_norm(*waiting)
                waiting = None
            for sl in range(slabs):
                vt_ref[c * slabs + sl, :, tok] = vt[sl * LANES:(sl + 1) * LANES, :]

    rows = tm // PROJ_SUB_TILES
    for n in range(PROJ_SUB_TILES):
        sub_tile(n * rows, rows)
    u_scr[0:CARRY_ROWS, :] = u_scr[tm:tm + CARRY_ROWS, :]


def _bias_tile(rb):
    lane = lax.broadcasted_iota(jnp.int32, rb.shape, 1)
    top = rb[:, 2 * MAX_REL:2 * MAX_REL + 1]
    fwd = jnp.where(lane > 2 * MAX_REL, top, rb) * LOG2E
    rolled = pltpu.roll(jnp.broadcast_to(fwd, (BAND_ROWS, BIAS_LANES)), Q_TILE, axis=1,
                        stride=1, stride_axis=0)
    kj = lax.broadcasted_iota(jnp.int32, (BAND_ROWS, LANES), 0)
    qi = lax.broadcasted_iota(jnp.int32, (BAND_ROWS, LANES), 1)
    dchunk = (qi // CHUNK + N_PREV_CHUNKS) - kj // CHUNK
    band = (dchunk >= 0) & (dchunk <= N_PREV_CHUNKS)
    return jnp.where(band, rolled[:, :LANES], NEG_INF)


def _attn_kernel(q_ref, *refs):
    k_refs, v_refs = refs[:KEY_REFS], refs[KEY_REFS:2 * KEY_REFS]
    rb_ref, o_ref, bias_ref = refs[2 * KEY_REFS:]
    tq = Q_TILE
    lane = lax.broadcasted_iota(jnp.int32, (tq, LANES), 1)
    ones_rows = jnp.ones((BF16_ROWS, tq), _BF16)

    @pl.when((pl.program_id(0) == 0) & (pl.program_id(1) == 0))
    def _():
        def build(head, carry):
            bias_ref[head] = _bias_tile(rb_ref[head])
            return carry
        lax.fori_loop(0, N_HEADS, build, 0)

    def scores(sub, tiles, head):
        hp, hh = divmod(head, 2)
        sel = (lane < HEAD_DIM) if hh == 0 else (lane >= HEAD_DIM)
        q = q_ref[hp, sub * tq:(sub + 1) * tq, :]
        qh = jnp.where(sel, q, jnp.zeros_like(q))
        keys = jnp.concatenate([k_refs[sub + t][hp] for t in tiles], axis=0)
        return lax.dot_general(keys, qh, _NT_DIMS, preferred_element_type=_F32)

    def attend(sub, tiles, head, s):
        lo, hi = tiles[0] * tq, (tiles[-1] + 1) * tq
        hp, hh = divmod(head, 2)
        n_cols = tq // LANES
        m = [None] * n_cols
        acc = None
        for b0 in range(lo, hi, KEY_BLOCK):
            p_parts, alpha_parts = [], []
            for c in range(n_cols):
                if not (max(lo, c * LANES) <= b0 < min(hi, c * LANES + BAND_ROWS)):
                    p_parts.append(jnp.zeros((KEY_BLOCK, LANES), _BF16))
                    alpha_parts.append(jnp.ones((1, LANES), _F32))
                    continue
                sb = (s[b0 - lo:b0 - lo + KEY_BLOCK, c * LANES:(c + 1) * LANES]
                      + bias_ref[head, b0 - c * LANES:b0 - c * LANES + KEY_BLOCK, :])
                mb = jnp.max(sb, axis=0, keepdims=True)
                if m[c] is None:
                    m_new = mb
                    alpha_parts.append(jnp.ones((1, LANES), _F32))
                else:
                    m_new = jnp.maximum(m[c], mb)
                    alpha_parts.append(jnp.exp2(m[c] - m_new))
                m[c] = m_new
                p_parts.append(jnp.exp2(sb - m_new).astype(_BF16))
            t, off = divmod(b0, tq)
            v_ext = jnp.concatenate(
                [v_refs[sub + t][hp, hh * HEAD_DIM:(hh + 1) * HEAD_DIM, off:off + KEY_BLOCK],
                 ones_rows[:, :KEY_BLOCK]], axis=0)
            part = jnp.dot(v_ext, jnp.concatenate(p_parts, axis=1),
                           preferred_element_type=_F32)
            acc = part if acc is None else acc * jnp.concatenate(alpha_parts, axis=1) + part
        return acc[:HEAD_DIM] * (1.0 / acc[HEAD_DIM:HEAD_DIM + 1])

    def run(tiles_per_sub):
        work = [(sub, tiles, head) for sub, tiles in enumerate(tiles_per_sub)
                for head in range(N_HEADS)]
        scored = [scores(*w) for w in work[:SCORE_LOOKAHEAD]]
        halves = []
        for n, (sub, tiles, head) in enumerate(work):
            if n + SCORE_LOOKAHEAD < len(work):
                scored.append(scores(*work[n + SCORE_LOOKAHEAD]))
            halves.append(attend(sub, tiles, head, scored.pop(0)))
            if head % 2:
                o_ref[head // 2, sub * tq:(sub + 1) * tq, :] = (
                    jnp.concatenate(halves, axis=0).T.astype(_BF16))
                halves = []

    all_tiles = tuple(range(KEY_TILES))
    first_step = tuple(all_tiles[max(KEY_TILES - 1 - j, 0):] for j in range(Q_TILES_PER_STEP))
    qs = pl.program_id(1)
    pl.when(qs == 0)(functools.partial(run, first_step))
    pl.when(qs > 0)(functools.partial(run, (all_tiles,) * Q_TILES_PER_STEP))


def _mix_mlp_kernel(x_ref, a_ref, yc_ref, g1_ref, bgate_ref, g2_ref,
                    wg_hbm, wap_hbm, wcp_hbm, wout_hbm, wup_hbm, wdown_hbm, o_ref,
                    wg_ref, wap_ref, wcp_ref, wout_ref, wup_ref, wdown_ref, stage, sem):
    @pl.when(pl.program_id(0) == 0)
    def _():
        pairs = ((wg_hbm, wg_ref), (wap_hbm, wap_ref), (wcp_hbm, wcp_ref), (wout_hbm, wout_ref),
                 (wup_hbm, wup_ref), (wdown_hbm, wdown_ref))
        _stage_weights([job for src, dst in pairs for job in _weight_jobs(src, dst)], stage, sem)

    def dot(lhs, w_ref, rows=slice(None), cols=slice(None)):
        return jnp.dot(lhs, _unpack(w_ref[rows, cols]), preferred_element_type=_F32)

    sub_rows = x_ref.shape[0] // MLP_SUB_TILES
    subs = [slice(n * sub_rows, (n + 1) * sub_rows) for n in range(MLP_SUB_TILES)]

    def branches(rows):
        x = x_ref[rows, :]
        h = _rms_norm_rows(x, g1_ref[...]).astype(_BF16)
        gates = jax.nn.sigmoid(dot(h, wg_ref) + bgate_ref[...])
        a = jnp.concatenate([a_ref[hp, rows, :] for hp in range(HEAD_PAIRS)], axis=-1)
        ya = dot(a, wap_ref)
        yc = dot(yc_ref[rows, :], wcp_ref)
        return x, (gates[:, :D_MODEL] * ya + gates[:, D_MODEL:] * yc).astype(_BF16)

    def residual(x, merged):
        return x + dot(merged, wout_ref)

    def mlp(rows, x1):
        h2 = _rms_norm_rows(x1, g2_ref[...]).astype(_BF16)
        acc = x1
        for f in range(D_FF // FF_TILE):
            up = dot(h2, wup_ref, cols=slice(f * FF_TILE, (f + 1) * FF_TILE))
            act = jnp.square(jnp.maximum(up, 0.0)).astype(_BF16)
            acc = acc + dot(act, wdown_ref, rows=slice(f * FF_TILE // 2, (f + 1) * FF_TILE // 2))
        o_ref[rows, :] = acc

    merged = [branches(rows) for rows in subs]
    x1 = [residual(*m) for m in merged]
    for rows, v in zip(subs, x1):
        mlp(rows, v)


def kernel(x, norm1_g, w_in, q_norm_g, k_norm_g, rel_bias, conv_w, conv_b, w_attn_proj,
           w_conv_proj, w_gate, b_gate, w_out, norm2_g, w_up, w_down):
    b, s, d = x.shape
    assert d == D_MODEL and s % ROW_TILE == 0 and s % PROJ_ROWS == 0
    assert s % (Q_TILE * Q_TILES_PER_STEP) == 0
    t = b * s
    xf = x.reshape(t, d)
    row = lambda v: v.reshape(1, -1).astype(_F32)
    cparams = functools.partial(pltpu.CompilerParams, vmem_limit_bytes=VMEM_LIMIT)

    gq = row(jnp.tile(q_norm_g, N_HEADS) * (HEAD_DIM ** -0.5 * LOG2E))
    gk = row(jnp.tile(k_norm_g, N_HEADS))
    staging = [pltpu.VMEM((STAGE_SLOTS, STAGE_ROWS, STAGE_COLS), _F32),
               pltpu.SemaphoreType.DMA((STAGE_SLOTS,))]
    heads_shape = jax.ShapeDtypeStruct((HEAD_PAIRS, t, LANES), _BF16)

    def heads_spec(rows):
        return pl.BlockSpec((HEAD_PAIRS, rows, LANES), lambda i: (0, i, 0))

    q, k, vt, yc = pl.pallas_call(
        functools.partial(_proj_kernel, tiles_per_seq=s // PROJ_ROWS),
        grid=(t // PROJ_ROWS,),
        in_specs=[
            pl.BlockSpec((PROJ_ROWS, d), lambda i: (i, 0)),
            _resident((1, d)),
            pl.BlockSpec(memory_space=pl.ANY),
            _resident((1, d)),
            _resident((1, d)),
            _resident((CONV_WIDTH, d)),
            _resident((1, d)),
        ],
        out_specs=[heads_spec(PROJ_ROWS), heads_spec(PROJ_ROWS),
                   pl.BlockSpec((HEAD_PAIRS, LANES, PROJ_ROWS), lambda i: (0, 0, i)),
                   pl.BlockSpec((PROJ_ROWS, d), lambda i: (i, 0))],
        out_shape=[heads_shape, heads_shape,
                   jax.ShapeDtypeStruct((HEAD_PAIRS, LANES, t), _BF16),
                   jax.ShapeDtypeStruct((t, d), _BF16)],
        scratch_shapes=[pltpu.VMEM((PROJ_ROWS + CARRY_ROWS, d), _F32),
                        _packed_weight((d, 2 * d)), _packed_weight((d, d)),
                        _packed_weight((d, 3 * d))] + staging,
        compiler_params=cparams(dimension_semantics=("arbitrary",)),
        name="proj",
    )(xf, row(norm1_g), w_in.astype(_F32), gq, gk, conv_w.astype(_F32), row(conv_b))

    n_rel = rel_bias.shape[1]
    rb_rows = jnp.pad(rel_bias.astype(_F32), ((0, 0), (0, BIAS_LANES - n_rel)))
    nq = s // Q_TILE
    ns = nq // Q_TILES_PER_STEP
    q_spec = pl.BlockSpec((HEAD_PAIRS, Q_TILES_PER_STEP * Q_TILE, LANES),
                          lambda bi, si: (0, bi * ns + si, 0))

    def key_tile(si, ref_idx):
        return jnp.maximum(si * Q_TILES_PER_STEP - (KEY_TILES - 1) + ref_idx, 0)

    def k_spec(ref_idx):
        return pl.BlockSpec((HEAD_PAIRS, Q_TILE, LANES),
                            lambda bi, si: (0, bi * nq + key_tile(si, ref_idx), 0))

    def vt_spec(ref_idx):
        return pl.BlockSpec((HEAD_PAIRS, LANES, Q_TILE),
                            lambda bi, si: (0, 0, bi * nq + key_tile(si, ref_idx)))

    attn = pl.pallas_call(
        _attn_kernel,
        grid=(b, ns),
        in_specs=[q_spec] + [k_spec(r) for r in range(KEY_REFS)]
                 + [vt_spec(r) for r in range(KEY_REFS)]
                 + [_resident((N_HEADS, 1, BIAS_LANES))],
        out_specs=q_spec,
        out_shape=heads_shape,
        scratch_shapes=[pltpu.VMEM((N_HEADS, BAND_ROWS, LANES), _F32)],
        compiler_params=cparams(dimension_semantics=("arbitrary", "arbitrary")),
        name="attn",
    )(q, *([k] * KEY_REFS), *([vt] * KEY_REFS), rb_rows.reshape(N_HEADS, 1, BIAS_LANES))

    mlp_weights = [w.astype(_F32) for w in (w_gate, w_attn_proj, w_conv_proj, w_out, w_up, w_down)]
    out = pl.pallas_call(
        _mix_mlp_kernel,
        grid=(t // ROW_TILE,),
        in_specs=[
            pl.BlockSpec((ROW_TILE, d), lambda i: (i, 0)),
            heads_spec(ROW_TILE),
            pl.BlockSpec((ROW_TILE, d), lambda i: (i, 0)),
            _resident((1, d)),
            _resident((1, 2 * d)),
            _resident((1, d)),
        ] + [pl.BlockSpec(memory_space=pl.ANY)] * len(mlp_weights),
        out_specs=pl.BlockSpec((ROW_TILE, d), lambda i: (i, 0)),
        out_shape=jax.ShapeDtypeStruct((t, d), _F32),
        scratch_shapes=[_packed_weight(w.shape) for w in mlp_weights] + staging,
        compiler_params=cparams(dimension_semantics=("arbitrary",)),
        name="mix_mlp",
    )(xf, attn, yc, row(norm1_g), row(b_gate), row(norm2_g), *mlp_weights)
    return out.reshape(b, s, d)
```

```python
import functools
import math

import jax
import jax.numpy as jnp
from jax import lax
from jax.experimental import pallas as pl
from jax.experimental.pallas import tpu as pltpu

D_MODEL = 1024
N_HEADS = 16
HEAD_DIM = 64
CHUNK = 64
N_PREV_CHUNKS = 8
MAX_REL = 256
CONV_WIDTH = 3
D_FF = 4 * D_MODEL
EPS = 1e-6
NEG_INF = -1e30
LOG2E = math.log2(math.e)

LANES = 128
BF16_ROWS = 16
HEAD_PAIRS = D_MODEL // LANES
MXU_TILE = 256

PROJ_ROWS = 512
ROW_TILE = 512
PROJ_TILE = 2 * MXU_TILE
CONV_TILE = MXU_TILE
Q_TILE = 256
KEY_TILES = 1 + (N_PREV_CHUNKS * CHUNK) // Q_TILE
Q_TILES_PER_STEP = 2
KEY_REFS = KEY_TILES + Q_TILES_PER_STEP - 1
BAND_ROWS = (N_PREV_CHUNKS + LANES // CHUNK) * CHUNK
BIAS_LANES = (KEY_TILES + 1) * Q_TILE
assert 2 * MAX_REL + 1 <= BIAS_LANES and Q_TILE - 1 <= MAX_REL
SCORE_LOOKAHEAD = 3
KEY_BLOCK = LANES
FF_TILE = 1024
PROJ_SUB_TILES = 2
MLP_SUB_TILES = 2
CARRY_ROWS = 8
STAGE_ROWS, STAGE_COLS = 256, D_MODEL
STAGE_SLOTS = 4
MLP_STAGE_SLOTS = 10
VMEM_LIMIT = 56 * 1024 * 1024

_BF16 = jnp.bfloat16
_F32 = jnp.float32
_NT_DIMS = (((1,), (1,)), ((), ()))


def _resident(shape):
    return pl.BlockSpec(shape, lambda *_: (0,) * len(shape), pipeline_mode=pl.Buffered(1))


def _rms_norm_rows(x, g):
    ms = jnp.mean(x * x, axis=-1, keepdims=True)
    return x * lax.rsqrt(ms + EPS) * g


def _packed_weight(shape):
    k, n = shape
    return pltpu.VMEM((k // 2, n), jnp.uint32)


def _pack(block):
    return pltpu.bitcast(block.astype(_BF16), jnp.uint32)


def _unpack(words):
    return pltpu.bitcast(words, _BF16)


class _WeightStager:
    def __init__(self, jobs, stage, sem):
        self.jobs, self.stage, self.sem = jobs, stage, sem
        self.slots = stage.shape[0]
        self.done = 0
        for n in range(min(self.slots - 1, len(jobs))):
            self._copy(n).start(priority=n % 2)

    def _copy(self, n):
        src, r0, c0, _ = self.jobs[n]
        return pltpu.make_async_copy(src.at[pl.ds(r0, STAGE_ROWS), pl.ds(c0, STAGE_COLS)],
                                     self.stage.at[n % self.slots], self.sem.at[n % self.slots])

    def take(self, count):
        for n in range(self.done, self.done + count):
            ahead = n + self.slots - 1
            if ahead < len(self.jobs):
                self._copy(ahead).start(priority=ahead % 2)
            self._copy(n).wait()
            self.jobs[n][3](self.stage[n % self.slots])
        self.done += count

    def take_all(self):
        self.take(len(self.jobs) - self.done)


def _weight_jobs(src, dst, src_lo=(0, 0), shape=None, dst_lo=(0, 0)):
    k, n = shape if shape is not None else src.shape
    jobs = []
    for r in range(0, k, STAGE_ROWS):
        for c in range(0, n, STAGE_COLS):
            def store(block, r=dst_lo[0] + r, c=dst_lo[1] + c):
                dst[r // 2:(r + STAGE_ROWS) // 2, c:c + STAGE_COLS] = _pack(block)
            jobs.append((src, src_lo[0] + r, src_lo[1] + c, store))
    return jobs


def _proj_kernel(x_ref, g1_ref, w_in_hbm, gq_ref, gk_ref, cw_ref, cb_ref,
                 q_ref, k_ref, vt_ref, yc_ref,
                 u_scr, wqk_ref, wvt_ref, wc_ref, stage, sem, *, tiles_per_seq):
    tm = x_ref.shape[0]

    @pl.when(pl.program_id(0) == 0)
    def _():
        def store_vt(block, r):
            wvt_ref[:, r:r + STAGE_ROWS] = _pack(block.T)

        jobs = _weight_jobs(w_in_hbm, wqk_ref, (0, 0), (D_MODEL, 2 * D_MODEL))
        jobs += _weight_jobs(w_in_hbm, wc_ref, (0, 3 * D_MODEL), (D_MODEL, 3 * D_MODEL))
        jobs += [(w_in_hbm, r, 2 * D_MODEL, functools.partial(store_vt, r=r))
                 for r in range(0, D_MODEL, STAGE_ROWS)]
        _WeightStager(jobs, stage, sem).take_all()

    n_col_tiles = D_MODEL // PROJ_TILE
    slabs = PROJ_TILE // LANES

    @pl.when(pl.program_id(0) % tiles_per_seq == 0)
    def _():
        u_scr[0:CARRY_ROWS, :] = jnp.zeros((CARRY_ROWS, D_MODEL), _F32)

    def sub_tile(r0, rows):
        tok = slice(r0, r0 + rows)
        h = _rms_norm_rows(x_ref[tok, :], g1_ref[...]).astype(_BF16)
        low_half = lax.broadcasted_iota(jnp.int32, (rows, LANES), 1) < HEAD_DIM

        def proj(w_ref, j, c, width=PROJ_TILE):
            lo = j * D_MODEL + c * width
            return jnp.dot(h, _unpack(w_ref[:, lo:lo + width]), preferred_element_type=_F32)

        for c in range(D_MODEL // CONV_TILE):
            cols = slice(c * CONV_TILE, (c + 1) * CONV_TILE)
            bg = proj(wc_ref, 0, c, CONV_TILE)
            u = proj(wc_ref, 1, c, CONV_TILE) * proj(wc_ref, 2, c, CONV_TILE)
            base = CARRY_ROWS + r0
            u_scr[base:base + rows, cols] = u
            u1 = u_scr[base - 1:base - 1 + rows, cols]
            u2 = u_scr[base - 2:base - 2 + rows, cols]
            conv = (cb_ref[:, cols] + cw_ref[0:1, cols] * u2 + cw_ref[1:2, cols] * u1
                    + cw_ref[2:3, cols] * u)
            yc_ref[tok, cols] = (bg * conv).astype(_BF16)

        def finish_head_norm(p, g_ref, o_ref, c):
            g = g_ref[:, c * PROJ_TILE:(c + 1) * PROJ_TILE]
            for sl in range(slabs):
                ps = p[:, sl * LANES:(sl + 1) * LANES]
                p2 = ps * ps
                ss_lo = jnp.sum(jnp.where(low_half, p2, 0.0), axis=-1, keepdims=True)
                ss_hi = jnp.sum(jnp.where(low_half, 0.0, p2), axis=-1, keepdims=True)
                ss = jnp.where(low_half, ss_lo, ss_hi)
                pn = ps * lax.rsqrt(ss * (1.0 / HEAD_DIM) + EPS) * g[:, sl * LANES:(sl + 1) * LANES]
                o_ref[c * slabs + sl, tok, :] = pn.astype(_BF16)

        waiting = None
        for j, (g_ref, o_ref) in enumerate(((gq_ref, q_ref), (gk_ref, k_ref))):
            for c in range(n_col_tiles):
                p = proj(wqk_ref, j, c)
                if waiting is not None:
                    finish_head_norm(*waiting)
                waiting = (p, g_ref, o_ref, c)

        for c in range(n_col_tiles):
            w_rows = slice(c * PROJ_TILE // 2, (c + 1) * PROJ_TILE // 2)
            vt = lax.dot_general(_unpack(wvt_ref[w_rows, :]), h, _NT_DIMS,
                                 preferred_element_type=_F32).astype(_BF16)
            if waiting is not None:
                finish_head_norm(*waiting)
                waiting = None
            for sl in range(slabs):
                vt_ref[c * slabs + sl, :, tok] = vt[sl * LANES:(sl + 1) * LANES, :]

    rows = tm // PROJ_SUB_TILES
    for n in range(PROJ_SUB_TILES):
        sub_tile(n * rows, rows)
    u_scr[0:CARRY_ROWS, :] = u_scr[tm:tm + CARRY_ROWS, :]


def _bias_tile(rb):
    lane = lax.broadcasted_iota(jnp.int32, rb.shape, 1)
    top = rb[:, 2 * MAX_REL:2 * MAX_REL + 1]
    fwd = jnp.where(lane > 2 * MAX_REL, top, rb) * LOG2E
    rolled = pltpu.roll(jnp.broadcast_to(fwd, (BAND_ROWS, BIAS_LANES)), Q_TILE, axis=1,
                        stride=1, stride_axis=0)
    kj = lax.broadcasted_iota(jnp.int32, (BAND_ROWS, LANES), 0)
    qi = lax.broadcasted_iota(jnp.int32, (BAND_ROWS, LANES), 1)
    dchunk = (qi // CHUNK + N_PREV_CHUNKS) - kj // CHUNK
    band = (dchunk >= 0) & (dchunk <= N_PREV_CHUNKS)
    return jnp.where(band, rolled[:, :LANES], NEG_INF)


def _attn_kernel(q_ref, *refs):
    k_refs, v_refs = refs[:KEY_REFS], refs[KEY_REFS:2 * KEY_REFS]
    rb_ref, o_ref, bias_ref = refs[2 * KEY_REFS:]
    tq = Q_TILE
    lane = lax.broadcasted_iota(jnp.int32, (tq, LANES), 1)
    ones_rows = jnp.ones((BF16_ROWS, tq), _BF16)

    @pl.when((pl.program_id(0) == 0) & (pl.program_id(1) == 0))
    def _():
        def build(head, carry):
            bias_ref[head] = _bias_tile(rb_ref[head])
            return carry
        lax.fori_loop(0, N_HEADS, build, 0)

    def scores(sub, tiles, head):
        hp, hh = divmod(head, 2)
        sel = (lane < HEAD_DIM) if hh == 0 else (lane >= HEAD_DIM)
        q = q_ref[hp, sub * tq:(sub + 1) * tq, :]
        qh = jnp.where(sel, q, jnp.zeros_like(q))
        keys = jnp.concatenate([k_refs[sub + t][hp] for t in tiles], axis=0)
        return lax.dot_general(keys, qh, _NT_DIMS, preferred_element_type=_F32)

    def attend(sub, tiles, head, s):
        lo, hi = tiles[0] * tq, (tiles[-1] + 1) * tq
        hp, hh = divmod(head, 2)
        n_cols = tq // LANES
        m = [None] * n_cols
        acc = None
        for b0 in range(lo, hi, KEY_BLOCK):
            p_parts, alpha_parts = [], []
            for c in range(n_cols):
                if not (max(lo, c * LANES) <= b0 < min(hi, c * LANES + BAND_ROWS)):
                    p_parts.append(jnp.zeros((KEY_BLOCK, LANES), _BF16))
                    alpha_parts.append(jnp.ones((1, LANES), _F32))
                    continue
                sb = (s[b0 - lo:b0 - lo + KEY_BLOCK, c * LANES:(c + 1) * LANES]
                      + bias_ref[head, b0 - c * LANES:b0 - c * LANES + KEY_BLOCK, :])
                mb = jnp.max(sb, axis=0, keepdims=True)
                if m[c] is None:
                    m_new = mb
                    alpha_parts.append(jnp.ones((1, LANES), _F32))
                else:
                    m_new = jnp.maximum(m[c], mb)
                    alpha_parts.append(jnp.exp2(m[c] - m_new))
                m[c] = m_new
                p_parts.append(jnp.exp2(sb - m_new).astype(_BF16))
            t, off = divmod(b0, tq)
            v_ext = jnp.concatenate(
                [v_refs[sub + t][hp, hh * HEAD_DIM:(hh + 1) * HEAD_DIM, off:off + KEY_BLOCK],
                 ones_rows[:, :KEY_BLOCK]], axis=0)
            part = jnp.dot(v_ext, jnp.concatenate(p_parts, axis=1),
                           preferred_element_type=_F32)
            acc = part if acc is None else acc * jnp.concatenate(alpha_parts, axis=1) + part
        return acc[:HEAD_DIM] * (1.0 / acc[HEAD_DIM:HEAD_DIM + 1])

    def run(tiles_per_sub):
        work = [(sub, tiles, head) for sub, tiles in enumerate(tiles_per_sub)
                for head in range(N_HEADS)]
        scored = [scores(*w) for w in work[:SCORE_LOOKAHEAD]]
        halves = []
        for n, (sub, tiles, head) in enumerate(work):
            if n + SCORE_LOOKAHEAD < len(work):
                scored.append(scores(*work[n + SCORE_LOOKAHEAD]))
            halves.append(attend(sub, tiles, head, scored.pop(0)))
            if head % 2:
                o_ref[head // 2, sub * tq:(sub + 1) * tq, :] = (
                    jnp.concatenate(halves, axis=0).T.astype(_BF16))
                halves = []

    all_tiles = tuple(range(KEY_TILES))
    first_step = tuple(all_tiles[max(KEY_TILES - 1 - j, 0):] for j in range(Q_TILES_PER_STEP))
    qs = pl.program_id(1)
    pl.when(qs == 0)(functools.partial(run, first_step))
    pl.when(qs > 0)(functools.partial(run, (all_tiles,) * Q_TILES_PER_STEP))


def _mix_mlp_kernel(x_ref, a_ref, yc_ref, g1_ref, bgate_ref, g2_ref,
                    wg_hbm, wap_hbm, wcp_hbm, wout_hbm, wup_hbm, wdown_hbm, o_ref,
                    wg_ref, wap_ref, wcp_ref, wout_ref, wup_ref, wdown_ref, stage, sem):
    def dot(lhs, w_ref, rows=slice(None), cols=slice(None)):
        return jnp.dot(lhs, _unpack(w_ref[rows, cols]), preferred_element_type=_F32)

    sub_rows = x_ref.shape[0] // MLP_SUB_TILES
    subs = [slice(n * sub_rows, (n + 1) * sub_rows) for n in range(MLP_SUB_TILES)]
    ff_tiles = D_FF // FF_TILE

    def step(stager):
        def need(n_jobs):
            if stager is not None:
                stager.take(n_jobs)

        def branches(rows):
            x = x_ref[rows, :]
            h = _rms_norm_rows(x, g1_ref[...]).astype(_BF16)
            gates = jax.nn.sigmoid(dot(h, wg_ref) + bgate_ref[...])
            a = jnp.concatenate([a_ref[hp, rows, :] for hp in range(HEAD_PAIRS)], axis=-1)
            ya = dot(a, wap_ref)
            yc = dot(yc_ref[rows, :], wcp_ref)
            return x, (gates[:, :D_MODEL] * ya + gates[:, D_MODEL:] * yc).astype(_BF16)

        def residual(x, merged):
            return x + dot(merged, wout_ref)

        def mlp(rows, x1, first):
            h2 = _rms_norm_rows(x1, g2_ref[...]).astype(_BF16)
            acc = x1
            for f in range(ff_tiles):
                if first:
                    need(n_up_jobs)
                up = dot(h2, wup_ref, cols=slice(f * FF_TILE, (f + 1) * FF_TILE))
                act = jnp.square(jnp.maximum(up, 0.0)).astype(_BF16)
                if first:
                    need(n_down_jobs)
                acc = acc + dot(act, wdown_ref,
                                rows=slice(f * FF_TILE // 2, (f + 1) * FF_TILE // 2))
            o_ref[rows, :] = acc

        need(n_branch_jobs)
        merged = [branches(rows) for rows in subs]
        need(n_out_jobs)
        x1 = [residual(*m) for m in merged]
        for n, (rows, v) in enumerate(zip(subs, x1)):
            mlp(rows, v, first=n == 0)

    branch_jobs = (_weight_jobs(wg_hbm, wg_ref) + _weight_jobs(wap_hbm, wap_ref)
                   + _weight_jobs(wcp_hbm, wcp_ref))
    out_jobs = _weight_jobs(wout_hbm, wout_ref)
    ff_jobs = []
    for f in range(ff_tiles):
        up_jobs = _weight_jobs(wup_hbm, wup_ref, (0, f * FF_TILE), (D_MODEL, FF_TILE),
                               (0, f * FF_TILE))
        down_jobs = _weight_jobs(wdown_hbm, wdown_ref, (f * FF_TILE, 0), (FF_TILE, D_MODEL),
                                 (f * FF_TILE, 0))
        ff_jobs += up_jobs + down_jobs
    n_branch_jobs, n_out_jobs = len(branch_jobs), len(out_jobs)
    n_up_jobs, n_down_jobs = len(up_jobs), len(down_jobs)

    @pl.when(pl.program_id(0) == 0)
    def _():
        step(_WeightStager(branch_jobs + out_jobs + ff_jobs, stage, sem))

    @pl.when(pl.program_id(0) > 0)
    def _():
        step(None)


def kernel(x, norm1_g, w_in, q_norm_g, k_norm_g, rel_bias, conv_w, conv_b, w_attn_proj,
           w_conv_proj, w_gate, b_gate, w_out, norm2_g, w_up, w_down):
    b, s, d = x.shape
    assert d == D_MODEL and s % ROW_TILE == 0 and s % PROJ_ROWS == 0
    assert s % (Q_TILE * Q_TILES_PER_STEP) == 0
    t = b * s
    xf = x.reshape(t, d)
    row = lambda v: v.reshape(1, -1).astype(_F32)
    cparams = functools.partial(pltpu.CompilerParams, vmem_limit_bytes=VMEM_LIMIT)

    gq = row(jnp.tile(q_norm_g, N_HEADS) * (HEAD_DIM ** -0.5 * LOG2E))
    gk = row(jnp.tile(k_norm_g, N_HEADS))
    def staging(slots):
        return [pltpu.VMEM((slots, STAGE_ROWS, STAGE_COLS), _F32),
                pltpu.SemaphoreType.DMA((slots,))]
    heads_shape = jax.ShapeDtypeStruct((HEAD_PAIRS, t, LANES), _BF16)

    def heads_spec(rows):
        return pl.BlockSpec((HEAD_PAIRS, rows, LANES), lambda i: (0, i, 0))

    q, k, vt, yc = pl.pallas_call(
        functools.partial(_proj_kernel, tiles_per_seq=s // PROJ_ROWS),
        grid=(t // PROJ_ROWS,),
        in_specs=[
            pl.BlockSpec((PROJ_ROWS, d), lambda i: (i, 0)),
            _resident((1, d)),
            pl.BlockSpec(memory_space=pl.ANY),
            _resident((1, d)),
            _resident((1, d)),
            _resident((CONV_WIDTH, d)),
            _resident((1, d)),
        ],
        out_specs=[heads_spec(PROJ_ROWS), heads_spec(PROJ_ROWS),
                   pl.BlockSpec((HEAD_PAIRS, LANES, PROJ_ROWS), lambda i: (0, 0, i)),
                   pl.BlockSpec((PROJ_ROWS, d), lambda i: (i, 0))],
        out_shape=[heads_shape, heads_shape,
                   jax.ShapeDtypeStruct((HEAD_PAIRS, LANES, t), _BF16),
                   jax.ShapeDtypeStruct((t, d), _BF16)],
        scratch_shapes=[pltpu.VMEM((PROJ_ROWS + CARRY_ROWS, d), _F32),
                        _packed_weight((d, 2 * d)), _packed_weight((d, d)),
                        _packed_weight((d, 3 * d))] + staging(STAGE_SLOTS),
        compiler_params=cparams(dimension_semantics=("arbitrary",)),
        name="proj",
    )(xf, row(norm1_g), w_in.astype(_F32), gq, gk, conv_w.astype(_F32), row(conv_b))

    n_rel = rel_bias.shape[1]
    rb_rows = jnp.pad(rel_bias.astype(_F32), ((0, 0), (0, BIAS_LANES - n_rel)))
    nq = s // Q_TILE
    ns = nq // Q_TILES_PER_STEP
    q_spec = pl.BlockSpec((HEAD_PAIRS, Q_TILES_PER_STEP * Q_TILE, LANES),
                          lambda bi, si: (0, bi * ns + si, 0))

    def key_tile(si, ref_idx):
        return jnp.maximum(si * Q_TILES_PER_STEP - (KEY_TILES - 1) + ref_idx, 0)

    def k_spec(ref_idx):
        return pl.BlockSpec((HEAD_PAIRS, Q_TILE, LANES),
                            lambda bi, si: (0, bi * nq + key_tile(si, ref_idx), 0))

    def vt_spec(ref_idx):
        return pl.BlockSpec((HEAD_PAIRS, LANES, Q_TILE),
                            lambda bi, si: (0, 0, bi * nq + key_tile(si, ref_idx)))

    attn = pl.pallas_call(
        _attn_kernel,
        grid=(b, ns),
        in_specs=[q_spec] + [k_spec(r) for r in range(KEY_REFS)]
                 + [vt_spec(r) for r in range(KEY_REFS)]
                 + [_resident((N_HEADS, 1, BIAS_LANES))],
        out_specs=q_spec,
        out_shape=heads_shape,
        scratch_shapes=[pltpu.VMEM((N_HEADS, BAND_ROWS, LANES), _F32)],
        compiler_params=cparams(dimension_semantics=("arbitrary", "arbitrary")),
        name="attn",
    )(q, *([k] * KEY_REFS), *([vt] * KEY_REFS), rb_rows.reshape(N_HEADS, 1, BIAS_LANES))

    mlp_weights = [w.astype(_F32) for w in (w_gate, w_attn_proj, w_conv_proj, w_out, w_up, w_down)]
    out = pl.pallas_call(
        _mix_mlp_kernel,
        grid=(t // ROW_TILE,),
        in_specs=[
            pl.BlockSpec((ROW_TILE, d), lambda i: (i, 0)),
            heads_spec(ROW_TILE),
            pl.BlockSpec((ROW_TILE, d), lambda i: (i, 0)),
            _resident((1, d)),
            _resident((1, 2 * d)),
            _resident((1, d)),
        ] + [pl.BlockSpec(memory_space=pl.ANY)] * len(mlp_weights),
        out_specs=pl.BlockSpec((ROW_TILE, d), lambda i: (i, 0)),
        out_shape=jax.ShapeDtypeStruct((t, d), _F32),
        scratch_shapes=[_packed_weight(w.shape) for w in mlp_weights] + staging(MLP_STAGE_SLOTS),
        compiler_params=cparams(dimension_semantics=("arbitrary",)),
        name="mix_mlp",
    )(xf, attn, yc, row(norm1_g), row(b_gate), row(norm2_g), *mlp_weights)
    return out.reshape(b, s, d)
```

```python
import functools
import math

import jax
import jax.numpy as jnp
from jax import lax
from jax.experimental import pallas as pl
from jax.experimental.pallas import tpu as pltpu

D_MODEL = 1024
N_HEADS = 16
HEAD_DIM = 64
CHUNK = 64
N_PREV_CHUNKS = 8
MAX_REL = 256
CONV_WIDTH = 3
D_FF = 4 * D_MODEL
EPS = 1e-6
NEG_INF = -1e30
LOG2E = math.log2(math.e)

LANES = 128
BF16_ROWS = 16
HEAD_PAIRS = D_MODEL // LANES
MXU_TILE = 256

PROJ_ROWS = 512
ROW_TILE = 512
PROJ_TILE = 2 * MXU_TILE
CONV_TILE = MXU_TILE
Q_TILE = 256
KEY_TILES = 1 + (N_PREV_CHUNKS * CHUNK) // Q_TILE
Q_TILES_PER_STEP = 2
KEY_REFS = KEY_TILES + Q_TILES_PER_STEP - 1
BAND_ROWS = (N_PREV_CHUNKS + LANES // CHUNK) * CHUNK
BIAS_LANES = (KEY_TILES + 1) * Q_TILE
assert 2 * MAX_REL + 1 <= BIAS_LANES and Q_TILE - 1 <= MAX_REL
SCORE_LOOKAHEAD = 3
KEY_BLOCK = LANES
FF_TILE = 1024
PROJ_SUB_TILES = 2
MLP_SUB_TILES = 2
CARRY_ROWS = 8
STAGE_ROWS, STAGE_COLS = 256, D_MODEL
STAGE_SLOTS = 4
VMEM_LIMIT = 56 * 1024 * 1024

_BF16 = jnp.bfloat16
_F32 = jnp.float32
_NT_DIMS = (((1,), (1,)), ((), ()))


def _resident(shape):
    return pl.BlockSpec(shape, lambda *_: (0,) * len(shape), pipeline_mode=pl.Buffered(1))


def _rms_norm_rows(x, g):
    ms = jnp.mean(x * x, axis=-1, keepdims=True)
    return x * lax.rsqrt(ms + EPS) * g


def _packed_weight(shape):
    k, n = shape
    return pltpu.VMEM((k // 2, n), jnp.uint32)


def _pack(block):
    return pltpu.bitcast(block.astype(_BF16), jnp.uint32)


def _unpack(words):
    return pltpu.bitcast(words, _BF16)


def _stage_weights(jobs, stage, sem):
    def copy(n):
        src, r0, c0, _ = jobs[n]
        return pltpu.make_async_copy(src.at[pl.ds(r0, STAGE_ROWS), pl.ds(c0, STAGE_COLS)],
                                     stage.at[n % STAGE_SLOTS], sem.at[n % STAGE_SLOTS])

    ahead = STAGE_SLOTS - 1
    for n in range(min(ahead, len(jobs))):
        copy(n).start(priority=n % 2)
    for n, job in enumerate(jobs):
        if n + ahead < len(jobs):
            copy(n + ahead).start(priority=(n + ahead) % 2)
        copy(n).wait()
        job[3](stage[n % STAGE_SLOTS])


def _weight_jobs(src, dst, row_lo=0, col_lo=0, shape=None):
    k, n = shape if shape is not None else src.shape
    jobs = []
    for r in range(0, k, STAGE_ROWS):
        for c in range(0, n, STAGE_COLS):
            def store(block, r=r, c=c):
                dst[r // 2:(r + STAGE_ROWS) // 2, c:c + STAGE_COLS] = _pack(block)
            jobs.append((src, row_lo + r, col_lo + c, store))
    return jobs


def _proj_kernel(x_ref, g1_ref, w_in_hbm, gq_ref, gk_ref, cw_ref, cb_ref,
                 qt_ref, k_ref, vt_ref, yc_ref,
                 u_scr, wqt_ref, wk_ref, wvt_ref, wc_ref, stage, sem, *, tiles_per_seq):
    tm = x_ref.shape[0]

    @pl.when(pl.program_id(0) == 0)
    def _():
        def store_transposed(block, dst, r):
            dst[:, r:r + STAGE_ROWS] = _pack(block.T)

        jobs = _weight_jobs(w_in_hbm, wk_ref, 0, D_MODEL, (D_MODEL, D_MODEL))
        jobs += _weight_jobs(w_in_hbm, wc_ref, 0, 3 * D_MODEL, (D_MODEL, 3 * D_MODEL))
        for col, dst in ((0, wqt_ref), (2 * D_MODEL, wvt_ref)):
            jobs += [(w_in_hbm, r, col, functools.partial(store_transposed, dst=dst, r=r))
                     for r in range(0, D_MODEL, STAGE_ROWS)]
        _stage_weights(jobs, stage, sem)

    n_col_tiles = D_MODEL // PROJ_TILE
    slabs = PROJ_TILE // LANES

    @pl.when(pl.program_id(0) % tiles_per_seq == 0)
    def _():
        u_scr[0:CARRY_ROWS, :] = jnp.zeros((CARRY_ROWS, D_MODEL), _F32)

    def sub_tile(r0, rows):
        tok = slice(r0, r0 + rows)
        h = _rms_norm_rows(x_ref[tok, :], g1_ref[...]).astype(_BF16)
        low_half = lax.broadcasted_iota(jnp.int32, (rows, LANES), 1) < HEAD_DIM

        def proj(w_ref, j, c, width=PROJ_TILE):
            lo = j * D_MODEL + c * width
            return jnp.dot(h, _unpack(w_ref[:, lo:lo + width]), preferred_element_type=_F32)

        for c in range(D_MODEL // CONV_TILE):
            cols = slice(c * CONV_TILE, (c + 1) * CONV_TILE)
            bg = proj(wc_ref, 0, c, CONV_TILE)
            u = proj(wc_ref, 1, c, CONV_TILE) * proj(wc_ref, 2, c, CONV_TILE)
            base = CARRY_ROWS + r0
            u_scr[base:base + rows, cols] = u
            u1 = u_scr[base - 1:base - 1 + rows, cols]
            u2 = u_scr[base - 2:base - 2 + rows, cols]
            conv = (cb_ref[:, cols] + cw_ref[0:1, cols] * u2 + cw_ref[1:2, cols] * u1
                    + cw_ref[2:3, cols] * u)
            yc_ref[tok, cols] = (bg * conv).astype(_BF16)

        def finish_head_norm(p, c):
            g = gk_ref[:, c * PROJ_TILE:(c + 1) * PROJ_TILE]
            for sl in range(slabs):
                ps = p[:, sl * LANES:(sl + 1) * LANES]
                p2 = ps * ps
                ss_lo = jnp.sum(jnp.where(low_half, p2, 0.0), axis=-1, keepdims=True)
                ss_hi = jnp.sum(jnp.where(low_half, 0.0, p2), axis=-1, keepdims=True)
                ss = jnp.where(low_half, ss_lo, ss_hi)
                pn = ps * lax.rsqrt(ss * (1.0 / HEAD_DIM) + EPS) * g[:, sl * LANES:(sl + 1) * LANES]
                k_ref[c * slabs + sl, tok, :] = pn.astype(_BF16)

        waiting = None
        for c in range(n_col_tiles):
            p = proj(wk_ref, 0, c)
            if waiting is not None:
                finish_head_norm(*waiting)
            waiting = (p, c)

        def transposed_proj(wt_ref, c):
            w_rows = slice(c * PROJ_TILE // 2, (c + 1) * PROJ_TILE // 2)
            return lax.dot_general(_unpack(wt_ref[w_rows, :]), h, _NT_DIMS,
                                   preferred_element_type=_F32)

        for c in range(n_col_tiles):
            qt = transposed_proj(wqt_ref, c)
            if waiting is not None:
                finish_head_norm(*waiting)
                waiting = None
            for sl in range(slabs):
                halves = []
                for hh in range(LANES // HEAD_DIM):
                    d0 = sl * LANES + hh * HEAD_DIM
                    blk = qt[d0:d0 + HEAD_DIM, :]
                    ss = jnp.sum(blk * blk, axis=0, keepdims=True)
                    g = gq_ref[c * PROJ_TILE + d0:c * PROJ_TILE + d0 + HEAD_DIM, :]
                    halves.append(blk * lax.rsqrt(ss * (1.0 / HEAD_DIM) + EPS)
                                  * jnp.concatenate([g] * (rows // LANES), axis=1))
                qt_ref[c * slabs + sl, :, tok] = jnp.concatenate(halves, axis=0).astype(_BF16)

        for c in range(n_col_tiles):
            vt = transposed_proj(wvt_ref, c).astype(_BF16)
            for sl in range(slabs):
                vt_ref[c * slabs + sl, :, tok] = vt[sl * LANES:(sl + 1) * LANES, :]

    rows = tm // PROJ_SUB_TILES
    for n in range(PROJ_SUB_TILES):
        sub_tile(n * rows, rows)
    u_scr[0:CARRY_ROWS, :] = u_scr[tm:tm + CARRY_ROWS, :]


def _bias_tile(rb):
    lane = lax.broadcasted_iota(jnp.int32, rb.shape, 1)
    top = rb[:, 2 * MAX_REL:2 * MAX_REL + 1]
    fwd = jnp.where(lane > 2 * MAX_REL, top, rb) * LOG2E
    rolled = pltpu.roll(jnp.broadcast_to(fwd, (BAND_ROWS, BIAS_LANES)), Q_TILE, axis=1,
                        stride=1, stride_axis=0)
    kj = lax.broadcasted_iota(jnp.int32, (BAND_ROWS, LANES), 0)
    qi = lax.broadcasted_iota(jnp.int32, (BAND_ROWS, LANES), 1)
    dchunk = (qi // CHUNK + N_PREV_CHUNKS) - kj // CHUNK
    band = (dchunk >= 0) & (dchunk <= N_PREV_CHUNKS)
    return jnp.where(band, rolled[:, :LANES], NEG_INF)


def _attn_kernel(qt_ref, *refs):
    k_refs, v_refs = refs[:KEY_REFS], refs[KEY_REFS:2 * KEY_REFS]
    rb_ref, o_ref, bias_ref = refs[2 * KEY_REFS:]
    tq = Q_TILE
    dim = lax.broadcasted_iota(jnp.int32, (LANES, tq), 0)
    ones_rows = jnp.ones((BF16_ROWS, tq), _BF16)

    @pl.when((pl.program_id(0) == 0) & (pl.program_id(1) == 0))
    def _():
        def build(head, carry):
            bias_ref[head] = _bias_tile(rb_ref[head])
            return carry
        lax.fori_loop(0, N_HEADS, build, 0)

    def scores(sub, tiles, head):
        hp, hh = divmod(head, 2)
        sel = (dim < HEAD_DIM) if hh == 0 else (dim >= HEAD_DIM)
        qt = qt_ref[hp, :, sub * tq:(sub + 1) * tq]
        qh = jnp.where(sel, qt, jnp.zeros_like(qt))
        keys = jnp.concatenate([k_refs[sub + t][hp] for t in tiles], axis=0)
        return jnp.dot(keys, qh, preferred_element_type=_F32)

    def attend(sub, tiles, head, s):
        lo, hi = tiles[0] * tq, (tiles[-1] + 1) * tq
        hp, hh = divmod(head, 2)
        n_cols = tq // LANES
        m = [None] * n_cols
        acc = None
        for b0 in range(lo, hi, KEY_BLOCK):
            p_parts, alpha_parts = [], []
            for c in range(n_cols):
                if not (max(lo, c * LANES) <= b0 < min(hi, c * LANES + BAND_ROWS)):
                    p_parts.append(jnp.zeros((KEY_BLOCK, LANES), _BF16))
                    alpha_parts.append(jnp.ones((1, LANES), _F32))
                    continue
                sb = (s[b0 - lo:b0 - lo + KEY_BLOCK, c * LANES:(c + 1) * LANES]
                      + bias_ref[head, b0 - c * LANES:b0 - c * LANES + KEY_BLOCK, :])
                mb = jnp.max(sb, axis=0, keepdims=True)
                if m[c] is None:
                    m_new = mb
                    alpha_parts.append(jnp.ones((1, LANES), _F32))
                else:
                    m_new = jnp.maximum(m[c], mb)
                    alpha_parts.append(jnp.exp2(m[c] - m_new))
                m[c] = m_new
                p_parts.append(jnp.exp2(sb - m_new).astype(_BF16))
            t, off = divmod(b0, tq)
            v_ext = jnp.concatenate(
                [v_refs[sub + t][hp, hh * HEAD_DIM:(hh + 1) * HEAD_DIM, off:off + KEY_BLOCK],
                 ones_rows[:, :KEY_BLOCK]], axis=0)
            part = jnp.dot(v_ext, jnp.concatenate(p_parts, axis=1),
                           preferred_element_type=_F32)
            acc = part if acc is None else acc * jnp.concatenate(alpha_parts, axis=1) + part
        return acc[:HEAD_DIM] * (1.0 / acc[HEAD_DIM:HEAD_DIM + 1])

    def run(tiles_per_sub):
        work = [(sub, tiles, head) for sub, tiles in enumerate(tiles_per_sub)
                for head in range(N_HEADS)]
        scored = [scores(*w) for w in work[:SCORE_LOOKAHEAD]]
        halves = []
        for n, (sub, tiles, head) in enumerate(work):
            if n + SCORE_LOOKAHEAD < len(work):
                scored.append(scores(*work[n + SCORE_LOOKAHEAD]))
            halves.append(attend(sub, tiles, head, scored.pop(0)))
            if head % 2:
                o_ref[head // 2, sub * tq:(sub + 1) * tq, :] = (
                    jnp.concatenate(halves, axis=0).T.astype(_BF16))
                halves = []

    all_tiles = tuple(range(KEY_TILES))
    first_step = tuple(all_tiles[max(KEY_TILES - 1 - j, 0):] for j in range(Q_TILES_PER_STEP))
    qs = pl.program_id(1)
    pl.when(qs == 0)(functools.partial(run, first_step))
    pl.when(qs > 0)(functools.partial(run, (all_tiles,) * Q_TILES_PER_STEP))


def _mix_mlp_kernel(x_ref, a_ref, yc_ref, g1_ref, bgate_ref, g2_ref,
                    wg_hbm, wap_hbm, wcp_hbm, wout_hbm, wup_hbm, wdown_hbm, o_ref,
                    wg_ref, wap_ref, wcp_ref, wout_ref, wup_ref, wdown_ref, stage, sem):
    @pl.when(pl.program_id(0) == 0)
    def _():
        pairs = ((wg_hbm, wg_ref), (wap_hbm, wap_ref), (wcp_hbm, wcp_ref), (wout_hbm, wout_ref),
                 (wup_hbm, wup_ref), (wdown_hbm, wdown_ref))
        _stage_weights([job for src, dst in pairs for job in _weight_jobs(src, dst)], stage, sem)

    def dot(lhs, w_ref, rows=slice(None), cols=slice(None)):
        return jnp.dot(lhs, _unpack(w_ref[rows, cols]), preferred_element_type=_F32)

    sub_rows = x_ref.shape[0] // MLP_SUB_TILES
    subs = [slice(n * sub_rows, (n + 1) * sub_rows) for n in range(MLP_SUB_TILES)]

    def branches(rows):
        x = x_ref[rows, :]
        h = _rms_norm_rows(x, g1_ref[...]).astype(_BF16)
        gates = jax.nn.sigmoid(dot(h, wg_ref) + bgate_ref[...])
        a = jnp.concatenate([a_ref[hp, rows, :] for hp in range(HEAD_PAIRS)], axis=-1)
        ya = dot(a, wap_ref)
        yc = dot(yc_ref[rows, :], wcp_ref)
        return x, (gates[:, :D_MODEL] * ya + gates[:, D_MODEL:] * yc).astype(_BF16)

    def residual(x, merged):
        return x + dot(merged, wout_ref)

    def mlp(rows, x1):
        h2 = _rms_norm_rows(x1, g2_ref[...]).astype(_BF16)
        acc = x1
        for f in range(D_FF // FF_TILE):
            up = dot(h2, wup_ref, cols=slice(f * FF_TILE, (f + 1) * FF_TILE))
            act = jnp.square(jnp.maximum(up, 0.0)).astype(_BF16)
            acc = acc + dot(act, wdown_ref, rows=slice(f * FF_TILE // 2, (f + 1) * FF_TILE // 2))
        o_ref[rows, :] = acc

    merged = [branches(rows) for rows in subs]
    x1 = [residual(*m) for m in merged]
    for rows, v in zip(subs, x1):
        mlp(rows, v)


def kernel(x, norm1_g, w_in, q_norm_g, k_norm_g, rel_bias, conv_w, conv_b, w_attn_proj,
           w_conv_proj, w_gate, b_gate, w_out, norm2_g, w_up, w_down):
    b, s, d = x.shape
    assert d == D_MODEL and s % ROW_TILE == 0 and s % PROJ_ROWS == 0
    assert s % (Q_TILE * Q_TILES_PER_STEP) == 0
    t = b * s
    xf = x.reshape(t, d)
    row = lambda v: v.reshape(1, -1).astype(_F32)
    cparams = functools.partial(pltpu.CompilerParams, vmem_limit_bytes=VMEM_LIMIT)

    gq = jnp.tile(q_norm_g.astype(_F32), N_HEADS) * (HEAD_DIM ** -0.5 * LOG2E)
    gq = jnp.broadcast_to(gq[:, None], (d, LANES))
    gk = row(jnp.tile(k_norm_g, N_HEADS))
    staging = [pltpu.VMEM((STAGE_SLOTS, STAGE_ROWS, STAGE_COLS), _F32),
               pltpu.SemaphoreType.DMA((STAGE_SLOTS,))]
    heads_shape = jax.ShapeDtypeStruct((HEAD_PAIRS, t, LANES), _BF16)

    heads_t_shape = jax.ShapeDtypeStruct((HEAD_PAIRS, LANES, t), _BF16)

    def heads_spec(rows):
        return pl.BlockSpec((HEAD_PAIRS, rows, LANES), lambda i: (0, i, 0))

    def heads_t_spec(rows):
        return pl.BlockSpec((HEAD_PAIRS, LANES, rows), lambda i: (0, 0, i))

    qt, k, vt, yc = pl.pallas_call(
        functools.partial(_proj_kernel, tiles_per_seq=s // PROJ_ROWS),
        grid=(t // PROJ_ROWS,),
        in_specs=[
            pl.BlockSpec((PROJ_ROWS, d), lambda i: (i, 0)),
            _resident((1, d)),
            pl.BlockSpec(memory_space=pl.ANY),
            _resident((d, LANES)),
            _resident((1, d)),
            _resident((CONV_WIDTH, d)),
            _resident((1, d)),
        ],
        out_specs=[heads_t_spec(PROJ_ROWS), heads_spec(PROJ_ROWS), heads_t_spec(PROJ_ROWS),
                   pl.BlockSpec((PROJ_ROWS, d), lambda i: (i, 0))],
        out_shape=[heads_t_shape, heads_shape, heads_t_shape,
                   jax.ShapeDtypeStruct((t, d), _BF16)],
        scratch_shapes=[pltpu.VMEM((PROJ_ROWS + CARRY_ROWS, d), _F32),
                        _packed_weight((d, d)), _packed_weight((d, d)), _packed_weight((d, d)),
                        _packed_weight((d, 3 * d))] + staging,
        compiler_params=cparams(dimension_semantics=("arbitrary",)),
        name="proj",
    )(xf, row(norm1_g), w_in.astype(_F32), gq, gk, conv_w.astype(_F32), row(conv_b))

    n_rel = rel_bias.shape[1]
    rb_rows = jnp.pad(rel_bias.astype(_F32), ((0, 0), (0, BIAS_LANES - n_rel)))
    nq = s // Q_TILE
    ns = nq // Q_TILES_PER_STEP
    qt_spec = pl.BlockSpec((HEAD_PAIRS, LANES, Q_TILES_PER_STEP * Q_TILE),
                           lambda bi, si: (0, 0, bi * ns + si))
    o_spec = pl.BlockSpec((HEAD_PAIRS, Q_TILES_PER_STEP * Q_TILE, LANES),
                          lambda bi, si: (0, bi * ns + si, 0))

    def key_tile(si, ref_idx):
        return jnp.maximum(si * Q_TILES_PER_STEP - (KEY_TILES - 1) + ref_idx, 0)

    def k_spec(ref_idx):
        return pl.BlockSpec((HEAD_PAIRS, Q_TILE, LANES),
                            lambda bi, si: (0, bi * nq + key_tile(si, ref_idx), 0))

    def vt_spec(ref_idx):
        return pl.BlockSpec((HEAD_PAIRS, LANES, Q_TILE),
                            lambda bi, si: (0, 0, bi * nq + key_tile(si, ref_idx)))

    attn = pl.pallas_call(
        _attn_kernel,
        grid=(b, ns),
        in_specs=[qt_spec] + [k_spec(r) for r in range(KEY_REFS)]
                 + [vt_spec(r) for r in range(KEY_REFS)]
                 + [_resident((N_HEADS, 1, BIAS_LANES))],
        out_specs=o_spec,
        out_shape=heads_shape,
        scratch_shapes=[pltpu.VMEM((N_HEADS, BAND_ROWS, LANES), _F32)],
        compiler_params=cparams(dimension_semantics=("arbitrary", "arbitrary")),
        name="attn",
    )(qt, *([k] * KEY_REFS), *([vt] * KEY_REFS), rb_rows.reshape(N_HEADS, 1, BIAS_LANES))

    mlp_weights = [w.astype(_F32) for w in (w_gate, w_attn_proj, w_conv_proj, w_out, w_up, w_down)]
    out = pl.pallas_call(
        _mix_mlp_kernel,
        grid=(t // ROW_TILE,),
        in_specs=[
            pl.BlockSpec((ROW_TILE, d), lambda i: (i, 0)),
            heads_spec(ROW_TILE),
            pl.BlockSpec((ROW_TILE, d), lambda i: (i, 0)),
            _resident((1, d)),
            _resident((1, 2 * d)),
            _resident((1, d)),
        ] + [pl.BlockSpec(memory_space=pl.ANY)] * len(mlp_weights),
        out_specs=pl.BlockSpec((ROW_TILE, d), lambda i: (i, 0)),
        out_shape=jax.ShapeDtypeStruct((t, d), _F32),
        scratch_shapes=[_packed_weight(w.shape) for w in mlp_weights] + staging,
        compiler_params=cparams(dimension_semantics=("arbitrary",)),
        name="mix_mlp",
    )(xf, attn, yc, row(norm1_g), row(b_gate), row(norm2_g), *mlp_weights)
    return out.reshape(b, s, d)
```

```python
import functools
import math

import jax
import jax.numpy as jnp
from jax import lax
from jax.experimental import pallas as pl
from jax.experimental.pallas import tpu as pltpu

D_MODEL = 1024
N_HEADS = 16
HEAD_DIM = 64
CHUNK = 64
N_PREV_CHUNKS = 8
MAX_REL = 256
CONV_WIDTH = 3
D_FF = 4 * D_MODEL
EPS = 1e-6
NEG_INF = -1e30
LOG2E = math.log2(math.e)

LANES = 128
BF16_ROWS = 16
HEAD_PAIRS = D_MODEL // LANES
MXU_TILE = 256

PROJ_ROWS = 512
ROW_TILE = 512
PROJ_TILE = 2 * MXU_TILE
CONV_TILE = MXU_TILE
Q_TILE = 256
KEY_TILES = 1 + (N_PREV_CHUNKS * CHUNK) // Q_TILE
Q_TILES_PER_STEP = 2
KEY_REFS = KEY_TILES + Q_TILES_PER_STEP - 1
BAND_ROWS = (N_PREV_CHUNKS + LANES // CHUNK) * CHUNK
BIAS_LANES = (KEY_TILES + 1) * Q_TILE
assert 2 * MAX_REL + 1 <= BIAS_LANES and Q_TILE - 1 <= MAX_REL
SCORE_LOOKAHEAD = 3
KEY_BLOCK = LANES
FF_TILE = 1024
PROJ_SUB_TILES = 2
MLP_SUB_TILES = 2
CARRY_ROWS = 8
STAGE_ROWS, STAGE_COLS = 256, D_MODEL
STAGE_SLOTS = 4
VMEM_LIMIT = 56 * 1024 * 1024

_BF16 = jnp.bfloat16
_F32 = jnp.float32
_NT_DIMS = (((1,), (1,)), ((), ()))


def _resident(shape):
    return pl.BlockSpec(shape, lambda *_: (0,) * len(shape), pipeline_mode=pl.Buffered(1))


def _rms_norm_rows(x, g):
    ms = jnp.mean(x * x, axis=-1, keepdims=True)
    return x * lax.rsqrt(ms + EPS) * g


def _packed_weight(shape):
    k, n = shape
    return pltpu.VMEM((k // 2, n), jnp.uint32)


def _pack(block):
    return pltpu.bitcast(block.astype(_BF16), jnp.uint32)


def _unpack(words):
    return pltpu.bitcast(words, _BF16)


def _stage_weights(jobs, stage, sem):
    def copy(n):
        src, r0, c0, _ = jobs[n]
        return pltpu.make_async_copy(src.at[pl.ds(r0, STAGE_ROWS), pl.ds(c0, STAGE_COLS)],
                                     stage.at[n % STAGE_SLOTS], sem.at[n % STAGE_SLOTS])

    ahead = STAGE_SLOTS - 1
    for n in range(min(ahead, len(jobs))):
        copy(n).start(priority=n % 2)
    for n, job in enumerate(jobs):
        if n + ahead < len(jobs):
            copy(n + ahead).start(priority=(n + ahead) % 2)
        copy(n).wait()
        job[3](stage[n % STAGE_SLOTS])


def _weight_jobs(src, dst, row_lo=0, col_lo=0, shape=None):
    k, n = shape if shape is not None else src.shape
    jobs = []
    for r in range(0, k, STAGE_ROWS):
        for c in range(0, n, STAGE_COLS):
            def store(block, r=r, c=c):
                dst[r // 2:(r + STAGE_ROWS) // 2, c:c + STAGE_COLS] = _pack(block)
            jobs.append((src, row_lo + r, col_lo + c, store))
    return jobs


def _proj_kernel(x_ref, g1_ref, w_in_hbm, gq_ref, gk_ref, cw_ref, cb_ref,
                 qt_ref, k_ref, vt_ref, yc_ref,
                 u_scr, wqt_ref, wk_ref, wvt_ref, wc_ref, stage, sem, *, tiles_per_seq):
    tm = x_ref.shape[0]

    @pl.when(pl.program_id(0) == 0)
    def _():
        def store_transposed(block, dst, r):
            dst[:, r:r + STAGE_ROWS] = _pack(block.T)

        jobs = _weight_jobs(w_in_hbm, wk_ref, 0, D_MODEL, (D_MODEL, D_MODEL))
        jobs += _weight_jobs(w_in_hbm, wc_ref, 0, 3 * D_MODEL, (D_MODEL, 3 * D_MODEL))
        for col, dst in ((0, wqt_ref), (2 * D_MODEL, wvt_ref)):
            jobs += [(w_in_hbm, r, col, functools.partial(store_transposed, dst=dst, r=r))
                     for r in range(0, D_MODEL, STAGE_ROWS)]
        _stage_weights(jobs, stage, sem)

    n_col_tiles = D_MODEL // PROJ_TILE
    slabs = PROJ_TILE // LANES

    @pl.when(pl.program_id(0) % tiles_per_seq == 0)
    def _():
        u_scr[0:CARRY_ROWS, :] = jnp.zeros((CARRY_ROWS, D_MODEL), _F32)

    def sub_tile(r0, rows):
        tok = slice(r0, r0 + rows)
        h = _rms_norm_rows(x_ref[tok, :], g1_ref[...]).astype(_BF16)
        low_half = lax.broadcasted_iota(jnp.int32, (rows, LANES), 1) < HEAD_DIM

        def proj(w_ref, j, c, width=PROJ_TILE):
            lo = j * D_MODEL + c * width
            return jnp.dot(h, _unpack(w_ref[:, lo:lo + width]), preferred_element_type=_F32)

        for c in range(D_MODEL // CONV_TILE):
            cols = slice(c * CONV_TILE, (c + 1) * CONV_TILE)
            bg = proj(wc_ref, 0, c, CONV_TILE)
            u = proj(wc_ref, 1, c, CONV_TILE) * proj(wc_ref, 2, c, CONV_TILE)
            base = CARRY_ROWS + r0
            u_scr[base:base + rows, cols] = u
            u1 = u_scr[base - 1:base - 1 + rows, cols]
            u2 = u_scr[base - 2:base - 2 + rows, cols]
            conv = (cb_ref[:, cols] + cw_ref[0:1, cols] * u2 + cw_ref[1:2, cols] * u1
                    + cw_ref[2:3, cols] * u)
            yc_ref[tok, cols] = (bg * conv).astype(_BF16)

        def finish_head_norm(p, c):
            g = gk_ref[:, c * PROJ_TILE:(c + 1) * PROJ_TILE]
            for sl in range(slabs):
                ps = p[:, sl * LANES:(sl + 1) * LANES]
                p2 = ps * ps
                ss_lo = jnp.sum(jnp.where(low_half, p2, 0.0), axis=-1, keepdims=True)
                ss_hi = jnp.sum(jnp.where(low_half, 0.0, p2), axis=-1, keepdims=True)
                ss = jnp.where(low_half, ss_lo, ss_hi)
                pn = ps * lax.rsqrt(ss * (1.0 / HEAD_DIM) + EPS) * g[:, sl * LANES:(sl + 1) * LANES]
                k_ref[c * slabs + sl, tok, :] = pn.astype(_BF16)

        waiting = None
        for c in range(n_col_tiles):
            p = proj(wk_ref, 0, c)
            if waiting is not None:
                finish_head_norm(*waiting)
            waiting = (p, c)

        def transposed_proj(wt_ref, c):
            w_rows = slice(c * PROJ_TILE // 2, (c + 1) * PROJ_TILE // 2)
            return lax.dot_general(_unpack(wt_ref[w_rows, :]), h, _NT_DIMS,
                                   preferred_element_type=_F32)

        for c in range(n_col_tiles):
            qt = transposed_proj(wqt_ref, c)
            if waiting is not None:
                finish_head_norm(*waiting)
                waiting = None
            for sl in range(slabs):
                halves = []
                for hh in range(LANES // HEAD_DIM):
                    d0 = sl * LANES + hh * HEAD_DIM
                    blk = qt[d0:d0 + HEAD_DIM, :]
                    ss = jnp.sum(blk * blk, axis=0, keepdims=True)
                    g = gq_ref[c * PROJ_TILE + d0:c * PROJ_TILE + d0 + HEAD_DIM, :]
                    halves.append(blk * lax.rsqrt(ss * (1.0 / HEAD_DIM) + EPS)
                                  * jnp.concatenate([g] * (rows // LANES), axis=1))
                qt_ref[c * slabs + sl, :, tok] = jnp.concatenate(halves, axis=0).astype(_BF16)

        for c in range(n_col_tiles):
            vt = transposed_proj(wvt_ref, c).astype(_BF16)
            for sl in range(slabs):
                vt_ref[c * slabs + sl, :, tok] = vt[sl * LANES:(sl + 1) * LANES, :]

    rows = tm // PROJ_SUB_TILES
    for n in range(PROJ_SUB_TILES):
        sub_tile(n * rows, rows)
    u_scr[0:CARRY_ROWS, :] = u_scr[tm:tm + CARRY_ROWS, :]


def _bias_tile(rb):
    lane = lax.broadcasted_iota(jnp.int32, rb.shape, 1)
    top = rb[:, 2 * MAX_REL:2 * MAX_REL + 1]
    fwd = jnp.where(lane > 2 * MAX_REL, top, rb) * LOG2E
    rolled = pltpu.roll(jnp.broadcast_to(fwd, (BAND_ROWS, BIAS_LANES)), Q_TILE, axis=1,
                        stride=1, stride_axis=0)
    kj = lax.broadcasted_iota(jnp.int32, (BAND_ROWS, LANES), 0)
    qi = lax.broadcasted_iota(jnp.int32, (BAND_ROWS, LANES), 1)
    dchunk = (qi // CHUNK + N_PREV_CHUNKS) - kj // CHUNK
    band = (dchunk >= 0) & (dchunk <= N_PREV_CHUNKS)
    return jnp.where(band, rolled[:, :LANES], NEG_INF)


def _attn_kernel(qt_ref, *refs):
    k_refs, v_refs = refs[:KEY_REFS], refs[KEY_REFS:2 * KEY_REFS]
    rb_ref, o_ref, bias_ref = refs[2 * KEY_REFS:]
    tq = Q_TILE
    dim = lax.broadcasted_iota(jnp.int32, (LANES, tq), 0)
    ones_rows = jnp.ones((BF16_ROWS, tq), _BF16)

    @pl.when((pl.program_id(0) == 0) & (pl.program_id(1) == 0))
    def _():
        def build(head, carry):
            bias_ref[head] = _bias_tile(rb_ref[head])
            return carry
        lax.fori_loop(0, N_HEADS, build, 0)

    def scores(sub, tiles, head):
        hp, hh = divmod(head, 2)
        sel = (dim < HEAD_DIM) if hh == 0 else (dim >= HEAD_DIM)
        qt = qt_ref[hp, :, sub * tq:(sub + 1) * tq]
        qh = jnp.where(sel, qt, jnp.zeros_like(qt))
        keys = jnp.concatenate([k_refs[sub + t][hp] for t in tiles], axis=0)
        return jnp.dot(keys, qh, preferred_element_type=_F32)

    def attend(sub, tiles, head, s):
        lo, hi = tiles[0] * tq, (tiles[-1] + 1) * tq
        hp, hh = divmod(head, 2)
        n_cols = tq // LANES
        m = [None] * n_cols
        acc = None
        for b0 in range(lo, hi, KEY_BLOCK):
            p_parts, alpha_parts = [], []
            for c in range(n_cols):
                if not (max(lo, c * LANES) <= b0 < min(hi, c * LANES + BAND_ROWS)):
                    p_parts.append(jnp.zeros((KEY_BLOCK, LANES), _BF16))
                    alpha_parts.append(jnp.ones((1, LANES), _F32))
                    continue
                sb = (s[b0 - lo:b0 - lo + KEY_BLOCK, c * LANES:(c + 1) * LANES]
                      + bias_ref[head, b0 - c * LANES:b0 - c * LANES + KEY_BLOCK, :])
                mb = jnp.max(sb, axis=0, keepdims=True)
                if m[c] is None:
                    m_new = mb
                    alpha_parts.append(jnp.ones((1, LANES), _F32))
                else:
                    m_new = jnp.maximum(m[c], mb)
                    alpha_parts.append(jnp.exp2(m[c] - m_new))
                m[c] = m_new
                p_parts.append(jnp.exp2(sb - m_new).astype(_BF16))
            t, off = divmod(b0, tq)
            v_ext = jnp.concatenate(
                [v_refs[sub + t][hp, hh * HEAD_DIM:(hh + 1) * HEAD_DIM, off:off + KEY_BLOCK],
                 ones_rows[:, :KEY_BLOCK]], axis=0)
            part = jnp.dot(v_ext, jnp.concatenate(p_parts, axis=1),
                           preferred_element_type=_F32)
            acc = part if acc is None else acc * jnp.concatenate(alpha_parts, axis=1) + part
        return acc[:HEAD_DIM] * (1.0 / acc[HEAD_DIM:HEAD_DIM + 1])

    def run(tiles_per_sub):
        work = [(sub, tiles, head) for sub, tiles in enumerate(tiles_per_sub)
                for head in range(N_HEADS)]
        scored = [scores(*w) for w in work[:SCORE_LOOKAHEAD]]
        halves = []
        for n, (sub, tiles, head) in enumerate(work):
            if n + SCORE_LOOKAHEAD < len(work):
                scored.append(scores(*work[n + SCORE_LOOKAHEAD]))
            halves.append(attend(sub, tiles, head, scored.pop(0)))
            if head % 2:
                o_ref[head // 2, :, sub * tq:(sub + 1) * tq] = (
                    jnp.concatenate(halves, axis=0).astype(_BF16))
                halves = []

    all_tiles = tuple(range(KEY_TILES))
    first_step = tuple(all_tiles[max(KEY_TILES - 1 - j, 0):] for j in range(Q_TILES_PER_STEP))
    qs = pl.program_id(1)
    pl.when(qs == 0)(functools.partial(run, first_step))
    pl.when(qs > 0)(functools.partial(run, (all_tiles,) * Q_TILES_PER_STEP))


def _mix_mlp_kernel(x_ref, a_ref, yc_ref, g1_ref, bgate_ref, g2_ref,
                    wg_hbm, wap_hbm, wcp_hbm, wout_hbm, wup_hbm, wdown_hbm, o_ref,
                    wg_ref, wap_ref, wcp_ref, wout_ref, wup_ref, wdown_ref, stage, sem):
    @pl.when(pl.program_id(0) == 0)
    def _():
        pairs = ((wg_hbm, wg_ref), (wap_hbm, wap_ref), (wcp_hbm, wcp_ref), (wout_hbm, wout_ref),
                 (wup_hbm, wup_ref), (wdown_hbm, wdown_ref))
        _stage_weights([job for src, dst in pairs for job in _weight_jobs(src, dst)], stage, sem)

    def dot(lhs, w_ref, rows=slice(None), cols=slice(None)):
        return jnp.dot(lhs, _unpack(w_ref[rows, cols]), preferred_element_type=_F32)

    sub_rows = x_ref.shape[0] // MLP_SUB_TILES
    subs = [slice(n * sub_rows, (n + 1) * sub_rows) for n in range(MLP_SUB_TILES)]

    def branches(rows):
        x = x_ref[rows, :]
        h = _rms_norm_rows(x, g1_ref[...]).astype(_BF16)
        gates = jax.nn.sigmoid(dot(h, wg_ref) + bgate_ref[...])
        a = jnp.concatenate([a_ref[hp, :, rows].T for hp in range(HEAD_PAIRS)], axis=-1)
        ya = dot(a, wap_ref)
        yc = dot(yc_ref[rows, :], wcp_ref)
        return x, (gates[:, :D_MODEL] * ya + gates[:, D_MODEL:] * yc).astype(_BF16)

    def residual(x, merged):
        return x + dot(merged, wout_ref)

    def mlp(rows, x1):
        h2 = _rms_norm_rows(x1, g2_ref[...]).astype(_BF16)
        acc = x1
        for f in range(D_FF // FF_TILE):
            up = dot(h2, wup_ref, cols=slice(f * FF_TILE, (f + 1) * FF_TILE))
            act = jnp.square(jnp.maximum(up, 0.0)).astype(_BF16)
            acc = acc + dot(act, wdown_ref, rows=slice(f * FF_TILE // 2, (f + 1) * FF_TILE // 2))
        o_ref[rows, :] = acc

    merged = [branches(rows) for rows in subs]
    x1 = [residual(*m) for m in merged]
    for rows, v in zip(subs, x1):
        mlp(rows, v)


def kernel(x, norm1_g, w_in, q_norm_g, k_norm_g, rel_bias, conv_w, conv_b, w_attn_proj,
           w_conv_proj, w_gate, b_gate, w_out, norm2_g, w_up, w_down):
    b, s, d = x.shape
    assert d == D_MODEL and s % ROW_TILE == 0 and s % PROJ_ROWS == 0
    assert s % (Q_TILE * Q_TILES_PER_STEP) == 0
    t = b * s
    xf = x.reshape(t, d)
    row = lambda v: v.reshape(1, -1).astype(_F32)
    cparams = functools.partial(pltpu.CompilerParams, vmem_limit_bytes=VMEM_LIMIT)

    gq = jnp.tile(q_norm_g.astype(_F32), N_HEADS) * (HEAD_DIM ** -0.5 * LOG2E)
    gq = jnp.broadcast_to(gq[:, None], (d, LANES))
    gk = row(jnp.tile(k_norm_g, N_HEADS))
    staging = [pltpu.VMEM((STAGE_SLOTS, STAGE_ROWS, STAGE_COLS), _F32),
               pltpu.SemaphoreType.DMA((STAGE_SLOTS,))]
    heads_shape = jax.ShapeDtypeStruct((HEAD_PAIRS, t, LANES), _BF16)

    heads_t_shape = jax.ShapeDtypeStruct((HEAD_PAIRS, LANES, t), _BF16)

    def heads_spec(rows):
        return pl.BlockSpec((HEAD_PAIRS, rows, LANES), lambda i: (0, i, 0))

    def heads_t_spec(rows):
        return pl.BlockSpec((HEAD_PAIRS, LANES, rows), lambda i: (0, 0, i))

    qt, k, vt, yc = pl.pallas_call(
        functools.partial(_proj_kernel, tiles_per_seq=s // PROJ_ROWS),
        grid=(t // PROJ_ROWS,),
        in_specs=[
            pl.BlockSpec((PROJ_ROWS, d), lambda i: (i, 0)),
            _resident((1, d)),
            pl.BlockSpec(memory_space=pl.ANY),
            _resident((d, LANES)),
            _resident((1, d)),
            _resident((CONV_WIDTH, d)),
            _resident((1, d)),
        ],
        out_specs=[heads_t_spec(PROJ_ROWS), heads_spec(PROJ_ROWS), heads_t_spec(PROJ_ROWS),
                   pl.BlockSpec((PROJ_ROWS, d), lambda i: (i, 0))],
        out_shape=[heads_t_shape, heads_shape, heads_t_shape,
                   jax.ShapeDtypeStruct((t, d), _BF16)],
        scratch_shapes=[pltpu.VMEM((PROJ_ROWS + CARRY_ROWS, d), _F32),
                        _packed_weight((d, d)), _packed_weight((d, d)), _packed_weight((d, d)),
                        _packed_weight((d, 3 * d))] + staging,
        compiler_params=cparams(dimension_semantics=("arbitrary",)),
        name="proj",
    )(xf, row(norm1_g), w_in.astype(_F32), gq, gk, conv_w.astype(_F32), row(conv_b))

    n_rel = rel_bias.shape[1]
    rb_rows = jnp.pad(rel_bias.astype(_F32), ((0, 0), (0, BIAS_LANES - n_rel)))
    nq = s // Q_TILE
    ns = nq // Q_TILES_PER_STEP
    qt_spec = pl.BlockSpec((HEAD_PAIRS, LANES, Q_TILES_PER_STEP * Q_TILE),
                           lambda bi, si: (0, 0, bi * ns + si))

    def key_tile(si, ref_idx):
        return jnp.maximum(si * Q_TILES_PER_STEP - (KEY_TILES - 1) + ref_idx, 0)

    def k_spec(ref_idx):
        return pl.BlockSpec((HEAD_PAIRS, Q_TILE, LANES),
                            lambda bi, si: (0, bi * nq + key_tile(si, ref_idx), 0))

    def vt_spec(ref_idx):
        return pl.BlockSpec((HEAD_PAIRS, LANES, Q_TILE),
                            lambda bi, si: (0, 0, bi * nq + key_tile(si, ref_idx)))

    attn = pl.pallas_call(
        _attn_kernel,
        grid=(b, ns),
        in_specs=[qt_spec] + [k_spec(r) for r in range(KEY_REFS)]
                 + [vt_spec(r) for r in range(KEY_REFS)]
                 + [_resident((N_HEADS, 1, BIAS_LANES))],
        out_specs=qt_spec,
        out_shape=heads_t_shape,
        scratch_shapes=[pltpu.VMEM((N_HEADS, BAND_ROWS, LANES), _F32)],
        compiler_params=cparams(dimension_semantics=("arbitrary", "arbitrary")),
        name="attn",
    )(qt, *([k] * KEY_REFS), *([vt] * KEY_REFS), rb_rows.reshape(N_HEADS, 1, BIAS_LANES))

    mlp_weights = [w.astype(_F32) for w in (w_gate, w_attn_proj, w_conv_proj, w_out, w_up, w_down)]
    out = pl.pallas_call(
        _mix_mlp_kernel,
        grid=(t // ROW_TILE,),
        in_specs=[
            pl.BlockSpec((ROW_TILE, d), lambda i: (i, 0)),
            heads_t_spec(ROW_TILE),
            pl.BlockSpec((ROW_TILE, d), lambda i: (i, 0)),
            _resident((1, d)),
            _resident((1, 2 * d)),
            _resident((1, d)),
        ] + [pl.BlockSpec(memory_space=pl.ANY)] * len(mlp_weights),
        out_specs=pl.BlockSpec((ROW_TILE, d), lambda i: (i, 0)),
        out_shape=jax.ShapeDtypeStruct((t, d), _F32),
        scratch_shapes=[_packed_weight(w.shape) for w in mlp_weights] + staging,
        compiler_params=cparams(dimension_semantics=("arbitrary",)),
        name="mix_mlp",
    )(xf, attn, yc, row(norm1_g), row(b_gate), row(norm2_g), *mlp_weights)
    return out.reshape(b, s, d)
```

```python
import functools
import math

import jax
import jax.numpy as jnp
from jax import lax
from jax.experimental import pallas as pl
from jax.experimental.pallas import tpu as pltpu

D_MODEL = 1024
N_HEADS = 16
HEAD_DIM = 64
CHUNK = 64
N_PREV_CHUNKS = 8
MAX_REL = 256
CONV_WIDTH = 3
D_FF = 4 * D_MODEL
EPS = 1e-6
NEG_INF = -1e30
LOG2E = math.log2(math.e)

LANES = 128
BF16_ROWS = 16
HEAD_PAIRS = D_MODEL // LANES
MXU_TILE = 256

PROJ_ROWS = 512
ROW_TILE = 512
PROJ_TILE = 2 * MXU_TILE
CONV_TILE = MXU_TILE
Q_TILE = 256
KEY_TILES = 1 + (N_PREV_CHUNKS * CHUNK) // Q_TILE
Q_TILES_PER_STEP = 2
KEY_REFS = KEY_TILES + Q_TILES_PER_STEP - 1
BAND_ROWS = (N_PREV_CHUNKS + LANES // CHUNK) * CHUNK
BIAS_LANES = (KEY_TILES + 1) * Q_TILE
assert 2 * MAX_REL + 1 <= BIAS_LANES and Q_TILE - 1 <= MAX_REL
SCORE_LOOKAHEAD = 4
KEY_BLOCK = LANES
FF_TILE = 1024
PROJ_SUB_TILES = 2
MLP_SUB_TILES = 2
CARRY_ROWS = 8
STAGE_ROWS, STAGE_COLS = 256, D_MODEL
STAGE_SLOTS = 4
VMEM_LIMIT = 56 * 1024 * 1024

_BF16 = jnp.bfloat16
_F32 = jnp.float32
_NT_DIMS = (((1,), (1,)), ((), ()))


def _resident(shape):
    return pl.BlockSpec(shape, lambda *_: (0,) * len(shape), pipeline_mode=pl.Buffered(1))


def _rms_norm_rows(x, g):
    ms = jnp.mean(x * x, axis=-1, keepdims=True)
    return x * lax.rsqrt(ms + EPS) * g


def _packed_weight(shape):
    k, n = shape
    return pltpu.VMEM((k // 2, n), jnp.uint32)


def _pack(block):
    return pltpu.bitcast(block.astype(_BF16), jnp.uint32)


def _unpack(words):
    return pltpu.bitcast(words, _BF16)


def _stage_weights(jobs, stage, sem):
    def copy(n):
        src, r0, c0, _ = jobs[n]
        return pltpu.make_async_copy(src.at[pl.ds(r0, STAGE_ROWS), pl.ds(c0, STAGE_COLS)],
                                     stage.at[n % STAGE_SLOTS], sem.at[n % STAGE_SLOTS])

    ahead = STAGE_SLOTS - 1
    for n in range(min(ahead, len(jobs))):
        copy(n).start(priority=n % 2)
    for n, job in enumerate(jobs):
        if n + ahead < len(jobs):
            copy(n + ahead).start(priority=(n + ahead) % 2)
        copy(n).wait()
        job[3](stage[n % STAGE_SLOTS])


def _weight_jobs(src, dst, row_lo=0, col_lo=0, shape=None):
    k, n = shape if shape is not None else src.shape
    jobs = []
    for r in range(0, k, STAGE_ROWS):
        for c in range(0, n, STAGE_COLS):
            def store(block, r=r, c=c):
                dst[r // 2:(r + STAGE_ROWS) // 2, c:c + STAGE_COLS] = _pack(block)
            jobs.append((src, row_lo + r, col_lo + c, store))
    return jobs


def _proj_kernel(x_ref, g1_ref, w_in_hbm, gq_ref, gk_ref, cw_ref, cb_ref,
                 qt_ref, k_ref, vt_ref, yc_ref,
                 u_scr, wqt_ref, wk_ref, wvt_ref, wc_ref, stage, sem, *, tiles_per_seq):
    tm = x_ref.shape[0]

    @pl.when(pl.program_id(0) == 0)
    def _():
        def store_transposed(block, dst, r):
            dst[:, r:r + STAGE_ROWS] = _pack(block.T)

        jobs = _weight_jobs(w_in_hbm, wk_ref, 0, D_MODEL, (D_MODEL, D_MODEL))
        jobs += _weight_jobs(w_in_hbm, wc_ref, 0, 3 * D_MODEL, (D_MODEL, 3 * D_MODEL))
        for col, dst in ((0, wqt_ref), (2 * D_MODEL, wvt_ref)):
            jobs += [(w_in_hbm, r, col, functools.partial(store_transposed, dst=dst, r=r))
                     for r in range(0, D_MODEL, STAGE_ROWS)]
        _stage_weights(jobs, stage, sem)

    n_col_tiles = D_MODEL // PROJ_TILE
    slabs = PROJ_TILE // LANES

    @pl.when(pl.program_id(0) % tiles_per_seq == 0)
    def _():
        u_scr[0:CARRY_ROWS, :] = jnp.zeros((CARRY_ROWS, D_MODEL), _F32)

    def sub_tile(r0, rows):
        tok = slice(r0, r0 + rows)
        h = _rms_norm_rows(x_ref[tok, :], g1_ref[...]).astype(_BF16)
        low_half = lax.broadcasted_iota(jnp.int32, (rows, LANES), 1) < HEAD_DIM

        def proj(w_ref, j, c, width=PROJ_TILE):
            lo = j * D_MODEL + c * width
            return jnp.dot(h, _unpack(w_ref[:, lo:lo + width]), preferred_element_type=_F32)

        for c in range(D_MODEL // CONV_TILE):
            cols = slice(c * CONV_TILE, (c + 1) * CONV_TILE)
            bg = proj(wc_ref, 0, c, CONV_TILE)
            u = proj(wc_ref, 1, c, CONV_TILE) * proj(wc_ref, 2, c, CONV_TILE)
            base = CARRY_ROWS + r0
            u_scr[base:base + rows, cols] = u
            u1 = u_scr[base - 1:base - 1 + rows, cols]
            u2 = u_scr[base - 2:base - 2 + rows, cols]
            conv = (cb_ref[:, cols] + cw_ref[0:1, cols] * u2 + cw_ref[1:2, cols] * u1
                    + cw_ref[2:3, cols] * u)
            yc_ref[tok, cols] = (bg * conv).astype(_BF16)

        def finish_head_norm(p, c):
            g = gk_ref[:, c * PROJ_TILE:(c + 1) * PROJ_TILE]
            for sl in range(slabs):
                ps = p[:, sl * LANES:(sl + 1) * LANES]
                p2 = ps * ps
                ss_lo = jnp.sum(jnp.where(low_half, p2, 0.0), axis=-1, keepdims=True)
                ss_hi = jnp.sum(jnp.where(low_half, 0.0, p2), axis=-1, keepdims=True)
                ss = jnp.where(low_half, ss_lo, ss_hi)
                pn = ps * lax.rsqrt(ss * (1.0 / HEAD_DIM) + EPS) * g[:, sl * LANES:(sl + 1) * LANES]
                k_ref[c * slabs + sl, tok, :] = pn.astype(_BF16)

        waiting = None
        for c in range(n_col_tiles):
            p = proj(wk_ref, 0, c)
            if waiting is not None:
                finish_head_norm(*waiting)
            waiting = (p, c)

        def transposed_proj(wt_ref, c):
            w_rows = slice(c * PROJ_TILE // 2, (c + 1) * PROJ_TILE // 2)
            return lax.dot_general(_unpack(wt_ref[w_rows, :]), h, _NT_DIMS,
                                   preferred_element_type=_F32)

        for c in range(n_col_tiles):
            qt = transposed_proj(wqt_ref, c)
            if waiting is not None:
                finish_head_norm(*waiting)
                waiting = None
            for sl in range(slabs):
                halves = []
                for hh in range(LANES // HEAD_DIM):
                    d0 = sl * LANES + hh * HEAD_DIM
                    blk = qt[d0:d0 + HEAD_DIM, :]
                    ss = jnp.sum(blk * blk, axis=0, keepdims=True)
                    g = gq_ref[c * PROJ_TILE + d0:c * PROJ_TILE + d0 + HEAD_DIM, :]
                    halves.append(blk * lax.rsqrt(ss * (1.0 / HEAD_DIM) + EPS)
                                  * jnp.concatenate([g] * (rows // LANES), axis=1))
                qt_ref[c * slabs + sl, :, tok] = jnp.concatenate(halves, axis=0).astype(_BF16)

        for c in range(n_col_tiles):
            vt = transposed_proj(wvt_ref, c).astype(_BF16)
            for sl in range(slabs):
                vt_ref[c * slabs + sl, :, tok] = vt[sl * LANES:(sl + 1) * LANES, :]

    rows = tm // PROJ_SUB_TILES
    for n in range(PROJ_SUB_TILES):
        sub_tile(n * rows, rows)
    u_scr[0:CARRY_ROWS, :] = u_scr[tm:tm + CARRY_ROWS, :]


def _bias_tile(rb):
    lane = lax.broadcasted_iota(jnp.int32, rb.shape, 1)
    top = rb[:, 2 * MAX_REL:2 * MAX_REL + 1]
    fwd = jnp.where(lane > 2 * MAX_REL, top, rb) * LOG2E
    rolled = pltpu.roll(jnp.broadcast_to(fwd, (BAND_ROWS, BIAS_LANES)), Q_TILE, axis=1,
                        stride=1, stride_axis=0)
    kj = lax.broadcasted_iota(jnp.int32, (BAND_ROWS, LANES), 0)
    qi = lax.broadcasted_iota(jnp.int32, (BAND_ROWS, LANES), 1)
    dchunk = (qi // CHUNK + N_PREV_CHUNKS) - kj // CHUNK
    band = (dchunk >= 0) & (dchunk <= N_PREV_CHUNKS)
    return jnp.where(band, rolled[:, :LANES], NEG_INF)


def _attn_kernel(qt_ref, *refs):
    k_refs, v_refs = refs[:KEY_REFS], refs[KEY_REFS:2 * KEY_REFS]
    rb_ref, o_ref, bias_ref = refs[2 * KEY_REFS:]
    tq = Q_TILE
    dim = lax.broadcasted_iota(jnp.int32, (LANES, tq), 0)
    ones_rows = jnp.ones((BF16_ROWS, tq), _BF16)

    @pl.when((pl.program_id(0) == 0) & (pl.program_id(1) == 0))
    def _():
        def build(head, carry):
            bias_ref[head] = _bias_tile(rb_ref[head])
            return carry
        lax.fori_loop(0, N_HEADS, build, 0)

    def scores(sub, tiles, head):
        hp, hh = divmod(head, 2)
        sel = (dim < HEAD_DIM) if hh == 0 else (dim >= HEAD_DIM)
        qt = qt_ref[hp, :, sub * tq:(sub + 1) * tq]
        qh = jnp.where(sel, qt, jnp.zeros_like(qt))
        keys = jnp.concatenate([k_refs[sub + t][hp] for t in tiles], axis=0)
        return jnp.dot(keys, qh, preferred_element_type=_F32)

    def attend(sub, tiles, head, s):
        lo, hi = tiles[0] * tq, (tiles[-1] + 1) * tq
        hp, hh = divmod(head, 2)
        n_cols = tq // LANES
        m = [None] * n_cols
        acc = None
        for b0 in range(lo, hi, KEY_BLOCK):
            p_parts, alpha_parts = [], []
            for c in range(n_cols):
                if not (max(lo, c * LANES) <= b0 < min(hi, c * LANES + BAND_ROWS)):
                    p_parts.append(jnp.zeros((KEY_BLOCK, LANES), _BF16))
                    alpha_parts.append(jnp.ones((1, LANES), _F32))
                    continue
                sb = (s[b0 - lo:b0 - lo + KEY_BLOCK, c * LANES:(c + 1) * LANES]
                      + bias_ref[head, b0 - c * LANES:b0 - c * LANES + KEY_BLOCK, :])
                mb = jnp.max(sb, axis=0, keepdims=True)
                if m[c] is None:
                    m_new = mb
                    alpha_parts.append(jnp.ones((1, LANES), _F32))
                else:
                    m_new = jnp.maximum(m[c], mb)
                    alpha_parts.append(jnp.exp2(m[c] - m_new))
                m[c] = m_new
                p_parts.append(jnp.exp2(sb - m_new).astype(_BF16))
            t, off = divmod(b0, tq)
            v_ext = jnp.concatenate(
                [v_refs[sub + t][hp, hh * HEAD_DIM:(hh + 1) * HEAD_DIM, off:off + KEY_BLOCK],
                 ones_rows[:, :KEY_BLOCK]], axis=0)
            part = jnp.dot(v_ext, jnp.concatenate(p_parts, axis=1),
                           preferred_element_type=_F32)
            acc = part if acc is None else acc * jnp.concatenate(alpha_parts, axis=1) + part
        return acc[:HEAD_DIM] * (1.0 / acc[HEAD_DIM:HEAD_DIM + 1])

    def run(tiles_per_sub):
        work = [(sub, tiles, head) for sub, tiles in enumerate(tiles_per_sub)
                for head in range(N_HEADS)]
        scored = [scores(*w) for w in work[:SCORE_LOOKAHEAD]]
        halves = []
        for n, (sub, tiles, head) in enumerate(work):
            if n + SCORE_LOOKAHEAD < len(work):
                scored.append(scores(*work[n + SCORE_LOOKAHEAD]))
            halves.append(attend(sub, tiles, head, scored.pop(0)))
            if head % 2:
                o_ref[head // 2, :, sub * tq:(sub + 1) * tq] = (
                    jnp.concatenate(halves, axis=0).astype(_BF16))
                halves = []

    all_tiles = tuple(range(KEY_TILES))
    first_step = tuple(all_tiles[max(KEY_TILES - 1 - j, 0):] for j in range(Q_TILES_PER_STEP))
    qs = pl.program_id(1)
    pl.when(qs == 0)(functools.partial(run, first_step))
    pl.when(qs > 0)(functools.partial(run, (all_tiles,) * Q_TILES_PER_STEP))


def _mix_mlp_kernel(x_ref, a_ref, yc_ref, g1_ref, bgate_ref, g2_ref,
                    wg_hbm, wap_hbm, wcp_hbm, wout_hbm, wup_hbm, wdown_hbm, o_ref,
                    wg_ref, wap_ref, wcp_ref, wout_ref, wup_ref, wdown_ref, stage, sem):
    @pl.when(pl.program_id(0) == 0)
    def _():
        pairs = ((wg_hbm, wg_ref), (wap_hbm, wap_ref), (wcp_hbm, wcp_ref), (wout_hbm, wout_ref),
                 (wup_hbm, wup_ref), (wdown_hbm, wdown_ref))
        _stage_weights([job for src, dst in pairs for job in _weight_jobs(src, dst)], stage, sem)

    def dot(lhs, w_ref, rows=slice(None), cols=slice(None)):
        return jnp.dot(lhs, _unpack(w_ref[rows, cols]), preferred_element_type=_F32)

    sub_rows = x_ref.shape[0] // MLP_SUB_TILES
    subs = [slice(n * sub_rows, (n + 1) * sub_rows) for n in range(MLP_SUB_TILES)]

    def branches(rows):
        x = x_ref[rows, :]
        h = _rms_norm_rows(x, g1_ref[...]).astype(_BF16)
        gates = jax.nn.sigmoid(dot(h, wg_ref) + bgate_ref[...])
        a = jnp.concatenate([a_ref[hp, :, rows].T for hp in range(HEAD_PAIRS)], axis=-1)
        ya = dot(a, wap_ref)
        yc = dot(yc_ref[rows, :], wcp_ref)
        return x, (gates[:, :D_MODEL] * ya + gates[:, D_MODEL:] * yc).astype(_BF16)

    def residual(x, merged):
        return x + dot(merged, wout_ref)

    def mlp(rows, x1):
        h2 = _rms_norm_rows(x1, g2_ref[...]).astype(_BF16)
        acc = x1
        for f in range(D_FF // FF_TILE):
            up = dot(h2, wup_ref, cols=slice(f * FF_TILE, (f + 1) * FF_TILE))
            act = jnp.square(jnp.maximum(up, 0.0)).astype(_BF16)
            acc = acc + dot(act, wdown_ref, rows=slice(f * FF_TILE // 2, (f + 1) * FF_TILE // 2))
        o_ref[rows, :] = acc

    merged = [branches(rows) for rows in subs]
    x1 = [residual(*m) for m in merged]
    for rows, v in zip(subs, x1):
        mlp(rows, v)


def kernel(x, norm1_g, w_in, q_norm_g, k_norm_g, rel_bias, conv_w, conv_b, w_attn_proj,
           w_conv_proj, w_gate, b_gate, w_out, norm2_g, w_up, w_down):
    b, s, d = x.shape
    assert d == D_MODEL and s % ROW_TILE == 0 and s % PROJ_ROWS == 0
    assert s % (Q_TILE * Q_TILES_PER_STEP) == 0
    t = b * s
    xf = x.reshape(t, d)
    row = lambda v: v.reshape(1, -1).astype(_F32)
    cparams = functools.partial(pltpu.CompilerParams, vmem_limit_bytes=VMEM_LIMIT)

    gq = jnp.tile(q_norm_g.astype(_F32), N_HEADS) * (HEAD_DIM ** -0.5 * LOG2E)
    gq = jnp.broadcast_to(gq[:, None], (d, LANES))
    gk = row(jnp.tile(k_norm_g, N_HEADS))
    staging = [pltpu.VMEM((STAGE_SLOTS, STAGE_ROWS, STAGE_COLS), _F32),
               pltpu.SemaphoreType.DMA((STAGE_SLOTS,))]
    heads_shape = jax.ShapeDtypeStruct((HEAD_PAIRS, t, LANES), _BF16)

    heads_t_shape = jax.ShapeDtypeStruct((HEAD_PAIRS, LANES, t), _BF16)

    def heads_spec(rows):
        return pl.BlockSpec((HEAD_PAIRS, rows, LANES), lambda i: (0, i, 0))

    def heads_t_spec(rows):
        return pl.BlockSpec((HEAD_PAIRS, LANES, rows), lambda i: (0, 0, i))

    qt, k, vt, yc = pl.pallas_call(
        functools.partial(_proj_kernel, tiles_per_seq=s // PROJ_ROWS),
        grid=(t // PROJ_ROWS,),
        in_specs=[
            pl.BlockSpec((PROJ_ROWS, d), lambda i: (i, 0)),
            _resident((1, d)),
            pl.BlockSpec(memory_space=pl.ANY),
            _resident((d, LANES)),
            _resident((1, d)),
            _resident((CONV_WIDTH, d)),
            _resident((1, d)),
        ],
        out_specs=[heads_t_spec(PROJ_ROWS), heads_spec(PROJ_ROWS), heads_t_spec(PROJ_ROWS),
                   pl.BlockSpec((PROJ_ROWS, d), lambda i: (i, 0))],
        out_shape=[heads_t_shape, heads_shape, heads_t_shape,
                   jax.ShapeDtypeStruct((t, d), _BF16)],
        scratch_shapes=[pltpu.VMEM((PROJ_ROWS + CARRY_ROWS, d), _F32),
                        _packed_weight((d, d)), _packed_weight((d, d)), _packed_weight((d, d)),
                        _packed_weight((d, 3 * d))] + staging,
        compiler_params=cparams(dimension_semantics=("arbitrary",)),
        name="proj",
    )(xf, row(norm1_g), w_in.astype(_F32), gq, gk, conv_w.astype(_F32), row(conv_b))

    n_rel = rel_bias.shape[1]
    rb_rows = jnp.pad(rel_bias.astype(_F32), ((0, 0), (0, BIAS_LANES - n_rel)))
    nq = s // Q_TILE
    ns = nq // Q_TILES_PER_STEP
    qt_spec = pl.BlockSpec((HEAD_PAIRS, LANES, Q_TILES_PER_STEP * Q_TILE),
                           lambda bi, si: (0, 0, bi * ns + si))

    def key_tile(si, ref_idx):
        return jnp.maximum(si * Q_TILES_PER_STEP - (KEY_TILES - 1) + ref_idx, 0)

    def k_spec(ref_idx):
        return pl.BlockSpec((HEAD_PAIRS, Q_TILE, LANES),
                            lambda bi, si: (0, bi * nq + key_tile(si, ref_idx), 0))

    def vt_spec(ref_idx):
        return pl.BlockSpec((HEAD_PAIRS, LANES, Q_TILE),
                            lambda bi, si: (0, 0, bi * nq + key_tile(si, ref_idx)))

    attn = pl.pallas_call(
        _attn_kernel,
        grid=(b, ns),
        in_specs=[qt_spec] + [k_spec(r) for r in range(KEY_REFS)]
                 + [vt_spec(r) for r in range(KEY_REFS)]
                 + [_resident((N_HEADS, 1, BIAS_LANES))],
        out_specs=qt_spec,
        out_shape=heads_t_shape,
        scratch_shapes=[pltpu.VMEM((N_HEADS, BAND_ROWS, LANES), _F32)],
        compiler_params=cparams(dimension_semantics=("arbitrary", "arbitrary")),
        name="attn",
    )(qt, *([k] * KEY_REFS), *([vt] * KEY_REFS), rb_rows.reshape(N_HEADS, 1, BIAS_LANES))

    mlp_weights = [w.astype(_F32) for w in (w_gate, w_attn_proj, w_conv_proj, w_out, w_up, w_down)]
    out = pl.pallas_call(
        _mix_mlp_kernel,
        grid=(t // ROW_TILE,),
        in_specs=[
            pl.BlockSpec((ROW_TILE, d), lambda i: (i, 0)),
            heads_t_spec(ROW_TILE),
            pl.BlockSpec((ROW_TILE, d), lambda i: (i, 0)),
            _resident((1, d)),
            _resident((1, 2 * d)),
            _resident((1, d)),
        ] + [pl.BlockSpec(memory_space=pl.ANY)] * len(mlp_weights),
        out_specs=pl.BlockSpec((ROW_TILE, d), lambda i: (i, 0)),
        out_shape=jax.ShapeDtypeStruct((t, d), _F32),
        scratch_shapes=[_packed_weight(w.shape) for w in mlp_weights] + staging,
        compiler_params=cparams(dimension_semantics=("arbitrary",)),
        name="mix_mlp",
    )(xf, attn, yc, row(norm1_g), row(b_gate), row(norm2_g), *mlp_weights)
    return out.reshape(b, s, d)
```
